```python
import math
import jax
import jax.numpy as jnp
from jax import lax
import numpy as np

D_MODEL = 4096
BATCH = 4
SEQ = 2048
DEPTH = 2
DEC_BATCH = 8
DEC_SEQ = 1
PAST_LEN = 16384
PAGE_SIZE = 128

HEAD_DIM = 128
N_BRANCH = 4
BRANCH_WIDTH = D_MODEL // N_BRANCH
NSA_HEADS = BRANCH_WIDTH // HEAD_DIM
NSA_KV_HEADS = 2
NSA_HPG = NSA_HEADS // NSA_KV_HEADS
CMP_BLOCK = 32
CMP_STRIDE = 16
SEL_BLOCK = 64
N_SELECT = 16
WINDOW = 512
Q_BLOCK = 64
BAND = 128
REL_BUCKETS = 32
REL_MAX_DIST = 128
GLA_HEADS = 4
GLA_DK = BRANCH_WIDTH // (2 * GLA_HEADS)
GLA_DV = BRANCH_WIDTH // GLA_HEADS
GLA_GATE_RANK = 16
GLA_TAU = 16.0
HGRN_EXPAND = 128
HGRN_HEADS = BRANCH_WIDTH // HGRN_EXPAND
MLSTM_HEADS = 4
MLSTM_DQK = BRANCH_WIDTH // (2 * MLSTM_HEADS)
MLSTM_DV = BRANCH_WIDTH // MLSTM_HEADS
MLSTM_CONV = 4
CHUNK = 64
FF_DIM = ((8 * D_MODEL + 3 * 256 - 1) // (3 * 256)) * 256
EPS = 1e-6
NEG = -1e30
BIG = 1e6
IN_SIZES = (
    NSA_HEADS * HEAD_DIM, 6 * NSA_KV_HEADS * HEAD_DIM, 3 * NSA_HEADS,
    GLA_HEADS * GLA_DK, GLA_HEADS * GLA_DK, GLA_HEADS * GLA_DV, GLA_GATE_RANK, BRANCH_WIDTH,
    BRANCH_WIDTH, BRANCH_WIDTH, BRANCH_WIDTH, BRANCH_WIDTH,
    2 * MLSTM_HEADS * MLSTM_DQK, MLSTM_HEADS * MLSTM_DV, MLSTM_HEADS, MLSTM_HEADS, BRANCH_WIDTH,
    N_BRANCH * D_MODEL,
)
D_IN = sum(IN_SIZES)

kernel_name = 'hybrid_nsa_gla_hgrn2_mlstm_step'


def _split(z, sizes):
    cuts = [int(c) for c in np.cumsum(sizes)[:-1]]
    return jnp.split(z, cuts, axis=-1)


def _rmsnorm(x, g):
    xf = x.astype(jnp.float32)
    y = xf * lax.rsqrt(jnp.mean(xf * xf, axis=-1, keepdims=True) + EPS)
    return (y * g.astype(jnp.float32)).astype(x.dtype)


def _masked_softmax(s, mask):
    s = jnp.where(mask, s.astype(jnp.float32), NEG)
    return jnp.where(mask, jax.nn.softmax(s, axis=-1), 0.0)


def _t5_bucket(dist):
    d = jnp.maximum(dist, 0)
    exact = REL_BUCKETS // 2
    far = exact + (jnp.log(jnp.maximum(d, 1).astype(jnp.float32) / exact)
                   / math.log(REL_MAX_DIST / exact) * (REL_BUCKETS - exact)).astype(jnp.int32)
    return jnp.where(d < exact, d, jnp.minimum(far, REL_BUCKETS - 1))


def _nsa_compress(k_rows, v_rows, pe, w1, w2, g_kc):
    B, L, G, _ = k_rows.shape
    nc = (L - CMP_BLOCK) // CMP_STRIDE + 1
    idx = jnp.arange(nc)[:, None] * CMP_STRIDE + jnp.arange(CMP_BLOCK)[None, :]

    def phi(rows, pe_, w1_, w2_):
        blk = rows[:, idx] + pe_[None, None, :, None, :]
        flat = blk.transpose(0, 1, 3, 2, 4).reshape(B, nc, G, CMP_BLOCK * HEAD_DIM)
        return jax.nn.silu(flat @ w1_) @ w2_

    kc = _rmsnorm(phi(k_rows, pe[0], w1[0], w2[0]), g_kc)
    vc = phi(v_rows, pe[1], w1[1], w2[1])
    c_end = jnp.arange(nc) * CMP_STRIDE + CMP_BLOCK - 1
    return kc, vc, c_end


def _block_cover(nc, ns):
    c0 = jnp.arange(nc)[:, None] * CMP_STRIDE
    s0 = jnp.arange(ns)[None, :] * SEL_BLOCK
    return ((c0 <= s0 + SEL_BLOCK - 1) & (c0 + CMP_BLOCK - 1 >= s0)).astype(jnp.float32)


def _nsa_cmp_sel(q, q_pos, kc, vc, c_end, ks_t, vs_t, rel_bias, cover):
    B, Tq, G, HPG, _ = q.shape
    L = ks_t.shape[2]
    ns = cover.shape[1]
    s_c = jnp.einsum('btghd,bngd->btghn', q, kc).astype(jnp.float32)
    dist_c = q_pos[:, None] - c_end[None, :]
    bias_c = rel_bias[_t5_bucket(dist_c)].reshape(Tq, -1, G, HPG).transpose(0, 2, 3, 1)
    p_c = _masked_softmax(s_c + bias_c, (dist_c >= 0)[:, None, None, :])
    o_cmp = jnp.einsum('btghn,bngd->btghd', p_c.astype(vc.dtype), vc)
    imp = jnp.einsum('btgn,ns->btgs', p_c.sum(axis=3), cover)
    blk = jnp.arange(ns)[None, :]
    cur = (q_pos // SEL_BLOCK)[:, None]
    forced = ((blk == 0) | (blk == cur) | (blk == cur - 1))[:, None, :]
    valid = (blk <= cur)[:, None, :]
    score = jnp.where(forced, BIG, jnp.where(valid, imp, -BIG))
    n_top = min(N_SELECT, ns)
    _, top = lax.top_k(score, n_top)
    tok = (top[..., None] * SEL_BLOCK + jnp.arange(SEL_BLOCK)).reshape(B, Tq, G, n_top * SEL_BLOCK)
    tok_c = jnp.minimum(tok, L - 1)
    bi = jnp.arange(B)[:, None, None, None]
    gi = jnp.arange(G)[None, None, :, None]
    kg = ks_t[bi, gi, tok_c]
    vg = vs_t[bi, gi, tok_c]
    s_s = jnp.einsum('btghd,btgsd->btghs', q, kg).astype(jnp.float32)
    dist_s = q_pos[None, :, None, None] - tok
    bias_s = rel_bias.reshape(REL_BUCKETS, G, HPG)[_t5_bucket(dist_s), gi]
    p_s = _masked_softmax(s_s + bias_s.transpose(0, 1, 2, 4, 3), (dist_s >= 0)[:, :, :, None, :])
    o_sel = jnp.einsum('btghs,btgsd->btghd', p_s.astype(vg.dtype), vg)
    return o_cmp, o_sel


def _window_attend(q, q_pos, k, v, k_pos, rel_bias):
    N, Qb = q_pos.shape
    Kb = k_pos.shape[1]
    G, HPG = q.shape[3], q.shape[4]
    s = jnp.einsum('bnqghd,bnkgd->bnqghk', q, k).astype(jnp.float32)
    dist = q_pos[:, :, None] - k_pos[:, None, :]
    bias = rel_bias[_t5_bucket(dist)].reshape(N, Qb, Kb, G, HPG).transpose(0, 1, 3, 4, 2)
    mask = ((dist >= 0) & (dist <= WINDOW) & (k_pos[:, None, :] >= 0))[:, :, None, None, :]
    p = _masked_softmax(s + bias, mask)
    return jnp.einsum('bnqghk,bnkgd->bnqghd', p.astype(v.dtype), v)


def _gated_linear_attn(q, k, v, log_a, s0):
    B, T, H, _ = q.shape
    dv = v.shape[-1]
    C = CHUNK if T % CHUNK == 0 else T
    n = T // C

    def chunks(a):
        return a.astype(jnp.float32).reshape(B, n, C, H, a.shape[-1]).transpose(1, 0, 3, 2, 4)

    causal = jnp.tril(jnp.ones((C, C), bool))

    def step(S, inp):
        qc, kc, vc, gc = inp
        b = jnp.cumsum(gc, axis=2)
        diff = jnp.where(causal[:, :, None], b[:, :, :, None, :] - b[:, :, None, :, :], -jnp.inf)
        att = jnp.einsum('bhtk,bhsk,bhtsk->bhts', qc, kc, jnp.exp(diff))
        o = (jnp.einsum('bhtk,bhkv->bhtv', qc * jnp.exp(b), S)
             + jnp.einsum('bhts,bhsv->bhtv', att, vc))
        b_last = b[:, :, -1:, :]
        S = (jnp.exp(b_last[:, :, 0, :])[..., None] * S
             + jnp.einsum('bhsk,bhsv->bhkv', kc * jnp.exp(b_last - b), vc))
        return S, o

    S, o = lax.scan(step, s0.astype(jnp.float32), (chunks(q), chunks(k), chunks(v), chunks(log_a)))
    o = o.transpose(1, 0, 3, 2, 4).reshape(B, T, H, dv)
    return o.astype(v.dtype), S.astype(s0.dtype)


def _mlstm(q, k, v, ig, lf, C0, n0, m0):
    B, T, H, _ = q.shape
    dv = v.shape[-1]
    C = CHUNK if T % CHUNK == 0 else T
    n = T // C

    def chunks(a):
        return a.astype(jnp.float32).reshape(B, n, C, H, a.shape[-1]).transpose(1, 0, 3, 2, 4)

    def gchunks(a):
        return a.astype(jnp.float32).reshape(B, n, C, H).transpose(1, 0, 3, 2)

    causal = jnp.tril(jnp.ones((C, C), bool))

    def step(carry, inp):
        Cs, ns_, m = carry
        qc, kc, vc, ic, fc = inp
        F = jnp.cumsum(fc, axis=-1)
        logw = jnp.where(causal, F[..., :, None] - F[..., None, :] + ic[..., None, :], -jnp.inf)
        from_state = F + m[..., None]
        m_hat = jnp.maximum(from_state, logw.max(-1))
        w = jnp.exp(logw - m_hat[..., None]) * jnp.einsum('bhtk,bhsk->bhts', qc, kc)
        ws = jnp.exp(from_state - m_hat)
        num = ws[..., None] * jnp.einsum('bhtk,bhvk->bhtv', qc, Cs) + jnp.einsum('bhts,bhsv->bhtv', w, vc)
        den = ws * jnp.einsum('bhtk,bhk->bht', qc, ns_) + w.sum(-1)
        h = num / jnp.maximum(jnp.abs(den), jnp.exp(-m_hat))[..., None]
        m_new = m_hat[..., -1]
        ds = jnp.exp(F[..., -1:] - F + ic - m_new[..., None])
        dst = jnp.exp(F[..., -1] + m - m_new)
        Cs = dst[..., None, None] * Cs + jnp.einsum('bhs,bhsv,bhsk->bhvk', ds, vc, kc)
        ns_ = dst[..., None] * ns_ + jnp.einsum('bhs,bhsk->bhk', ds, kc)
        return (Cs, ns_, m_new), h

    carry0 = (C0.astype(jnp.float32), n0.astype(jnp.float32), m0.astype(jnp.float32))
    (Cf, nf, mf), h = lax.scan(step, carry0, (chunks(q), chunks(k), chunks(v), gchunks(ig), gchunks(lf)))
    h = h.transpose(1, 0, 3, 2, 4).reshape(B, T, H, dv)
    return h.astype(v.dtype), Cf.astype(C0.dtype), nf.astype(n0.dtype), mf.astype(m0.dtype)


def _causal_conv(u, buf, w, b):
    T = u.shape[1]
    up = jnp.concatenate([buf, u], axis=1)
    y = b
    for j in range(MLSTM_CONV):
        y = y + up[:, j:j + T] * w[j]
    return jax.nn.silu(y), up[:, -(MLSTM_CONV - 1):]


def _token_mixers(h, pos0, past, p, rel_bias, lb):
    B, T, _ = h.shape
    G, HPG, HD = NSA_KV_HEADS, NSA_HPG, HEAD_DIM
    (nq, nkv, ngt, gq, gk, gv, ga, gr, hq, hf, hi, hg,
     mqk, mv, mi, mf, mo, mg) = _split(h @ p['w_in'], IN_SIZES)
    q_pos = pos0 + jnp.arange(T)

    q = _rmsnorm(nq.reshape(B, T, G, HPG, HD), p['nsa_gq']) * (HD ** -0.5)
    kv = nkv.reshape(B, T, 6, G, HD)
    rows_new = jnp.stack([kv[:, :, 0], kv[:, :, 1], _rmsnorm(kv[:, :, 2], p['nsa_gk'][1]), kv[:, :, 3]], axis=2)
    win_new = jnp.stack([_rmsnorm(kv[:, :, 4], p['nsa_gk'][2]), kv[:, :, 5]], axis=2)
    rows = rows_new if past is None else jnp.concatenate([past['kv'], rows_new], axis=1)
    kc, vc, c_end = _nsa_compress(rows[:, :, 0], rows[:, :, 1], p['cmp_pe'], p['cmp_w1'], p['cmp_w2'], p['nsa_gk'][0])
    L = rows.shape[1]
    cover = _block_cover(kc.shape[1], -(-L // SEL_BLOCK))
    ks_t = rows[:, :, 2].transpose(0, 2, 1, 3)
    vs_t = rows[:, :, 3].transpose(0, 2, 1, 3)

    def attend(args):
        return _nsa_cmp_sel(args[0], args[1], kc, vc, c_end, ks_t, vs_t, rel_bias, cover)

    if past is None:
        nqb = T // Q_BLOCK
        o_c, o_s = lax.map(attend, (q.reshape(B, nqb, Q_BLOCK, G, HPG, HD).swapaxes(0, 1),
                                    q_pos.reshape(nqb, Q_BLOCK)))
        o_c = o_c.swapaxes(0, 1).reshape(B, T, G, HPG, HD)
        o_s = o_s.swapaxes(0, 1).reshape(B, T, G, HPG, HD)
        nb = T // BAND
        kw = jnp.pad(win_new, ((0, 0), (WINDOW, 0), (0, 0), (0, 0), (0, 0)))
        idx = jnp.arange(nb)[:, None] * BAND + jnp.arange(BAND + WINDOW)[None, :]
        band = kw[:, idx]
        o_w = _window_attend(q.reshape(B, nb, BAND, G, HPG, HD), q_pos.reshape(nb, BAND),
                             band[:, :, :, 0], band[:, :, :, 1], idx - WINDOW, rel_bias).reshape(B, T, G, HPG, HD)
        win_state = win_new[:, -min(WINDOW, T):]
    else:
        o_c, o_s = attend((q, q_pos))
        wb = past['win'].shape[1]
        allw = jnp.concatenate([past['win'], win_new], axis=1)
        k_pos = (pos0 - wb + jnp.arange(wb + T))[None, :]
        o_w = _window_attend(q[:, None], q_pos[None, :], allw[:, None, :, 0], allw[:, None, :, 1], k_pos, rel_bias)[:, 0]
        win_state = allw[:, -wb:]
    gts = jax.nn.sigmoid(ngt.astype(jnp.float32)).reshape(B, T, 3, G, HPG, 1).astype(h.dtype)
    o_nsa = (gts[:, :, 0] * o_c + gts[:, :, 1] * o_s + gts[:, :, 2] * o_w).reshape(B, T, -1)

    g_q = gq.reshape(B, T, GLA_HEADS, GLA_DK) * (GLA_DK ** -0.5)
    g_k = gk.reshape(B, T, GLA_HEADS, GLA_DK)
    g_v = gv.reshape(B, T, GLA_HEADS, GLA_DV)
    log_a = (jax.nn.log_sigmoid((ga @ p['gla_wa'] + p['gla_ba']).astype(jnp.float32)) / GLA_TAU).reshape(B, T, GLA_HEADS, GLA_DK)
    s0 = jnp.zeros((B, GLA_HEADS, GLA_DK, GLA_DV), h.dtype) if past is None else past['gla']
    o, gla_state = _gated_linear_attn(g_q, g_k, g_v, log_a, s0)
    o_gla = (_rmsnorm(o, p['gla_gn']) * jax.nn.silu(gr.reshape(B, T, GLA_HEADS, GLA_DV))).reshape(B, T, -1)

    log_f = jnp.logaddexp(jnp.log(lb), jnp.log1p(-lb) + jax.nn.log_sigmoid(hf.astype(jnp.float32)))
    log_f = log_f.reshape(B, T, HGRN_HEADS, HGRN_EXPAND)
    h_q = jax.nn.silu(hq).reshape(B, T, HGRN_HEADS, HGRN_EXPAND)
    h_v = hi.reshape(B, T, HGRN_HEADS, HGRN_EXPAND)
    s0h = jnp.zeros((B, HGRN_HEADS, HGRN_EXPAND, HGRN_EXPAND), h.dtype) if past is None else past['hgrn']
    o, hgrn_state = _gated_linear_attn(h_q, -jnp.expm1(log_f), h_v, log_f, s0h)
    o_hgrn = (_rmsnorm(o, p['hgrn_gn']) * jax.nn.sigmoid(hg.reshape(B, T, HGRN_HEADS, HGRN_EXPAND))).reshape(B, T, -1)

    nqk = MLSTM_HEADS * MLSTM_DQK
    buf = jnp.zeros((B, MLSTM_CONV - 1, 2 * nqk), mqk.dtype) if past is None else past['conv']
    u, conv_state = _causal_conv(mqk, buf, p['m_wconv'], p['m_bconv'])
    m_q = u[..., :nqk].reshape(B, T, MLSTM_HEADS, MLSTM_DQK)
    m_k = u[..., nqk:].reshape(B, T, MLSTM_HEADS, MLSTM_DQK) * (MLSTM_DQK ** -0.5)
    m_v = mv.reshape(B, T, MLSTM_HEADS, MLSTM_DV)
    ig = (mi + p['m_bi']).astype(jnp.float32)
    lf = jax.nn.log_sigmoid((mf + p['m_bf']).astype(jnp.float32))
    if past is None:
        C0 = jnp.zeros((B, MLSTM_HEADS, MLSTM_DV, MLSTM_DQK), h.dtype)
        n0 = jnp.zeros((B, MLSTM_HEADS, MLSTM_DQK), h.dtype)
        m0 = jnp.zeros((B, MLSTM_HEADS), h.dtype)
    else:
        C0, n0, m0 = past['mC'], past['mn'], past['mm']
    hm, mC, mn, mm = _mlstm(m_q, m_k, m_v, ig, lf, C0, n0, m0)
    o_mlstm = (jax.nn.sigmoid(mo).reshape(B, T, MLSTM_HEADS, MLSTM_DV) * _rmsnorm(hm, p['m_gn'])).reshape(B, T, -1)

    gate = jax.nn.sigmoid(mg.reshape(B, T, N_BRANCH, D_MODEL))
    merged = jnp.zeros((B, T, D_MODEL), h.dtype)
    for i, ob in enumerate((o_nsa, o_gla, o_hgrn, o_mlstm)):
        merged = merged + gate[:, :, i] * (ob @ p['w_branch'][i])
    out = merged @ p['w_out']
    return out, (rows_new, win_state, gla_state, hgrn_state, mC, mn, mm, conv_state)


def _layer(x, c, pos0, past, p, rel_bias, lb):
    mod = (jax.nn.silu(c) @ p['w_ada'] + p['b_ada'])[:, None, :]
    sh1, sc1, gt1, sh2, sc2, gt2 = jnp.split(mod, 6, axis=-1)
    mix, st = _token_mixers(_rmsnorm(x, p['g_mix']) * (1 + sc1) + sh1, pos0, past, p, rel_bias, lb)
    x = x + gt1 * mix
    h = _rmsnorm(x, p['g_ffn']) * (1 + sc2) + sh2
    gu = h @ p['w_ffn_in']
    x = x + gt2 * ((jax.nn.silu(gu[..., :FF_DIM]) * gu[..., FF_DIM:]) @ p['w_ffn_out'])
    return x, st


def setup_inputs(seed: int = 0) -> dict:
    key = jax.random.key(seed)
    keys = iter(jax.random.split(key, 48))
    f32 = jnp.float32

    def nrm(shape, scale):
        return jax.random.normal(next(keys), shape, f32) * scale

    def gain(shape):
        return 1.0 + nrm(shape, 0.05)

    n_pages = PAST_LEN // PAGE_SIZE
    n_used = DEC_BATCH * n_pages
    n_pool = n_used + max(1, n_used // 4)
    G, HD = NSA_KV_HEADS, HEAD_DIM
    win_buf = min(WINDOW, PAST_LEN)
    nqk2 = 2 * MLSTM_HEADS * MLSTM_DQK
    page_table = jax.random.permutation(next(keys), n_pool)[:n_used].reshape(DEC_BATCH, n_pages).astype(jnp.int32)
    return {
        'x_prompt': nrm((BATCH, SEQ, D_MODEL), 1.0),
        'x_sample': nrm((DEC_BATCH, DEC_SEQ, D_MODEL), 1.0),
        'cache_nsa_kv': nrm((DEPTH, n_pool, PAGE_SIZE, 4, G, HD), 1.0),
        'state_nsa_win': nrm((DEPTH, DEC_BATCH, win_buf, 2, G, HD), 1.0),
        'state_gla': nrm((DEPTH, DEC_BATCH, GLA_HEADS, GLA_DK, GLA_DV), 0.5),
        'state_hgrn': nrm((DEPTH, DEC_BATCH, HGRN_HEADS, HGRN_EXPAND, HGRN_EXPAND), 0.5),
        'state_mlstm_C': nrm((DEPTH, DEC_BATCH, MLSTM_HEADS, MLSTM_DV, MLSTM_DQK), 1.0),
        'state_mlstm_n': nrm((DEPTH, DEC_BATCH, MLSTM_HEADS, MLSTM_DQK), 1.0),
        'state_mlstm_m': nrm((DEPTH, DEC_BATCH, MLSTM_HEADS), 1.0),
        'state_mlstm_conv': nrm((DEPTH, DEC_BATCH, MLSTM_CONV - 1, nqk2), 1.0),
        'page_table': page_table,
        'c_prompt': nrm((BATCH, D_MODEL), 1.0),
        'c_sample': nrm((DEC_BATCH, D_MODEL), 1.0),
        'rel_bias': nrm((REL_BUCKETS, NSA_HEADS), 0.5),
        'w_ada': nrm((DEPTH, D_MODEL, 6 * D_MODEL), 0.5 * D_MODEL ** -0.5),
        'b_ada': nrm((DEPTH, 6 * D_MODEL), 0.02),
        'g_mix': gain((DEPTH, D_MODEL)),
        'g_ffn': gain((DEPTH, D_MODEL)),
        'w_in': nrm((DEPTH, D_MODEL, D_IN), D_MODEL ** -0.5),
        'nsa_gq': gain((DEPTH, HD)),
        'nsa_gk': gain((DEPTH, 3, HD)),
        'cmp_pe': nrm((DEPTH, 2, CMP_BLOCK, HD), 0.1),
        'cmp_w1': nrm((DEPTH, 2, CMP_BLOCK * HD, HD), (CMP_BLOCK * HD) ** -0.5),
        'cmp_w2': nrm((DEPTH, 2, HD, HD), HD ** -0.5),
        'gla_wa': nrm((DEPTH, GLA_GATE_RANK, GLA_HEADS * GLA_DK), GLA_GATE_RANK ** -0.5),
        'gla_ba': nrm((DEPTH, GLA_HEADS * GLA_DK), 0.1),
        'gla_gn': gain((DEPTH, GLA_DV)),
        'hgrn_lb': nrm((DEPTH, BRANCH_WIDTH), 1.0),
        'hgrn_gn': gain((DEPTH, HGRN_EXPAND)),
        'm_wconv': nrm((DEPTH, MLSTM_CONV, nqk2), 0.5),
        'm_bconv': nrm((DEPTH, nqk2), 0.02),
        'm_bi': nrm((DEPTH, MLSTM_HEADS), 0.1),
        'm_bf': jnp.linspace(3.0, 6.0, MLSTM_HEADS, dtype=f32)[None, :] + nrm((DEPTH, MLSTM_HEADS), 0.1),
        'm_gn': gain((DEPTH, MLSTM_DV)),
        'w_branch': nrm((DEPTH, N_BRANCH, BRANCH_WIDTH, D_MODEL), BRANCH_WIDTH ** -0.5),
        'w_out': nrm((DEPTH, D_MODEL, D_MODEL), D_MODEL ** -0.5),
        'w_ffn_in': nrm((DEPTH, D_MODEL, 2 * FF_DIM), D_MODEL ** -0.5),
        'w_ffn_out': nrm((DEPTH, FF_DIM, D_MODEL), FF_DIM ** -0.5),
    }


def reference(x_prompt, x_sample, cache_nsa_kv, state_nsa_win, state_gla, state_hgrn, state_mlstm_C,
              state_mlstm_n, state_mlstm_m, state_mlstm_conv, page_table, c_prompt, c_sample, rel_bias,
              w_ada, b_ada, g_mix, g_ffn, w_in, nsa_gq, nsa_gk, cmp_pe, cmp_w1, cmp_w2, gla_wa, gla_ba,
              gla_gn, hgrn_lb, hgrn_gn, m_wconv, m_bconv, m_bi, m_bf, m_gn, w_branch, w_out, w_ffn_in, w_ffn_out):
    lb_cum = jnp.cumsum(jax.nn.softmax(hgrn_lb.astype(jnp.float32), axis=0), axis=0)
    lb_all = lb_cum - lb_cum[:1]
    dec_b, n_pages = page_table.shape
    past_len = n_pages * PAGE_SIZE
    x_p, x_s = x_prompt, x_sample
    st_prompt, st_sample = [], []
    for l in range(DEPTH):
        p = {'w_ada': w_ada[l], 'b_ada': b_ada[l], 'g_mix': g_mix[l], 'g_ffn': g_ffn[l], 'w_in': w_in[l],
             'nsa_gq': nsa_gq[l], 'nsa_gk': nsa_gk[l], 'cmp_pe': cmp_pe[l], 'cmp_w1': cmp_w1[l],
             'cmp_w2': cmp_w2[l], 'gla_wa': gla_wa[l], 'gla_ba': gla_ba[l], 'gla_gn': gla_gn[l],
             'hgrn_gn': hgrn_gn[l], 'm_wconv': m_wconv[l], 'm_bconv': m_bconv[l], 'm_bi': m_bi[l],
             'm_bf': m_bf[l], 'm_gn': m_gn[l], 'w_branch': w_branch[l], 'w_out': w_out[l],
             'w_ffn_in': w_ffn_in[l], 'w_ffn_out': w_ffn_out[l]}
        x_p, sp = _layer(x_p, c_prompt, 0, None, p, rel_bias, lb_all[l])
        past = {'kv': cache_nsa_kv[l][page_table].reshape(dec_b, past_len, 4, NSA_KV_HEADS, HEAD_DIM),
                'win': state_nsa_win[l], 'gla': state_gla[l], 'hgrn': state_hgrn[l],
                'mC': state_mlstm_C[l], 'mn': state_mlstm_n[l], 'mm': state_mlstm_m[l],
                'conv': state_mlstm_conv[l]}
        x_s, ss = _layer(x_s, c_sample, past_len, past, p, rel_bias, lb_all[l])
        st_prompt.append(sp)
        st_sample.append(ss)

    def stk(sts, i):
        return jnp.stack([s[i] for s in sts], axis=0)

    kv_p, kv_s = stk(st_prompt, 0), stk(st_sample, 0)
    win_p, win_s = stk(st_prompt, 1), stk(st_sample, 1)
    gla_p, gla_s = stk(st_prompt, 2), stk(st_sample, 2)
    hgrn_p, hgrn_s = stk(st_prompt, 3), stk(st_sample, 3)
    mC_p, mC_s = stk(st_prompt, 4), stk(st_sample, 4)
    mn_p, mn_s = stk(st_prompt, 5), stk(st_sample, 5)
    mm_p, mm_s = stk(st_prompt, 6), stk(st_sample, 6)
    conv_p, conv_s = stk(st_prompt, 7), stk(st_sample, 7)
    return (x_p, x_s, kv_p, kv_s, win_p, win_s, gla_p, gla_s, hgrn_p, hgrn_s,
            mC_p, mC_s, mn_p, mn_s, mm_p, mm_s, conv_p, conv_s)
```

```python
import functools
import math

import jax
import jax.numpy as jnp
import numpy as np
from jax import lax
from jax.experimental import pallas as pl
from jax.experimental.pallas import tpu as pltpu

D_MODEL = 4096
DEPTH = 2
PAGE_SIZE = 128
HEAD_DIM = 128
N_BRANCH = 4
BRANCH_WIDTH = D_MODEL // N_BRANCH
NSA_HEADS = BRANCH_WIDTH // HEAD_DIM
NSA_KV_HEADS = 2
NSA_HPG = NSA_HEADS // NSA_KV_HEADS
CMP_BLOCK = 32
CMP_STRIDE = 16
SEL_BLOCK = 64
N_SELECT = 16
WINDOW = 512
Q_BLOCK = 64
BAND = 128
REL_BUCKETS = 32
REL_MAX_DIST = 128
GLA_HEADS = 4
GLA_DK = BRANCH_WIDTH // (2 * GLA_HEADS)
GLA_DV = BRANCH_WIDTH // GLA_HEADS
GLA_GATE_RANK = 16
GLA_TAU = 16.0
HGRN_EXPAND = 128
HGRN_HEADS = BRANCH_WIDTH // HGRN_EXPAND
MLSTM_HEADS = 4
MLSTM_DQK = BRANCH_WIDTH // (2 * MLSTM_HEADS)
MLSTM_DV = BRANCH_WIDTH // MLSTM_HEADS
MLSTM_CONV = 4
CHUNK = 64
FF_DIM = ((8 * D_MODEL + 3 * 256 - 1) // (3 * 256)) * 256
EPS = 1e-6
NEG = -1e30
BIG = 1e6
IN_SIZES = (
    NSA_HEADS * HEAD_DIM, 6 * NSA_KV_HEADS * HEAD_DIM, 3 * NSA_HEADS,
    GLA_HEADS * GLA_DK, GLA_HEADS * GLA_DK, GLA_HEADS * GLA_DV, GLA_GATE_RANK, BRANCH_WIDTH,
    BRANCH_WIDTH, BRANCH_WIDTH, BRANCH_WIDTH, BRANCH_WIDTH,
    2 * MLSTM_HEADS * MLSTM_DQK, MLSTM_HEADS * MLSTM_DV, MLSTM_HEADS, MLSTM_HEADS, BRANCH_WIDTH,
    N_BRANCH * D_MODEL,
)
D_IN = sum(IN_SIZES)

V7X_VMEM_LIMIT_BYTES = 56 * 1024 * 1024
LANE = 128


def _round_up(n, m):
    return (n + m - 1) // m * m


def _mm_kernel(a_ref, w_ref, o_ref):
    o_ref[...] = jnp.dot(a_ref[...], w_ref[...], preferred_element_type=jnp.float32).astype(o_ref.dtype)


def _mm_k_kernel(a_ref, w_ref, o_ref, acc_ref):
    k = pl.program_id(2)

    @pl.when(k == 0)
    def _():
        acc_ref[...] = jnp.zeros_like(acc_ref)

    acc_ref[...] += jnp.dot(a_ref[...], w_ref[...], preferred_element_type=jnp.float32)

    @pl.when(k == pl.num_programs(2) - 1)
    def _():
        o_ref[...] = acc_ref[...].astype(o_ref.dtype)


def _mm(a, w, *, bm, bn, bk=None, out_dtype=jnp.float32, name="mm"):
    m, kd = a.shape
    n = w.shape[1]
    assert w.shape[0] == kd and m % bm == 0 and n % bn == 0, (a.shape, w.shape, bm, bn)
    params = dict(vmem_limit_bytes=V7X_VMEM_LIMIT_BYTES)
    if bk is None or bk == kd:
        return pl.pallas_call(
            _mm_kernel,
            grid=(n // bn, m // bm),
            in_specs=[pl.BlockSpec((bm, kd), lambda j, i: (i, 0)),
                      pl.BlockSpec((kd, bn), lambda j, i: (0, j))],
            out_specs=pl.BlockSpec((bm, bn), lambda j, i: (i, j)),
            out_shape=jax.ShapeDtypeStruct((m, n), out_dtype),
            compiler_params=pltpu.CompilerParams(dimension_semantics=("arbitrary", "arbitrary"), **params),
            name=name,
        )(a, w)
    assert kd % bk == 0
    return pl.pallas_call(
        _mm_k_kernel,
        grid=(n // bn, m // bm, kd // bk),
        in_specs=[pl.BlockSpec((bm, bk), lambda j, i, k: (i, k)),
                  pl.BlockSpec((bk, bn), lambda j, i, k: (k, j))],
        out_specs=pl.BlockSpec((bm, bn), lambda j, i, k: (i, j)),
        out_shape=jax.ShapeDtypeStruct((m, n), out_dtype),
        scratch_shapes=[pltpu.VMEM((bm, bn), jnp.float32)],
        compiler_params=pltpu.CompilerParams(
            dimension_semantics=("arbitrary", "arbitrary", "arbitrary"), **params),
        name=name,
    )(a, w)


def _matmul(a, w_bf16, name):
    m, kd = a.shape
    n = w_bf16.shape[1]
    a = a.astype(jnp.bfloat16)
    mp = _round_up(m, 16)
    if mp != m:
        a = jnp.pad(a, ((0, mp - m), (0, 0)))
    if mp >= 1024:
        bm = 1024
        bn = 512
    else:
        bm = mp
        bn = 1024 if n % 1024 == 0 else 512
    bk = None
    if kd > 8192:
        bk = kd // 4
    out = _mm(a, w_bf16, bm=bm, bn=bn, bk=bk, name=name)
    return out[:m] if mp != m else out


def _prep_w(w, n_mult=1024, k_mult=None):
    kd, n = w.shape
    np_ = _round_up(n, n_mult)
    kp = kd if k_mult is None else _round_up(kd, k_mult)
    w = w.astype(jnp.bfloat16)
    if np_ != n or kp != kd:
        w = jnp.pad(w, ((0, kp - kd), (0, np_ - n)))
    return w


def _split(z, sizes):
    cuts = [int(c) for c in np.cumsum(sizes)[:-1]]
    return jnp.split(z, cuts, axis=-1)


def _rmsnorm(x, g):
    xf = x.astype(jnp.float32)
    y = xf * lax.rsqrt(jnp.mean(xf * xf, axis=-1, keepdims=True) + EPS)
    return (y * g.astype(jnp.float32)).astype(x.dtype)


def _masked_softmax(s, mask):
    s = jnp.where(mask, s.astype(jnp.float32), NEG)
    return jnp.where(mask, jax.nn.softmax(s, axis=-1), 0.0)


def _t5_bucket(dist):
    d = jnp.maximum(dist, 0)
    exact = REL_BUCKETS // 2
    far = exact + (jnp.log(jnp.maximum(d, 1).astype(jnp.float32) / exact)
                   / math.log(REL_MAX_DIST / exact) * (REL_BUCKETS - exact)).astype(jnp.int32)
    return jnp.where(d < exact, d, jnp.minimum(far, REL_BUCKETS - 1))


def _nsa_compress(k_rows, v_rows, pe, w1, w2, g_kc):
    B, L, G, _ = k_rows.shape
    nc = (L - CMP_BLOCK) // CMP_STRIDE + 1
    idx = jnp.arange(nc)[:, None] * CMP_STRIDE + jnp.arange(CMP_BLOCK)[None, :]

    def phi(rows, pe_, w1_, w2_):
        blk = rows[:, idx] + pe_[None, None, :, None, :]
        flat = blk.transpose(0, 1, 3, 2, 4).reshape(B, nc, G, CMP_BLOCK * HEAD_DIM)
        return jax.nn.silu(flat @ w1_) @ w2_

    kc = _rmsnorm(phi(k_rows, pe[0], w1[0], w2[0]), g_kc)
    vc = phi(v_rows, pe[1], w1[1], w2[1])
    c_end = jnp.arange(nc) * CMP_STRIDE + CMP_BLOCK - 1
    return kc, vc, c_end


def _block_cover(nc, ns):
    c0 = jnp.arange(nc)[:, None] * CMP_STRIDE
    s0 = jnp.arange(ns)[None, :] * SEL_BLOCK
    return ((c0 <= s0 + SEL_BLOCK - 1) & (c0 + CMP_BLOCK - 1 >= s0)).astype(jnp.float32)


def _nsa_cmp_sel(q, q_pos, kc, vc, c_end, ks_t, vs_t, rel_bias, cover):
    B, Tq, G, HPG, _ = q.shape
    L = ks_t.shape[2]
    ns = cover.shape[1]
    s_c = jnp.einsum('btghd,bngd->btghn', q, kc).astype(jnp.float32)
    dist_c = q_pos[:, None] - c_end[None, :]
    bias_c = rel_bias[_t5_bucket(dist_c)].reshape(Tq, -1, G, HPG).transpose(0, 2, 3, 1)
    p_c = _masked_softmax(s_c + bias_c, (dist_c >= 0)[:, None, None, :])
    o_cmp = jnp.einsum('btghn,bngd->btghd', p_c.astype(vc.dtype), vc)
    imp = jnp.einsum('btgn,ns->btgs', p_c.sum(axis=3), cover)
    blk = jnp.arange(ns)[None, :]
    cur = (q_pos // SEL_BLOCK)[:, None]
    forced = ((blk == 0) | (blk == cur) | (blk == cur - 1))[:, None, :]
    valid = (blk <= cur)[:, None, :]
    score = jnp.where(forced, BIG, jnp.where(valid, imp, -BIG))
    n_top = min(N_SELECT, ns)
    _, top = lax.top_k(score, n_top)
    tok = (top[..., None] * SEL_BLOCK + jnp.arange(SEL_BLOCK)).reshape(B, Tq, G, n_top * SEL_BLOCK)
    tok_c = jnp.minimum(tok, L - 1)
    bi = jnp.arange(B)[:, None, None, None]
    gi = jnp.arange(G)[None, None, :, None]
    kg = ks_t[bi, gi, tok_c]
    vg = vs_t[bi, gi, tok_c]
    s_s = jnp.einsum('btghd,btgsd->btghs', q, kg).astype(jnp.float32)
    dist_s = q_pos[None, :, None, None] - tok
    bias_s = rel_bias.reshape(REL_BUCKETS, G, HPG)[_t5_bucket(dist_s), gi]
    p_s = _masked_softmax(s_s + bias_s.transpose(0, 1, 2, 4, 3), (dist_s >= 0)[:, :, :, None, :])
    o_sel = jnp.einsum('btghs,btgsd->btghd', p_s.astype(vg.dtype), vg)
    return o_cmp, o_sel


def _window_attend(q, q_pos, k, v, k_pos, rel_bias):
    N, Qb = q_pos.shape
    Kb = k_pos.shape[1]
    G, HPG = q.shape[3], q.shape[4]
    s = jnp.einsum('bnqghd,bnkgd->bnqghk', q, k).astype(jnp.float32)
    dist = q_pos[:, :, None] - k_pos[:, None, :]
    bias = rel_bias[_t5_bucket(dist)].reshape(N, Qb, Kb, G, HPG).transpose(0, 1, 3, 4, 2)
    mask = ((dist >= 0) & (dist <= WINDOW) & (k_pos[:, None, :] >= 0))[:, :, None, None, :]
    p = _masked_softmax(s + bias, mask)
    return jnp.einsum('bnqghk,bnkgd->bnqghd', p.astype(v.dtype), v)


def _gated_linear_attn(q, k, v, log_a, s0):
    B, T, H, _ = q.shape
    dv = v.shape[-1]
    C = CHUNK if T % CHUNK == 0 else T
    n = T // C

    def chunks(a):
        return a.astype(jnp.float32).reshape(B, n, C, H, a.shape[-1]).transpose(1, 0, 3, 2, 4)

    causal = jnp.tril(jnp.ones((C, C), bool))

    def step(S, inp):
        qc, kc, vc, gc = inp
        b = jnp.cumsum(gc, axis=2)
        diff = jnp.where(causal[:, :, None], b[:, :, :, None, :] - b[:, :, None, :, :], -jnp.inf)
        att = jnp.einsum('bhtk,bhsk,bhtsk->bhts', qc, kc, jnp.exp(diff))
        o = (jnp.einsum('bhtk,bhkv->bhtv', qc * jnp.exp(b), S)
             + jnp.einsum('bhts,bhsv->bhtv', att, vc))
        b_last = b[:, :, -1:, :]
        S = (jnp.exp(b_last[:, :, 0, :])[..., None] * S
             + jnp.einsum('bhsk,bhsv->bhkv', kc * jnp.exp(b_last - b), vc))
        return S, o

    S, o = lax.scan(step, s0.astype(jnp.float32), (chunks(q), chunks(k), chunks(v), chunks(log_a)))
    o = o.transpose(1, 0, 3, 2, 4).reshape(B, T, H, dv)
    return o.astype(v.dtype), S.astype(s0.dtype)


def _mlstm(q, k, v, ig, lf, C0, n0, m0):
    B, T, H, _ = q.shape
    dv = v.shape[-1]
    C = CHUNK if T % CHUNK == 0 else T
    n = T // C

    def chunks(a):
        return a.astype(jnp.float32).reshape(B, n, C, H, a.shape[-1]).transpose(1, 0, 3, 2, 4)

    def gchunks(a):
        return a.astype(jnp.float32).reshape(B, n, C, H).transpose(1, 0, 3, 2)

    causal = jnp.tril(jnp.ones((C, C), bool))

    def step(carry, inp):
        Cs, ns_, m = carry
        qc, kc, vc, ic, fc = inp
        F = jnp.cumsum(fc, axis=-1)
        logw = jnp.where(causal, F[..., :, None] - F[..., None, :] + ic[..., None, :], -jnp.inf)
        from_state = F + m[..., None]
        m_hat = jnp.maximum(from_state, logw.max(-1))
        w = jnp.exp(logw - m_hat[..., None]) * jnp.einsum('bhtk,bhsk->bhts', qc, kc)
        ws = jnp.exp(from_state - m_hat)
        num = ws[..., None] * jnp.einsum('bhtk,bhvk->bhtv', qc, Cs) + jnp.einsum('bhts,bhsv->bhtv', w, vc)
        den = ws * jnp.einsum('bhtk,bhk->bht', qc, ns_) + w.sum(-1)
        h = num / jnp.maximum(jnp.abs(den), jnp.exp(-m_hat))[..., None]
        m_new = m_hat[..., -1]
        ds = jnp.exp(F[..., -1:] - F + ic - m_new[..., None])
        dst = jnp.exp(F[..., -1] + m - m_new)
        Cs = dst[..., None, None] * Cs + jnp.einsum('bhs,bhsv,bhsk->bhvk', ds, vc, kc)
        ns_ = dst[..., None] * ns_ + jnp.einsum('bhs,bhsk->bhk', ds, kc)
        return (Cs, ns_, m_new), h

    carry0 = (C0.astype(jnp.float32), n0.astype(jnp.float32), m0.astype(jnp.float32))
    (Cf, nf, mf), h = lax.scan(step, carry0, (chunks(q), chunks(k), chunks(v), gchunks(ig), gchunks(lf)))
    h = h.transpose(1, 0, 3, 2, 4).reshape(B, T, H, dv)
    return h.astype(v.dtype), Cf.astype(C0.dtype), nf.astype(n0.dtype), mf.astype(m0.dtype)


def _causal_conv(u, buf, w, b):
    T = u.shape[1]
    up = jnp.concatenate([buf, u], axis=1)
    y = b
    for j in range(MLSTM_CONV):
        y = y + up[:, j:j + T] * w[j]
    return jax.nn.silu(y), up[:, -(MLSTM_CONV - 1):]


def _token_mixers(h, pos0, past, p, rel_bias, lb):
    B, T, _ = h.shape
    G, HPG, HD = NSA_KV_HEADS, NSA_HPG, HEAD_DIM
    z = _matmul(h.reshape(B * T, D_MODEL), p['w_in'], "mm_in")[:, :D_IN].reshape(B, T, D_IN)
    (nq, nkv, ngt, gq, gk, gv, ga, gr, hq, hf, hi, hg,
     mqk, mv, mi, mf, mo, mg) = _split(z, IN_SIZES)
    q_pos = pos0 + jnp.arange(T)

    q = _rmsnorm(nq.reshape(B, T, G, HPG, HD), p['nsa_gq']) * (HD ** -0.5)
    kv = nkv.reshape(B, T, 6, G, HD)
    rows_new = jnp.stack([kv[:, :, 0], kv[:, :, 1], _rmsnorm(kv[:, :, 2], p['nsa_gk'][1]), kv[:, :, 3]], axis=2)
    win_new = jnp.stack([_rmsnorm(kv[:, :, 4], p['nsa_gk'][2]), kv[:, :, 5]], axis=2)
    rows = rows_new if past is None else jnp.concatenate([past['kv'], rows_new], axis=1)
    kc, vc, c_end = _nsa_compress(rows[:, :, 0], rows[:, :, 1], p['cmp_pe'], p['cmp_w1'], p['cmp_w2'], p['nsa_gk'][0])
    L = rows.shape[1]
    cover = _block_cover(kc.shape[1], -(-L // SEL_BLOCK))
    ks_t = rows[:, :, 2].transpose(0, 2, 1, 3)
    vs_t = rows[:, :, 3].transpose(0, 2, 1, 3)

    def attend(args):
        return _nsa_cmp_sel(args[0], args[1], kc, vc, c_end, ks_t, vs_t, rel_bias, cover)

    if past is None:
        nqb = T // Q_BLOCK
        o_c, o_s = lax.map(attend, (q.reshape(B, nqb, Q_BLOCK, G, HPG, HD).swapaxes(0, 1),
                                    q_pos.reshape(nqb, Q_BLOCK)))
        o_c = o_c.swapaxes(0, 1).reshape(B, T, G, HPG, HD)
        o_s = o_s.swapaxes(0, 1).reshape(B, T, G, HPG, HD)
        nb = T // BAND
        kw = jnp.pad(win_new, ((0, 0), (WINDOW, 0), (0, 0), (0, 0), (0, 0)))
        idx = jnp.arange(nb)[:, None] * BAND + jnp.arange(BAND + WINDOW)[None, :]
        band = kw[:, idx]
        o_w = _window_attend(q.reshape(B, nb, BAND, G, HPG, HD), q_pos.reshape(nb, BAND),
                             band[:, :, :, 0], band[:, :, :, 1], idx - WINDOW, rel_bias).reshape(B, T, G, HPG, HD)
        win_state = win_new[:, -min(WINDOW, T):]
    else:
        o_c, o_s = attend((q, q_pos))
        wb = past['win'].shape[1]
        allw = jnp.concatenate([past['win'], win_new], axis=1)
        k_pos = (pos0 - wb + jnp.arange(wb + T))[None, :]
        o_w = _window_attend(q[:, None], q_pos[None, :], allw[:, None, :, 0], allw[:, None, :, 1], k_pos, rel_bias)[:, 0]
        win_state = allw[:, -wb:]
    gts = jax.nn.sigmoid(ngt.astype(jnp.float32)).reshape(B, T, 3, G, HPG, 1).astype(h.dtype)
    o_nsa = (gts[:, :, 0] * o_c + gts[:, :, 1] * o_s + gts[:, :, 2] * o_w).reshape(B, T, -1)

    g_q = gq.reshape(B, T, GLA_HEADS, GLA_DK) * (GLA_DK ** -0.5)
    g_k = gk.reshape(B, T, GLA_HEADS, GLA_DK)
    g_v = gv.reshape(B, T, GLA_HEADS, GLA_DV)
    log_a = (jax.nn.log_sigmoid((ga @ p['gla_wa'] + p['gla_ba']).astype(jnp.float32)) / GLA_TAU).reshape(B, T, GLA_HEADS, GLA_DK)
    s0 = jnp.zeros((B, GLA_HEADS, GLA_DK, GLA_DV), h.dtype) if past is None else past['gla']
    o, gla_state = _gated_linear_attn(g_q, g_k, g_v, log_a, s0)
    o_gla = (_rmsnorm(o, p['gla_gn']) * jax.nn.silu(gr.reshape(B, T, GLA_HEADS, GLA_DV))).reshape(B, T, -1)

    log_f = jnp.logaddexp(jnp.log(lb), jnp.log1p(-lb) + jax.nn.log_sigmoid(hf.astype(jnp.float32)))
    log_f = log_f.reshape(B, T, HGRN_HEADS, HGRN_EXPAND)
    h_q = jax.nn.silu(hq).reshape(B, T, HGRN_HEADS, HGRN_EXPAND)
    h_v = hi.reshape(B, T, HGRN_HEADS, HGRN_EXPAND)
    s0h = jnp.zeros((B, HGRN_HEADS, HGRN_EXPAND, HGRN_EXPAND), h.dtype) if past is None else past['hgrn']
    o, hgrn_state = _gated_linear_attn(h_q, -jnp.expm1(log_f), h_v, log_f, s0h)
    o_hgrn = (_rmsnorm(o, p['hgrn_gn']) * jax.nn.sigmoid(hg.reshape(B, T, HGRN_HEADS, HGRN_EXPAND))).reshape(B, T, -1)

    nqk = MLSTM_HEADS * MLSTM_DQK
    buf = jnp.zeros((B, MLSTM_CONV - 1, 2 * nqk), mqk.dtype) if past is None else past['conv']
    u, conv_state = _causal_conv(mqk, buf, p['m_wconv'], p['m_bconv'])
    m_q = u[..., :nqk].reshape(B, T, MLSTM_HEADS, MLSTM_DQK)
    m_k = u[..., nqk:].reshape(B, T, MLSTM_HEADS, MLSTM_DQK) * (MLSTM_DQK ** -0.5)
    m_v = mv.reshape(B, T, MLSTM_HEADS, MLSTM_DV)
    ig = (mi + p['m_bi']).astype(jnp.float32)
    lf = jax.nn.log_sigmoid((mf + p['m_bf']).astype(jnp.float32))
    if past is None:
        C0 = jnp.zeros((B, MLSTM_HEADS, MLSTM_DV, MLSTM_DQK), h.dtype)
        n0 = jnp.zeros((B, MLSTM_HEADS, MLSTM_DQK), h.dtype)
        m0 = jnp.zeros((B, MLSTM_HEADS), h.dtype)
    else:
        C0, n0, m0 = past['mC'], past['mn'], past['mm']
    hm, mC, mn, mm = _mlstm(m_q, m_k, m_v, ig, lf, C0, n0, m0)
    o_mlstm = (jax.nn.sigmoid(mo).reshape(B, T, MLSTM_HEADS, MLSTM_DV) * _rmsnorm(hm, p['m_gn'])).reshape(B, T, -1)

    gate = jax.nn.sigmoid(mg.reshape(B, T, N_BRANCH, D_MODEL))
    merged = jnp.zeros((B, T, D_MODEL), h.dtype)
    for i, ob in enumerate((o_nsa, o_gla, o_hgrn, o_mlstm)):
        proj = _matmul(ob.reshape(B * T, BRANCH_WIDTH), p['w_branch'][i], "mm_branch").reshape(B, T, D_MODEL)
        merged = merged + gate[:, :, i] * proj
    out = _matmul(merged.reshape(B * T, D_MODEL), p['w_out'], "mm_out").reshape(B, T, D_MODEL)
    return out, (rows_new, win_state, gla_state, hgrn_state, mC, mn, mm, conv_state)


def _layer(x, mod, pos0, past, p, rel_bias, lb):
    B, T, _ = x.shape
    sh1, sc1, gt1, sh2, sc2, gt2 = jnp.split(mod[:, None, :], 6, axis=-1)
    mix, st = _token_mixers(_rmsnorm(x, p['g_mix']) * (1 + sc1) + sh1, pos0, past, p, rel_bias, lb)
    x = x + gt1 * mix
    h = _rmsnorm(x, p['g_ffn']) * (1 + sc2) + sh2
    gu = _matmul(h.reshape(B * T, D_MODEL), p['w_ffn_in'], "mm_ffn_in")
    ffp = p['w_ffn_out'].shape[0]
    act = jax.nn.silu(gu[:, :ffp]) * gu[:, ffp:]
    y = _matmul(act, p['w_ffn_out'], "mm_ffn_out").reshape(B, T, D_MODEL)
    x = x + gt2 * y
    return x, st


def kernel(x_prompt, x_sample, cache_nsa_kv, state_nsa_win, state_gla, state_hgrn, state_mlstm_C,
           state_mlstm_n, state_mlstm_m, state_mlstm_conv, page_table, c_prompt, c_sample, rel_bias,
           w_ada, b_ada, g_mix, g_ffn, w_in, nsa_gq, nsa_gk, cmp_pe, cmp_w1, cmp_w2, gla_wa, gla_ba,
           gla_gn, hgrn_lb, hgrn_gn, m_wconv, m_bconv, m_bi, m_bf, m_gn, w_branch, w_out, w_ffn_in, w_ffn_out):
    lb_cum = jnp.cumsum(jax.nn.softmax(hgrn_lb.astype(jnp.float32), axis=0), axis=0)
    lb_all = lb_cum - lb_cum[:1]
    dec_b, n_pages = page_table.shape
    past_len = n_pages * PAGE_SIZE
    n_prompt = c_prompt.shape[0]
    ffp = _round_up(FF_DIM, 1024)
    x_p, x_s = x_prompt, x_sample
    st_prompt, st_sample = [], []
    c_all = jax.nn.silu(jnp.concatenate([c_prompt, c_sample], axis=0))
    for l in range(DEPTH):
        w_ffn_in_l = jnp.concatenate(
            [_prep_w(w_ffn_in[l][:, :FF_DIM]), _prep_w(w_ffn_in[l][:, FF_DIM:])], axis=1)
        p = {'g_mix': g_mix[l], 'g_ffn': g_ffn[l], 'w_in': _prep_w(w_in[l]),
             'nsa_gq': nsa_gq[l], 'nsa_gk': nsa_gk[l], 'cmp_pe': cmp_pe[l], 'cmp_w1': cmp_w1[l],
             'cmp_w2': cmp_w2[l], 'gla_wa': gla_wa[l], 'gla_ba': gla_ba[l], 'gla_gn': gla_gn[l],
             'hgrn_gn': hgrn_gn[l], 'm_wconv': m_wconv[l], 'm_bconv': m_bconv[l], 'm_bi': m_bi[l],
             'm_bf': m_bf[l], 'm_gn': m_gn[l],
             'w_branch': [_prep_w(w_branch[l][i]) for i in range(N_BRANCH)],
             'w_out': _prep_w(w_out[l]),
             'w_ffn_in': w_ffn_in_l, 'w_ffn_out': _prep_w(w_ffn_out[l], k_mult=1024)}
        assert p['w_ffn_out'].shape[0] == ffp
        mod = _matmul(c_all, _prep_w(w_ada[l]), "mm_ada") + b_ada[l]
        x_p, sp = _layer(x_p, mod[:n_prompt], 0, None, p, rel_bias, lb_all[l])
        past = {'kv': cache_nsa_kv[l][page_table].reshape(dec_b, past_len, 4, NSA_KV_HEADS, HEAD_DIM),
                'win': state_nsa_win[l], 'gla': state_gla[l], 'hgrn': state_hgrn[l],
                'mC': state_mlstm_C[l], 'mn': state_mlstm_n[l], 'mm': state_mlstm_m[l],
                'conv': state_mlstm_conv[l]}
        x_s, ss = _layer(x_s, mod[n_prompt:], past_len, past, p, rel_bias, lb_all[l])
        st_prompt.append(sp)
        st_sample.append(ss)

    def stk(sts, i):
        return jnp.stack([s[i] for s in sts], axis=0)

    outs = [x_p, x_s]
    for i in range(8):
        outs.append(stk(st_prompt, i))
        outs.append(stk(st_sample, i))
    return tuple(outs)
```

```python
import functools
import math

import jax
import jax.numpy as jnp
import numpy as np
from jax import lax
from jax.experimental import pallas as pl
from jax.experimental.pallas import tpu as pltpu

D_MODEL = 4096
DEPTH = 2
PAGE_SIZE = 128
HEAD_DIM = 128
N_BRANCH = 4
BRANCH_WIDTH = D_MODEL // N_BRANCH
NSA_HEADS = BRANCH_WIDTH // HEAD_DIM
NSA_KV_HEADS = 2
NSA_HPG = NSA_HEADS // NSA_KV_HEADS
CMP_BLOCK = 32
CMP_STRIDE = 16
SEL_BLOCK = 64
N_SELECT = 16
WINDOW = 512
Q_BLOCK = 64
BAND = 128
REL_BUCKETS = 32
REL_MAX_DIST = 128
GLA_HEADS = 4
GLA_DK = BRANCH_WIDTH // (2 * GLA_HEADS)
GLA_DV = BRANCH_WIDTH // GLA_HEADS
GLA_GATE_RANK = 16
GLA_TAU = 16.0
HGRN_EXPAND = 128
HGRN_HEADS = BRANCH_WIDTH // HGRN_EXPAND
MLSTM_HEADS = 4
MLSTM_DQK = BRANCH_WIDTH // (2 * MLSTM_HEADS)
MLSTM_DV = BRANCH_WIDTH // MLSTM_HEADS
MLSTM_CONV = 4
CHUNK = 64
FF_DIM = ((8 * D_MODEL + 3 * 256 - 1) // (3 * 256)) * 256
EPS = 1e-6
NEG = -1e30
BIG = 1e6
IN_SIZES = (
    NSA_HEADS * HEAD_DIM, 6 * NSA_KV_HEADS * HEAD_DIM, 3 * NSA_HEADS,
    GLA_HEADS * GLA_DK, GLA_HEADS * GLA_DK, GLA_HEADS * GLA_DV, GLA_GATE_RANK, BRANCH_WIDTH,
    BRANCH_WIDTH, BRANCH_WIDTH, BRANCH_WIDTH, BRANCH_WIDTH,
    2 * MLSTM_HEADS * MLSTM_DQK, MLSTM_HEADS * MLSTM_DV, MLSTM_HEADS, MLSTM_HEADS, BRANCH_WIDTH,
    N_BRANCH * D_MODEL,
)
D_IN = sum(IN_SIZES)

V7X_VMEM_LIMIT_BYTES = 56 * 1024 * 1024
LANE = 128


def _round_up(n, m):
    return (n + m - 1) // m * m


def _mm_kernel(a_ref, w_ref, o_ref):
    o_ref[...] = jnp.dot(a_ref[...], w_ref[...], preferred_element_type=jnp.float32).astype(o_ref.dtype)


def _mm_k_kernel(a_ref, w_ref, o_ref, acc_ref):
    k = pl.program_id(2)

    @pl.when(k == 0)
    def _():
        acc_ref[...] = jnp.zeros_like(acc_ref)

    acc_ref[...] += jnp.dot(a_ref[...], w_ref[...], preferred_element_type=jnp.float32)

    @pl.when(k == pl.num_programs(2) - 1)
    def _():
        o_ref[...] = acc_ref[...].astype(o_ref.dtype)


def _mm(a, w, *, bm, bn, bk=None, out_dtype=jnp.float32, name="mm"):
    m, kd = a.shape
    n = w.shape[1]
    assert w.shape[0] == kd and m % bm == 0 and n % bn == 0, (a.shape, w.shape, bm, bn)
    params = dict(vmem_limit_bytes=V7X_VMEM_LIMIT_BYTES)
    if bk is None or bk == kd:
        return pl.pallas_call(
            _mm_kernel,
            grid=(n // bn, m // bm),
            in_specs=[pl.BlockSpec((bm, kd), lambda j, i: (i, 0)),
                      pl.BlockSpec((kd, bn), lambda j, i: (0, j))],
            out_specs=pl.BlockSpec((bm, bn), lambda j, i: (i, j)),
            out_shape=jax.ShapeDtypeStruct((m, n), out_dtype),
            compiler_params=pltpu.CompilerParams(dimension_semantics=("arbitrary", "arbitrary"), **params),
            name=name,
        )(a, w)
    assert kd % bk == 0
    return pl.pallas_call(
        _mm_k_kernel,
        grid=(n // bn, m // bm, kd // bk),
        in_specs=[pl.BlockSpec((bm, bk), lambda j, i, k: (i, k)),
                  pl.BlockSpec((bk, bn), lambda j, i, k: (k, j))],
        out_specs=pl.BlockSpec((bm, bn), lambda j, i, k: (i, j)),
        out_shape=jax.ShapeDtypeStruct((m, n), out_dtype),
        scratch_shapes=[pltpu.VMEM((bm, bn), jnp.float32)],
        compiler_params=pltpu.CompilerParams(
            dimension_semantics=("arbitrary", "arbitrary", "arbitrary"), **params),
        name=name,
    )(a, w)


def _matmul(a, w_bf16, name):
    m, kd = a.shape
    n = w_bf16.shape[1]
    a = a.astype(jnp.bfloat16)
    mp = _round_up(m, 16)
    if mp != m:
        a = jnp.pad(a, ((0, mp - m), (0, 0)))
    if mp >= 1024:
        bm = 1024
        bn = 512
    else:
        bm = mp
        bn = 1024 if n % 1024 == 0 else 512
    bk = None
    if kd > 8192:
        bk = kd // 4
    out = _mm(a, w_bf16, bm=bm, bn=bn, bk=bk, name=name)
    return out[:m] if mp != m else out


def _prep_w(w, n_mult=1024, k_mult=None):
    kd, n = w.shape
    np_ = _round_up(n, n_mult)
    kp = kd if k_mult is None else _round_up(kd, k_mult)
    w = w.astype(jnp.bfloat16)
    if np_ != n or kp != kd:
        w = jnp.pad(w, ((0, kp - kd), (0, np_ - n)))
    return w


NSA_TQ = 128
NSA_KB = 128
SEL_PER_KB = NSA_KB // SEL_BLOCK
WIN_TILES = WINDOW // NSA_KB + 1


def _split3_bf16(x):
    hi = x.astype(jnp.bfloat16)
    r1 = x - hi.astype(jnp.float32)
    mid = r1.astype(jnp.bfloat16)
    lo = (r1 - mid.astype(jnp.float32)).astype(jnp.bfloat16)
    return hi, mid, lo


def _nsa_prompt_kernel(qT_ref, ksel_ref, vselT_ref, kwin_ref, vwinT_ref, kc_ref, vcT_ref, covT_ref,
                       bcmp_ref, bsel_ref, bwin_ref, gate_ref, o_ref,
                       mt_ref, m_ref, l_ref, acc_ref, *, n_top):
    hpg = qT_ref.shape[2]
    tq = qT_ref.shape[4]
    ns = covT_ref.shape[0]
    qb = pl.program_id(2)
    t0 = qb * tq
    gates = jax.nn.sigmoid(gate_ref[0, 0])

    kc = kc_ref[0, 0]
    vcT = vcT_ref[0, 0]
    p_sum = jnp.zeros((kc.shape[0], tq), jnp.float32)
    for h in range(hpg):
        bias = bcmp_ref[0, h]
        s = jnp.dot(kc, qT_ref[0, 0, h], preferred_element_type=jnp.float32) + bias
        e = jnp.exp(s - jnp.max(s, axis=0, keepdims=True))
        p = jnp.where(bias > 0.5 * NEG, e / jnp.sum(e, axis=0, keepdims=True), 0.0)
        p_sum = p_sum + p
        o_c = jnp.dot(vcT, p.astype(jnp.bfloat16), preferred_element_type=jnp.float32)
        o_ref[0, 0, h] = gates[0, h:h + 1, :] * o_c
    cov = covT_ref[...]
    imp = sum(jnp.dot(cov, part, preferred_element_type=jnp.float32) for part in _split3_bf16(p_sum))

    blk = lax.broadcasted_iota(jnp.int32, (ns, tq), 0)
    cur = (t0 + lax.broadcasted_iota(jnp.int32, (ns, tq), 1)) // SEL_BLOCK
    forced = (blk == 0) | (blk == cur) | (blk == cur - 1)
    score = jnp.where(forced, BIG, jnp.where(blk <= cur, imp, -BIG))
    rank = jnp.zeros((ns, tq), jnp.int32)
    for jp in range(ns):
        row = score[jp:jp + 1, :]
        beats = (row > score) | ((row == score) & (blk > jp))
        rank = rank + beats.astype(jnp.int32)
    mt_ref[...] = (rank < n_top).astype(jnp.float32)

    sub = lax.broadcasted_iota(jnp.int32, (NSA_KB, tq), 0)

    def attend(k_ref, vT_ref, bias_ref, n_tiles, lo, use_sel):
        m_ref[...] = jnp.full(m_ref.shape, NEG, jnp.float32)
        l_ref[...] = jnp.zeros(l_ref.shape, jnp.float32)
        acc_ref[...] = jnp.zeros(acc_ref.shape, jnp.float32)

        def body(kb, carry):
            koff = pl.multiple_of(kb * NSA_KB, NSA_KB)
            k_blk = k_ref[0, 0, pl.ds(koff, NSA_KB), :]
            vT_blk = vT_ref[0, 0, :, pl.ds(koff, NSA_KB)]
            tile = jnp.minimum(qb - kb, n_tiles - 1)
            if use_sel:
                r0 = mt_ref[pl.ds(kb * SEL_PER_KB, 1), :]
                r1 = mt_ref[pl.ds(kb * SEL_PER_KB + 1, 1), :]
                selm = jnp.where(sub < SEL_BLOCK, r0, r1) > 0.5
            for h in range(hpg):
                s = jnp.dot(k_blk, qT_ref[0, 0, h], preferred_element_type=jnp.float32) + bias_ref[0, h, tile]
                if use_sel:
                    s = jnp.where(selm, s, NEG)
                m_old = m_ref[h]
                m_new = jnp.maximum(m_old, jnp.max(s, axis=0, keepdims=True))
                alpha = jnp.exp(m_old - m_new)
                p = jnp.exp(s - m_new)
                l_ref[h] = alpha * l_ref[h] + jnp.sum(p, axis=0, keepdims=True)
                acc_ref[h] = alpha * acc_ref[h] + jnp.dot(vT_blk, p.astype(jnp.bfloat16),
                                                          preferred_element_type=jnp.float32)
                m_ref[h] = m_new
            return carry

        lax.fori_loop(lo, qb + 1, body, 0)

    attend(ksel_ref, vselT_ref, bsel_ref, bsel_ref.shape[2], 0, True)
    for h in range(hpg):
        o_ref[0, 0, h] += gates[1, h:h + 1, :] * (acc_ref[h] / l_ref[h])

    attend(kwin_ref, vwinT_ref, bwin_ref, bwin_ref.shape[2], jnp.maximum(qb - (WIN_TILES - 1), 0), False)
    for h in range(hpg):
        o_ref[0, 0, h] += gates[2, h:h + 1, :] * (acc_ref[h] / l_ref[h])


def _bias_tables(rel_bias, T):
    G, HPG = NSA_KV_HEADS, NSA_HPG
    nc = (T - CMP_BLOCK) // CMP_STRIDE + 1
    ncp = _round_up(nc, LANE)

    def lookup(dist, valid):
        b = rel_bias[_t5_bucket(dist)]
        b = jnp.where(valid[..., None], b, NEG)
        return jnp.moveaxis(b, -1, 0).reshape((G, HPG) + dist.shape)

    c = jnp.arange(NSA_KB)[:, None]
    i = jnp.arange(NSA_TQ)[None, :]
    sel_d = jnp.stack([dlt + i - c for dlt in (0, NSA_KB, 2 * NSA_KB)])
    bsel = lookup(sel_d, sel_d >= 0)
    win_d = jnp.stack([dlt * NSA_KB + i - c for dlt in range(WIN_TILES)])
    bwin = lookup(win_d, (win_d >= 0) & (win_d <= WINDOW))
    n = jnp.arange(ncp)[:, None]
    t = jnp.arange(T)[None, :]
    cmp_d = t - (n * CMP_STRIDE + CMP_BLOCK - 1)
    bcmp = lookup(cmp_d, (cmp_d >= 0) & (n < nc))
    ns = -(-T // SEL_BLOCK)
    covT = _block_cover(nc, ns).T
    covT = jnp.pad(covT, ((0, 0), (0, ncp - nc))).astype(jnp.bfloat16)
    return bsel, bwin, bcmp, covT


def _nsa_prompt(q, k_sel, v_sel, k_win, v_win, kc, vc, ngt, tables):
    B, T, G, HPG, HD = q.shape
    bsel, bwin, bcmp, covT = tables
    ns, ncp = covT.shape
    nc = kc.shape[1]
    bf = jnp.bfloat16
    qT = q.astype(bf).transpose(0, 2, 3, 4, 1)
    ksel = k_sel.astype(bf).transpose(0, 2, 1, 3)
    vselT = v_sel.astype(bf).transpose(0, 2, 3, 1)
    kwin = k_win.astype(bf).transpose(0, 2, 1, 3)
    vwinT = v_win.astype(bf).transpose(0, 2, 3, 1)
    kcp = jnp.pad(kc.astype(bf).transpose(0, 2, 1, 3), ((0, 0), (0, 0), (0, ncp - nc), (0, 0)))
    vcT = jnp.pad(vc.astype(bf).transpose(0, 2, 3, 1), ((0, 0), (0, 0), (0, 0), (0, ncp - nc)))
    gT = ngt.reshape(B, T, 3, G, HPG).transpose(0, 3, 2, 4, 1)
    tq = NSA_TQ
    full = lambda b, g, i: (b, g, 0, 0)
    oT = pl.pallas_call(
        functools.partial(_nsa_prompt_kernel, n_top=min(N_SELECT, ns)),
        grid=(B, G, T // tq),
        in_specs=[
            pl.BlockSpec((1, 1, HPG, HD, tq), lambda b, g, i: (b, g, 0, 0, i)),
            pl.BlockSpec((1, 1, T, HD), full),
            pl.BlockSpec((1, 1, HD, T), full),
            pl.BlockSpec((1, 1, T, HD), full),
            pl.BlockSpec((1, 1, HD, T), full),
            pl.BlockSpec((1, 1, ncp, HD), full),
            pl.BlockSpec((1, 1, HD, ncp), full),
            pl.BlockSpec((ns, ncp), lambda b, g, i: (0, 0)),
            pl.BlockSpec((1, HPG, ncp, tq), lambda b, g, i: (g, 0, 0, i)),
            pl.BlockSpec((1, HPG) + bsel.shape[2:], lambda b, g, i: (g, 0, 0, 0, 0)),
            pl.BlockSpec((1, HPG) + bwin.shape[2:], lambda b, g, i: (g, 0, 0, 0, 0)),
            pl.BlockSpec((1, 1, 3, HPG, tq), lambda b, g, i: (b, g, 0, 0, i)),
        ],
        out_specs=pl.BlockSpec((1, 1, HPG, HD, tq), lambda b, g, i: (b, g, 0, 0, i)),
        out_shape=jax.ShapeDtypeStruct((B, G, HPG, HD, T), jnp.float32),
        scratch_shapes=[pltpu.VMEM((ns, tq), jnp.float32),
                        pltpu.VMEM((HPG, 1, tq), jnp.float32),
                        pltpu.VMEM((HPG, 1, tq), jnp.float32),
                        pltpu.VMEM((HPG, HD, tq), jnp.float32)],
        compiler_params=pltpu.CompilerParams(
            dimension_semantics=("arbitrary", "arbitrary", "arbitrary"),
            vmem_limit_bytes=V7X_VMEM_LIMIT_BYTES),
        name="nsa_prompt",
    )(qT, ksel, vselT, kwin, vwinT, kcp, vcT, covT, bcmp, bsel, bwin, gT)
    return oT.transpose(0, 4, 1, 2, 3).reshape(B, T, G * HPG * HD)


def _split(z, sizes):
    cuts = [int(c) for c in np.cumsum(sizes)[:-1]]
    return jnp.split(z, cuts, axis=-1)


def _rmsnorm(x, g):
    xf = x.astype(jnp.float32)
    y = xf * lax.rsqrt(jnp.mean(xf * xf, axis=-1, keepdims=True) + EPS)
    return (y * g.astype(jnp.float32)).astype(x.dtype)


def _masked_softmax(s, mask):
    s = jnp.where(mask, s.astype(jnp.float32), NEG)
    return jnp.where(mask, jax.nn.softmax(s, axis=-1), 0.0)


def _t5_bucket(dist):
    d = jnp.maximum(dist, 0)
    exact = REL_BUCKETS // 2
    far = exact + (jnp.log(jnp.maximum(d, 1).astype(jnp.float32) / exact)
                   / math.log(REL_MAX_DIST / exact) * (REL_BUCKETS - exact)).astype(jnp.int32)
    return jnp.where(d < exact, d, jnp.minimum(far, REL_BUCKETS - 1))


def _nsa_compress(k_rows, v_rows, pe, w1, w2, g_kc):
    B, L, G, _ = k_rows.shape
    nc = (L - CMP_BLOCK) // CMP_STRIDE + 1
    idx = jnp.arange(nc)[:, None] * CMP_STRIDE + jnp.arange(CMP_BLOCK)[None, :]

    def phi(rows, pe_, w1_, w2_):
        blk = rows[:, idx] + pe_[None, None, :, None, :]
        flat = blk.transpose(0, 1, 3, 2, 4).reshape(B, nc, G, CMP_BLOCK * HEAD_DIM)
        return jax.nn.silu(flat @ w1_) @ w2_

    kc = _rmsnorm(phi(k_rows, pe[0], w1[0], w2[0]), g_kc)
    vc = phi(v_rows, pe[1], w1[1], w2[1])
    c_end = jnp.arange(nc) * CMP_STRIDE + CMP_BLOCK - 1
    return kc, vc, c_end


def _block_cover(nc, ns):
    c0 = jnp.arange(nc)[:, None] * CMP_STRIDE
    s0 = jnp.arange(ns)[None, :] * SEL_BLOCK
    return ((c0 <= s0 + SEL_BLOCK - 1) & (c0 + CMP_BLOCK - 1 >= s0)).astype(jnp.float32)


def _nsa_cmp_sel(q, q_pos, kc, vc, c_end, ks_t, vs_t, rel_bias, cover):
    B, Tq, G, HPG, _ = q.shape
    L = ks_t.shape[2]
    ns = cover.shape[1]
    s_c = jnp.einsum('btghd,bngd->btghn', q, kc).astype(jnp.float32)
    dist_c = q_pos[:, None] - c_end[None, :]
    bias_c = rel_bias[_t5_bucket(dist_c)].reshape(Tq, -1, G, HPG).transpose(0, 2, 3, 1)
    p_c = _masked_softmax(s_c + bias_c, (dist_c >= 0)[:, None, None, :])
    o_cmp = jnp.einsum('btghn,bngd->btghd', p_c.astype(vc.dtype), vc)
    imp = jnp.einsum('btgn,ns->btgs', p_c.sum(axis=3), cover)
    blk = jnp.arange(ns)[None, :]
    cur = (q_pos // SEL_BLOCK)[:, None]
    forced = ((blk == 0) | (blk == cur) | (blk == cur - 1))[:, None, :]
    valid = (blk <= cur)[:, None, :]
    score = jnp.where(forced, BIG, jnp.where(valid, imp, -BIG))
    n_top = min(N_SELECT, ns)
    _, top = lax.top_k(score, n_top)
    tok = (top[..., None] * SEL_BLOCK + jnp.arange(SEL_BLOCK)).reshape(B, Tq, G, n_top * SEL_BLOCK)
    tok_c = jnp.minimum(tok, L - 1)
    bi = jnp.arange(B)[:, None, None, None]
    gi = jnp.arange(G)[None, None, :, None]
    kg = ks_t[bi, gi, tok_c]
    vg = vs_t[bi, gi, tok_c]
    s_s = jnp.einsum('btghd,btgsd->btghs', q, kg).astype(jnp.float32)
    dist_s = q_pos[None, :, None, None] - tok
    bias_s = rel_bias.reshape(REL_BUCKETS, G, HPG)[_t5_bucket(dist_s), gi]
    p_s = _masked_softmax(s_s + bias_s.transpose(0, 1, 2, 4, 3), (dist_s >= 0)[:, :, :, None, :])
    o_sel = jnp.einsum('btghs,btgsd->btghd', p_s.astype(vg.dtype), vg)
    return o_cmp, o_sel


def _window_attend(q, q_pos, k, v, k_pos, rel_bias):
    N, Qb = q_pos.shape
    Kb = k_pos.shape[1]
    G, HPG = q.shape[3], q.shape[4]
    s = jnp.einsum('bnqghd,bnkgd->bnqghk', q, k).astype(jnp.float32)
    dist = q_pos[:, :, None] - k_pos[:, None, :]
    bias = rel_bias[_t5_bucket(dist)].reshape(N, Qb, Kb, G, HPG).transpose(0, 1, 3, 4, 2)
    mask = ((dist >= 0) & (dist <= WINDOW) & (k_pos[:, None, :] >= 0))[:, :, None, None, :]
    p = _masked_softmax(s + bias, mask)
    return jnp.einsum('bnqghk,bnkgd->bnqghd', p.astype(v.dtype), v)


def _gated_linear_attn(q, k, v, log_a, s0):
    B, T, H, _ = q.shape
    dv = v.shape[-1]
    C = CHUNK if T % CHUNK == 0 else T
    n = T // C

    def chunks(a):
        return a.astype(jnp.float32).reshape(B, n, C, H, a.shape[-1]).transpose(1, 0, 3, 2, 4)

    causal = jnp.tril(jnp.ones((C, C), bool))

    def step(S, inp):
        qc, kc, vc, gc = inp
        b = jnp.cumsum(gc, axis=2)
        diff = jnp.where(causal[:, :, None], b[:, :, :, None, :] - b[:, :, None, :, :], -jnp.inf)
        att = jnp.einsum('bhtk,bhsk,bhtsk->bhts', qc, kc, jnp.exp(diff))
        o = (jnp.einsum('bhtk,bhkv->bhtv', qc * jnp.exp(b), S)
             + jnp.einsum('bhts,bhsv->bhtv', att, vc))
        b_last = b[:, :, -1:, :]
        S = (jnp.exp(b_last[:, :, 0, :])[..., None] * S
             + jnp.einsum('bhsk,bhsv->bhkv', kc * jnp.exp(b_last - b), vc))
        return S, o

    S, o = lax.scan(step, s0.astype(jnp.float32), (chunks(q), chunks(k), chunks(v), chunks(log_a)))
    o = o.transpose(1, 0, 3, 2, 4).reshape(B, T, H, dv)
    return o.astype(v.dtype), S.astype(s0.dtype)


def _mlstm(q, k, v, ig, lf, C0, n0, m0):
    B, T, H, _ = q.shape
    dv = v.shape[-1]
    C = CHUNK if T % CHUNK == 0 else T
    n = T // C

    def chunks(a):
        return a.astype(jnp.float32).reshape(B, n, C, H, a.shape[-1]).transpose(1, 0, 3, 2, 4)

    def gchunks(a):
        return a.astype(jnp.float32).reshape(B, n, C, H).transpose(1, 0, 3, 2)

    causal = jnp.tril(jnp.ones((C, C), bool))

    def step(carry, inp):
        Cs, ns_, m = carry
        qc, kc, vc, ic, fc = inp
        F = jnp.cumsum(fc, axis=-1)
        logw = jnp.where(causal, F[..., :, None] - F[..., None, :] + ic[..., None, :], -jnp.inf)
        from_state = F + m[..., None]
        m_hat = jnp.maximum(from_state, logw.max(-1))
        w = jnp.exp(logw - m_hat[..., None]) * jnp.einsum('bhtk,bhsk->bhts', qc, kc)
        ws = jnp.exp(from_state - m_hat)
        num = ws[..., None] * jnp.einsum('bhtk,bhvk->bhtv', qc, Cs) + jnp.einsum('bhts,bhsv->bhtv', w, vc)
        den = ws * jnp.einsum('bhtk,bhk->bht', qc, ns_) + w.sum(-1)
        h = num / jnp.maximum(jnp.abs(den), jnp.exp(-m_hat))[..., None]
        m_new = m_hat[..., -1]
        ds = jnp.exp(F[..., -1:] - F + ic - m_new[..., None])
        dst = jnp.exp(F[..., -1] + m - m_new)
        Cs = dst[..., None, None] * Cs + jnp.einsum('bhs,bhsv,bhsk->bhvk', ds, vc, kc)
        ns_ = dst[..., None] * ns_ + jnp.einsum('bhs,bhsk->bhk', ds, kc)
        return (Cs, ns_, m_new), h

    carry0 = (C0.astype(jnp.float32), n0.astype(jnp.float32), m0.astype(jnp.float32))
    (Cf, nf, mf), h = lax.scan(step, carry0, (chunks(q), chunks(k), chunks(v), gchunks(ig), gchunks(lf)))
    h = h.transpose(1, 0, 3, 2, 4).reshape(B, T, H, dv)
    return h.astype(v.dtype), Cf.astype(C0.dtype), nf.astype(n0.dtype), mf.astype(m0.dtype)


def _causal_conv(u, buf, w, b):
    T = u.shape[1]
    up = jnp.concatenate([buf, u], axis=1)
    y = b
    for j in range(MLSTM_CONV):
        y = y + up[:, j:j + T] * w[j]
    return jax.nn.silu(y), up[:, -(MLSTM_CONV - 1):]


def _token_mixers(h, pos0, past, p, rel_bias, lb, tables):
    B, T, _ = h.shape
    G, HPG, HD = NSA_KV_HEADS, NSA_HPG, HEAD_DIM
    z = _matmul(h.reshape(B * T, D_MODEL), p['w_in'], "mm_in")[:, :D_IN].reshape(B, T, D_IN)
    (nq, nkv, ngt, gq, gk, gv, ga, gr, hq, hf, hi, hg,
     mqk, mv, mi, mf, mo, mg) = _split(z, IN_SIZES)
    q_pos = pos0 + jnp.arange(T)

    q = _rmsnorm(nq.reshape(B, T, G, HPG, HD), p['nsa_gq']) * (HD ** -0.5)
    kv = nkv.reshape(B, T, 6, G, HD)
    rows_new = jnp.stack([kv[:, :, 0], kv[:, :, 1], _rmsnorm(kv[:, :, 2], p['nsa_gk'][1]), kv[:, :, 3]], axis=2)
    win_new = jnp.stack([_rmsnorm(kv[:, :, 4], p['nsa_gk'][2]), kv[:, :, 5]], axis=2)
    rows = rows_new if past is None else jnp.concatenate([past['kv'], rows_new], axis=1)
    kc, vc, c_end = _nsa_compress(rows[:, :, 0], rows[:, :, 1], p['cmp_pe'], p['cmp_w1'], p['cmp_w2'], p['nsa_gk'][0])
    L = rows.shape[1]
    cover = _block_cover(kc.shape[1], -(-L // SEL_BLOCK))
    ks_t = rows[:, :, 2].transpose(0, 2, 1, 3)
    vs_t = rows[:, :, 3].transpose(0, 2, 1, 3)

    def attend(args):
        return _nsa_cmp_sel(args[0], args[1], kc, vc, c_end, ks_t, vs_t, rel_bias, cover)

    if past is None:
        o_nsa = _nsa_prompt(q, rows_new[:, :, 2], rows_new[:, :, 3], win_new[:, :, 0], win_new[:, :, 1],
                            kc, vc, ngt, tables)
        win_state = win_new[:, -min(WINDOW, T):]
    else:
        o_c, o_s = attend((q, q_pos))
        wb = past['win'].shape[1]
        allw = jnp.concatenate([past['win'], win_new], axis=1)
        k_pos = (pos0 - wb + jnp.arange(wb + T))[None, :]
        o_w = _window_attend(q[:, None], q_pos[None, :], allw[:, None, :, 0], allw[:, None, :, 1], k_pos, rel_bias)[:, 0]
        win_state = allw[:, -wb:]
        gts = jax.nn.sigmoid(ngt.astype(jnp.float32)).reshape(B, T, 3, G, HPG, 1).astype(h.dtype)
        o_nsa = (gts[:, :, 0] * o_c + gts[:, :, 1] * o_s + gts[:, :, 2] * o_w).reshape(B, T, -1)

    g_q = gq.reshape(B, T, GLA_HEADS, GLA_DK) * (GLA_DK ** -0.5)
    g_k = gk.reshape(B, T, GLA_HEADS, GLA_DK)
    g_v = gv.reshape(B, T, GLA_HEADS, GLA_DV)
    log_a = (jax.nn.log_sigmoid((ga @ p['gla_wa'] + p['gla_ba']).astype(jnp.float32)) / GLA_TAU).reshape(B, T, GLA_HEADS, GLA_DK)
    s0 = jnp.zeros((B, GLA_HEADS, GLA_DK, GLA_DV), h.dtype) if past is None else past['gla']
    o, gla_state = _gated_linear_attn(g_q, g_k, g_v, log_a, s0)
    o_gla = (_rmsnorm(o, p['gla_gn']) * jax.nn.silu(gr.reshape(B, T, GLA_HEADS, GLA_DV))).reshape(B, T, -1)

    log_f = jnp.logaddexp(jnp.log(lb), jnp.log1p(-lb) + jax.nn.log_sigmoid(hf.astype(jnp.float32)))
    log_f = log_f.reshape(B, T, HGRN_HEADS, HGRN_EXPAND)
    h_q = jax.nn.silu(hq).reshape(B, T, HGRN_HEADS, HGRN_EXPAND)
    h_v = hi.reshape(B, T, HGRN_HEADS, HGRN_EXPAND)
    s0h = jnp.zeros((B, HGRN_HEADS, HGRN_EXPAND, HGRN_EXPAND), h.dtype) if past is None else past['hgrn']
    o, hgrn_state = _gated_linear_attn(h_q, -jnp.expm1(log_f), h_v, log_f, s0h)
    o_hgrn = (_rmsnorm(o, p['hgrn_gn']) * jax.nn.sigmoid(hg.reshape(B, T, HGRN_HEADS, HGRN_EXPAND))).reshape(B, T, -1)

    nqk = MLSTM_HEADS * MLSTM_DQK
    buf = jnp.zeros((B, MLSTM_CONV - 1, 2 * nqk), mqk.dtype) if past is None else past['conv']
    u, conv_state = _causal_conv(mqk, buf, p['m_wconv'], p['m_bconv'])
    m_q = u[..., :nqk].reshape(B, T, MLSTM_HEADS, MLSTM_DQK)
    m_k = u[..., nqk:].reshape(B, T, MLSTM_HEADS, MLSTM_DQK) * (MLSTM_DQK ** -0.5)
    m_v = mv.reshape(B, T, MLSTM_HEADS, MLSTM_DV)
    ig = (mi + p['m_bi']).astype(jnp.float32)
    lf = jax.nn.log_sigmoid((mf + p['m_bf']).astype(jnp.float32))
    if past is None:
        C0 = jnp.zeros((B, MLSTM_HEADS, MLSTM_DV, MLSTM_DQK), h.dtype)
        n0 = jnp.zeros((B, MLSTM_HEADS, MLSTM_DQK), h.dtype)
        m0 = jnp.zeros((B, MLSTM_HEADS), h.dtype)
    else:
        C0, n0, m0 = past['mC'], past['mn'], past['mm']
    hm, mC, mn, mm = _mlstm(m_q, m_k, m_v, ig, lf, C0, n0, m0)
    o_mlstm = (jax.nn.sigmoid(mo).reshape(B, T, MLSTM_HEADS, MLSTM_DV) * _rmsnorm(hm, p['m_gn'])).reshape(B, T, -1)

    gate = jax.nn.sigmoid(mg.reshape(B, T, N_BRANCH, D_MODEL))
    merged = jnp.zeros((B, T, D_MODEL), h.dtype)
    for i, ob in enumerate((o_nsa, o_gla, o_hgrn, o_mlstm)):
        proj = _matmul(ob.reshape(B * T, BRANCH_WIDTH), p['w_branch'][i], "mm_branch").reshape(B, T, D_MODEL)
        merged = merged + gate[:, :, i] * proj
    out = _matmul(merged.reshape(B * T, D_MODEL), p['w_out'], "mm_out").reshape(B, T, D_MODEL)
    return out, (rows_new, win_state, gla_state, hgrn_state, mC, mn, mm, conv_state)


def _layer(x, mod, pos0, past, p, rel_bias, lb, tables=None):
    B, T, _ = x.shape
    sh1, sc1, gt1, sh2, sc2, gt2 = jnp.split(mod[:, None, :], 6, axis=-1)
    mix, st = _token_mixers(_rmsnorm(x, p['g_mix']) * (1 + sc1) + sh1, pos0, past, p, rel_bias, lb, tables)
    x = x + gt1 * mix
    h = _rmsnorm(x, p['g_ffn']) * (1 + sc2) + sh2
    gu = _matmul(h.reshape(B * T, D_MODEL), p['w_ffn_in'], "mm_ffn_in")
    ffp = p['w_ffn_out'].shape[0]
    act = jax.nn.silu(gu[:, :ffp]) * gu[:, ffp:]
    y = _matmul(act, p['w_ffn_out'], "mm_ffn_out").reshape(B, T, D_MODEL)
    x = x + gt2 * y
    return x, st


def kernel(x_prompt, x_sample, cache_nsa_kv, state_nsa_win, state_gla, state_hgrn, state_mlstm_C,
           state_mlstm_n, state_mlstm_m, state_mlstm_conv, page_table, c_prompt, c_sample, rel_bias,
           w_ada, b_ada, g_mix, g_ffn, w_in, nsa_gq, nsa_gk, cmp_pe, cmp_w1, cmp_w2, gla_wa, gla_ba,
           gla_gn, hgrn_lb, hgrn_gn, m_wconv, m_bconv, m_bi, m_bf, m_gn, w_branch, w_out, w_ffn_in, w_ffn_out):
    lb_cum = jnp.cumsum(jax.nn.softmax(hgrn_lb.astype(jnp.float32), axis=0), axis=0)
    lb_all = lb_cum - lb_cum[:1]
    dec_b, n_pages = page_table.shape
    past_len = n_pages * PAGE_SIZE
    n_prompt = c_prompt.shape[0]
    ffp = _round_up(FF_DIM, 1024)
    x_p, x_s = x_prompt, x_sample
    st_prompt, st_sample = [], []
    c_all = jax.nn.silu(jnp.concatenate([c_prompt, c_sample], axis=0))
    tables = _bias_tables(rel_bias, x_prompt.shape[1])
    for l in range(DEPTH):
        w_ffn_in_l = jnp.concatenate(
            [_prep_w(w_ffn_in[l][:, :FF_DIM]), _prep_w(w_ffn_in[l][:, FF_DIM:])], axis=1)
        p = {'g_mix': g_mix[l], 'g_ffn': g_ffn[l], 'w_in': _prep_w(w_in[l]),
             'nsa_gq': nsa_gq[l], 'nsa_gk': nsa_gk[l], 'cmp_pe': cmp_pe[l], 'cmp_w1': cmp_w1[l],
             'cmp_w2': cmp_w2[l], 'gla_wa': gla_wa[l], 'gla_ba': gla_ba[l], 'gla_gn': gla_gn[l],
             'hgrn_gn': hgrn_gn[l], 'm_wconv': m_wconv[l], 'm_bconv': m_bconv[l], 'm_bi': m_bi[l],
             'm_bf': m_bf[l], 'm_gn': m_gn[l],
             'w_branch': [_prep_w(w_branch[l][i]) for i in range(N_BRANCH)],
             'w_out': _prep_w(w_out[l]),
             'w_ffn_in': w_ffn_in_l, 'w_ffn_out': _prep_w(w_ffn_out[l], k_mult=1024)}
        assert p['w_ffn_out'].shape[0] == ffp
        mod = _matmul(c_all, _prep_w(w_ada[l]), "mm_ada") + b_ada[l]
        x_p, sp = _layer(x_p, mod[:n_prompt], 0, None, p, rel_bias, lb_all[l], tables)
        past = {'kv': cache_nsa_kv[l][page_table].reshape(dec_b, past_len, 4, NSA_KV_HEADS, HEAD_DIM),
                'win': state_nsa_win[l], 'gla': state_gla[l], 'hgrn': state_hgrn[l],
                'mC': state_mlstm_C[l], 'mn': state_mlstm_n[l], 'mm': state_mlstm_m[l],
                'conv': state_mlstm_conv[l]}
        x_s, ss = _layer(x_s, mod[n_prompt:], past_len, past, p, rel_bias, lb_all[l])
        st_prompt.append(sp)
        st_sample.append(ss)

    def stk(sts, i):
        return jnp.stack([s[i] for s in sts], axis=0)

    outs = [x_p, x_s]
    for i in range(8):
        outs.append(stk(st_prompt, i))
        outs.append(stk(st_sample, i))
    return tuple(outs)
```

```python
import functools
import math

import jax
import jax.numpy as jnp
import numpy as np
from jax import lax
from jax.experimental import pallas as pl
from jax.experimental.pallas import tpu as pltpu

D_MODEL = 4096
DEPTH = 2
PAGE_SIZE = 128
HEAD_DIM = 128
N_BRANCH = 4
BRANCH_WIDTH = D_MODEL // N_BRANCH
NSA_HEADS = BRANCH_WIDTH // HEAD_DIM
NSA_KV_HEADS = 2
NSA_HPG = NSA_HEADS // NSA_KV_HEADS
CMP_BLOCK = 32
CMP_STRIDE = 16
SEL_BLOCK = 64
N_SELECT = 16
WINDOW = 512
Q_BLOCK = 64
BAND = 128
REL_BUCKETS = 32
REL_MAX_DIST = 128
GLA_HEADS = 4
GLA_DK = BRANCH_WIDTH // (2 * GLA_HEADS)
GLA_DV = BRANCH_WIDTH // GLA_HEADS
GLA_GATE_RANK = 16
GLA_TAU = 16.0
HGRN_EXPAND = 128
HGRN_HEADS = BRANCH_WIDTH // HGRN_EXPAND
MLSTM_HEADS = 4
MLSTM_DQK = BRANCH_WIDTH // (2 * MLSTM_HEADS)
MLSTM_DV = BRANCH_WIDTH // MLSTM_HEADS
MLSTM_CONV = 4
CHUNK = 64
FF_DIM = ((8 * D_MODEL + 3 * 256 - 1) // (3 * 256)) * 256
EPS = 1e-6
NEG = -1e30
BIG = 1e6
IN_SIZES = (
    NSA_HEADS * HEAD_DIM, 6 * NSA_KV_HEADS * HEAD_DIM, 3 * NSA_HEADS,
    GLA_HEADS * GLA_DK, GLA_HEADS * GLA_DK, GLA_HEADS * GLA_DV, GLA_GATE_RANK, BRANCH_WIDTH,
    BRANCH_WIDTH, BRANCH_WIDTH, BRANCH_WIDTH, BRANCH_WIDTH,
    2 * MLSTM_HEADS * MLSTM_DQK, MLSTM_HEADS * MLSTM_DV, MLSTM_HEADS, MLSTM_HEADS, BRANCH_WIDTH,
    N_BRANCH * D_MODEL,
)
D_IN = sum(IN_SIZES)

V7X_VMEM_LIMIT_BYTES = 56 * 1024 * 1024
LANE = 128


def _round_up(n, m):
    return (n + m - 1) // m * m


def _mm_kernel(a_ref, w_ref, o_ref):
    o_ref[...] = jnp.dot(a_ref[...], w_ref[...], preferred_element_type=jnp.float32).astype(o_ref.dtype)


def _mm_k_kernel(a_ref, w_ref, o_ref, acc_ref):
    k = pl.program_id(2)

    @pl.when(k == 0)
    def _():
        acc_ref[...] = jnp.zeros_like(acc_ref)

    acc_ref[...] += jnp.dot(a_ref[...], w_ref[...], preferred_element_type=jnp.float32)

    @pl.when(k == pl.num_programs(2) - 1)
    def _():
        o_ref[...] = acc_ref[...].astype(o_ref.dtype)


def _mm(a, w, *, bm, bn, bk=None, out_dtype=jnp.float32, name="mm"):
    m, kd = a.shape
    n = w.shape[1]
    assert w.shape[0] == kd and m % bm == 0 and n % bn == 0, (a.shape, w.shape, bm, bn)
    params = dict(vmem_limit_bytes=V7X_VMEM_LIMIT_BYTES)
    if bk is None or bk == kd:
        return pl.pallas_call(
            _mm_kernel,
            grid=(n // bn, m // bm),
            in_specs=[pl.BlockSpec((bm, kd), lambda j, i: (i, 0)),
                      pl.BlockSpec((kd, bn), lambda j, i: (0, j))],
            out_specs=pl.BlockSpec((bm, bn), lambda j, i: (i, j)),
            out_shape=jax.ShapeDtypeStruct((m, n), out_dtype),
            compiler_params=pltpu.CompilerParams(dimension_semantics=("arbitrary", "arbitrary"), **params),
            name=name,
        )(a, w)
    assert kd % bk == 0
    return pl.pallas_call(
        _mm_k_kernel,
        grid=(n // bn, m // bm, kd // bk),
        in_specs=[pl.BlockSpec((bm, bk), lambda j, i, k: (i, k)),
                  pl.BlockSpec((bk, bn), lambda j, i, k: (k, j))],
        out_specs=pl.BlockSpec((bm, bn), lambda j, i, k: (i, j)),
        out_shape=jax.ShapeDtypeStruct((m, n), out_dtype),
        scratch_shapes=[pltpu.VMEM((bm, bn), jnp.float32)],
        compiler_params=pltpu.CompilerParams(
            dimension_semantics=("arbitrary", "arbitrary", "arbitrary"), **params),
        name=name,
    )(a, w)


def _matmul(a, w_bf16, name):
    m, kd = a.shape
    a = a.astype(jnp.bfloat16)
    mp = _round_up(m, 16)
    if mp != m:
        a = jnp.pad(a, ((0, mp - m), (0, 0)))
    bm, bn = _tiles(mp, w_bf16.shape[1])
    out = _mm(a, w_bf16, bm=bm, bn=bn, name=name)
    return out[:m] if mp != m else out


def _tiles(m, n):
    if m >= 1024:
        return 1024, 512
    return m, (1024 if n % 1024 == 0 else 512)


def _cparams(n_axes):
    return pltpu.CompilerParams(dimension_semantics=("arbitrary",) * n_axes,
                                vmem_limit_bytes=V7X_VMEM_LIMIT_BYTES)


def _proj_kernel(a_ref, w_ref, o_ref, *, act):
    y = jnp.dot(a_ref[...], w_ref[...], preferred_element_type=jnp.float32)
    if act == "sigmoid":
        y = jax.nn.sigmoid(y)
    o_ref[...] = y.astype(o_ref.dtype)


def _proj(a, w, *, out_dtype=jnp.float32, act=None, name="proj"):
    m, kd = a.shape
    n = w.shape[1]
    bm, bn = _tiles(m, n)
    return pl.pallas_call(
        functools.partial(_proj_kernel, act=act),
        grid=(n // bn, m // bm),
        in_specs=[pl.BlockSpec((bm, kd), lambda j, i: (i, 0)),
                  pl.BlockSpec((kd, bn), lambda j, i: (0, j))],
        out_specs=pl.BlockSpec((bm, bn), lambda j, i: (i, j)),
        out_shape=jax.ShapeDtypeStruct((m, n), out_dtype),
        compiler_params=_cparams(2),
        name=name,
    )(a, w)


def _mm_res_kernel(a_ref, w_ref, x_ref, gt_ref, o_ref, acc_ref):
    k = pl.program_id(2)

    @pl.when(k == 0)
    def _():
        acc_ref[...] = jnp.zeros_like(acc_ref)

    acc_ref[...] += jnp.dot(a_ref[...], w_ref[...], preferred_element_type=jnp.float32)

    @pl.when(k == pl.num_programs(2) - 1)
    def _():
        o_ref[...] = x_ref[...] + gt_ref[0] * acc_ref[...]


def _mm_res(a, w, x, gt3, *, rows_per_gate, bk=None, name="mm_res"):
    m, kd = a.shape
    n = w.shape[1]
    bm, bn = _tiles(m, n)
    bk = kd if bk is None else bk
    if rows_per_gate is None:
        gt_spec = pl.BlockSpec((1, bm, bn), lambda j, i, k: (0, i, j))
    else:
        tiles_per_gate = rows_per_gate // bm
        gt_spec = pl.BlockSpec((1, 1, bn), lambda j, i, k: (i // tiles_per_gate, 0, j))
    return pl.pallas_call(
        _mm_res_kernel,
        grid=(n // bn, m // bm, kd // bk),
        in_specs=[pl.BlockSpec((bm, bk), lambda j, i, k: (i, k)),
                  pl.BlockSpec((bk, bn), lambda j, i, k: (k, j)),
                  pl.BlockSpec((bm, bn), lambda j, i, k: (i, j)),
                  gt_spec],
        out_specs=pl.BlockSpec((bm, bn), lambda j, i, k: (i, j)),
        out_shape=jax.ShapeDtypeStruct((m, n), jnp.float32),
        scratch_shapes=[pltpu.VMEM((bm, bn), jnp.float32)],
        compiler_params=_cparams(3),
        name=name,
    )(a, w, x, gt3)


def _mm_swiglu_kernel(a_ref, wg_ref, wu_ref, o_ref):
    a = a_ref[...]
    g = jnp.dot(a, wg_ref[...], preferred_element_type=jnp.float32)
    u = jnp.dot(a, wu_ref[...], preferred_element_type=jnp.float32)
    o_ref[...] = (g * jax.nn.sigmoid(g) * u).astype(o_ref.dtype)


def _mm_swiglu(a, w_gu, name="mm_swiglu"):
    m, kd = a.shape
    f = w_gu.shape[1] // 2
    bm, bn = _tiles(m, f)
    nb = f // bn
    return pl.pallas_call(
        _mm_swiglu_kernel,
        grid=(nb, m // bm),
        in_specs=[pl.BlockSpec((bm, kd), lambda j, i: (i, 0)),
                  pl.BlockSpec((kd, bn), lambda j, i: (0, j)),
                  pl.BlockSpec((kd, bn), lambda j, i: (0, j + nb))],
        out_specs=pl.BlockSpec((bm, bn), lambda j, i: (i, j)),
        out_shape=jax.ShapeDtypeStruct((m, f), jnp.bfloat16),
        compiler_params=_cparams(2),
        name=name,
    )(a, w_gu, w_gu)


def _merge_kernel(*refs):
    obs, wb_ref, gates, o_ref = refs[:N_BRANCH], refs[N_BRANCH], refs[N_BRANCH + 1:2 * N_BRANCH + 1], refs[-1]
    acc = None
    for br in range(N_BRANCH):
        y = gates[br][...].astype(jnp.float32) * jnp.dot(obs[br][...], wb_ref[br],
                                                         preferred_element_type=jnp.float32)
        acc = y if acc is None else acc + y
    o_ref[...] = acc.astype(o_ref.dtype)


def _merge(obs, wb, gates, name="merge"):
    m, wd = obs[0].shape
    n = wb.shape[2]
    bm, bn = _tiles(m, n)
    nb = n // bn
    gate_specs = [pl.BlockSpec((bm, bn), functools.partial(lambda j, i, br: (i, br * nb + j), br=br))
                  for br in range(N_BRANCH)]
    return pl.pallas_call(
        _merge_kernel,
        grid=(nb, m // bm),
        in_specs=([pl.BlockSpec((bm, wd), lambda j, i: (i, 0))] * N_BRANCH
                  + [pl.BlockSpec((N_BRANCH, wd, bn), lambda j, i: (0, 0, j))] + gate_specs),
        out_specs=pl.BlockSpec((bm, bn), lambda j, i: (i, j)),
        out_shape=jax.ShapeDtypeStruct((m, n), jnp.bfloat16),
        compiler_params=_cparams(2),
        name=name,
    )(*obs, wb, *([gates] * N_BRANCH))


def _normmod_kernel(x_ref, g_ref, sc_ref, sh_ref, o_ref):
    x = x_ref[0]
    y = x * lax.rsqrt(jnp.mean(x * x, axis=-1, keepdims=True) + EPS) * g_ref[...]
    o_ref[0] = (y * (1.0 + sc_ref[0]) + sh_ref[0]).astype(o_ref.dtype)


def _normmod(x, g, sc, sh):
    B, T, D = x.shape
    tt = min(T, 256)
    row = pl.BlockSpec((1, 1, D), lambda b, t: (b, 0, 0))
    return pl.pallas_call(
        _normmod_kernel,
        grid=(B, T // tt),
        in_specs=[pl.BlockSpec((1, tt, D), lambda b, t: (b, t, 0)),
                  pl.BlockSpec((1, D), lambda b, t: (0, 0)), row, row],
        out_specs=pl.BlockSpec((1, tt, D), lambda b, t: (b, t, 0)),
        out_shape=jax.ShapeDtypeStruct((B, T, D), jnp.bfloat16),
        compiler_params=_cparams(2),
        name="normmod",
    )(x, g[None], sc[:, None], sh[:, None])


def _prep_w(w, n_mult=1024, k_mult=None):
    kd, n = w.shape
    np_ = _round_up(n, n_mult)
    kp = kd if k_mult is None else _round_up(kd, k_mult)
    w = w.astype(jnp.bfloat16)
    if np_ != n or kp != kd:
        w = jnp.pad(w, ((0, kp - kd), (0, np_ - n)))
    return w


NSA_TQ = 128
NSA_KB = 128
SEL_PER_KB = NSA_KB // SEL_BLOCK
WIN_TILES = WINDOW // NSA_KB + 1


def _split3_bf16(x):
    hi = x.astype(jnp.bfloat16)
    r1 = x - hi.astype(jnp.float32)
    mid = r1.astype(jnp.bfloat16)
    lo = (r1 - mid.astype(jnp.float32)).astype(jnp.bfloat16)
    return hi, mid, lo


def _nsa_prompt_kernel(qT_ref, ksel_ref, vselT_ref, kwin_ref, vwinT_ref, kc_ref, vcT_ref, covT_ref,
                       bcmp_ref, bsel_ref, bwin_ref, gate_ref, o_ref,
                       mt_ref, m_ref, l_ref, acc_ref, *, n_top):
    hpg = qT_ref.shape[2]
    tq = qT_ref.shape[4]
    ns = covT_ref.shape[0]
    qb = pl.program_id(2)
    t0 = qb * tq
    gates = jax.nn.sigmoid(gate_ref[0, 0])

    kc = kc_ref[0, 0]
    vcT = vcT_ref[0, 0]
    p_sum = jnp.zeros((kc.shape[0], tq), jnp.float32)
    for h in range(hpg):
        bias = bcmp_ref[0, h]
        s = jnp.dot(kc, qT_ref[0, 0, h], preferred_element_type=jnp.float32) + bias
        e = jnp.exp(s - jnp.max(s, axis=0, keepdims=True))
        p = jnp.where(bias > 0.5 * NEG, e / jnp.sum(e, axis=0, keepdims=True), 0.0)
        p_sum = p_sum + p
        o_c = jnp.dot(vcT, p.astype(jnp.bfloat16), preferred_element_type=jnp.float32)
        o_ref[0, 0, h] = gates[0, h:h + 1, :] * o_c
    cov = covT_ref[...]
    imp = sum(jnp.dot(cov, part, preferred_element_type=jnp.float32) for part in _split3_bf16(p_sum))

    blk = lax.broadcasted_iota(jnp.int32, (ns, tq), 0)
    cur = (t0 + lax.broadcasted_iota(jnp.int32, (ns, tq), 1)) // SEL_BLOCK
    forced = (blk == 0) | (blk == cur) | (blk == cur - 1)
    score = jnp.where(forced, BIG, jnp.where(blk <= cur, imp, -BIG))
    rank = jnp.zeros((ns, tq), jnp.int32)
    for jp in range(ns):
        row = score[jp:jp + 1, :]
        beats = (row > score) | ((row == score) & (blk > jp))
        rank = rank + beats.astype(jnp.int32)
    mt_ref[...] = (rank < n_top).astype(jnp.float32)

    sub = lax.broadcasted_iota(jnp.int32, (NSA_KB, tq), 0)

    def attend(k_ref, vT_ref, bias_ref, n_tiles, lo, use_sel):
        m_ref[...] = jnp.full(m_ref.shape, NEG, jnp.float32)
        l_ref[...] = jnp.zeros(l_ref.shape, jnp.float32)
        acc_ref[...] = jnp.zeros(acc_ref.shape, jnp.float32)

        def body(kb, carry):
            koff = pl.multiple_of(kb * NSA_KB, NSA_KB)
            k_blk = k_ref[0, 0, pl.ds(koff, NSA_KB), :]
            vT_blk = vT_ref[0, 0, :, pl.ds(koff, NSA_KB)]
            tile = jnp.minimum(qb - kb, n_tiles - 1)
            if use_sel:
                r0 = mt_ref[pl.ds(kb * SEL_PER_KB, 1), :]
                r1 = mt_ref[pl.ds(kb * SEL_PER_KB + 1, 1), :]
                selm = jnp.where(sub < SEL_BLOCK, r0, r1) > 0.5
            for h in range(hpg):
                s = jnp.dot(k_blk, qT_ref[0, 0, h], preferred_element_type=jnp.float32) + bias_ref[0, h, tile]
                if use_sel:
                    s = jnp.where(selm, s, NEG)
                m_old = m_ref[h]
                m_new = jnp.maximum(m_old, jnp.max(s, axis=0, keepdims=True))
                alpha = jnp.exp(m_old - m_new)
                p = jnp.exp(s - m_new)
                l_ref[h] = alpha * l_ref[h] + jnp.sum(p, axis=0, keepdims=True)
                acc_ref[h] = alpha * acc_ref[h] + jnp.dot(vT_blk, p.astype(jnp.bfloat16),
                                                          preferred_element_type=jnp.float32)
                m_ref[h] = m_new
            return carry

        lax.fori_loop(lo, qb + 1, body, 0)

    attend(ksel_ref, vselT_ref, bsel_ref, bsel_ref.shape[2], 0, True)
    for h in range(hpg):
        o_ref[0, 0, h] += gates[1, h:h + 1, :] * (acc_ref[h] / l_ref[h])

    attend(kwin_ref, vwinT_ref, bwin_ref, bwin_ref.shape[2], jnp.maximum(qb - (WIN_TILES - 1), 0), False)
    for h in range(hpg):
        o_ref[0, 0, h] += gates[2, h:h + 1, :] * (acc_ref[h] / l_ref[h])


def _bias_tables(rel_bias, T):
    G, HPG = NSA_KV_HEADS, NSA_HPG
    nc = (T - CMP_BLOCK) // CMP_STRIDE + 1
    ncp = _round_up(nc, LANE)

    def lookup(dist, valid):
        b = rel_bias[_t5_bucket(dist)]
        b = jnp.where(valid[..., None], b, NEG)
        return jnp.moveaxis(b, -1, 0).reshape((G, HPG) + dist.shape)

    c = jnp.arange(NSA_KB)[:, None]
    i = jnp.arange(NSA_TQ)[None, :]
    sel_d = jnp.stack([dlt + i - c for dlt in (0, NSA_KB, 2 * NSA_KB)])
    bsel = lookup(sel_d, sel_d >= 0)
    win_d = jnp.stack([dlt * NSA_KB + i - c for dlt in range(WIN_TILES)])
    bwin = lookup(win_d, (win_d >= 0) & (win_d <= WINDOW))
    n = jnp.arange(ncp)[:, None]
    t = jnp.arange(T)[None, :]
    cmp_d = t - (n * CMP_STRIDE + CMP_BLOCK - 1)
    bcmp = lookup(cmp_d, (cmp_d >= 0) & (n < nc))
    ns = -(-T // SEL_BLOCK)
    covT = _block_cover(nc, ns).T
    covT = jnp.pad(covT, ((0, 0), (0, ncp - nc))).astype(jnp.bfloat16)
    return bsel, bwin, bcmp, covT


def _nsa_prompt(q, k_sel, v_sel, k_win, v_win, kc, vc, ngt, tables):
    B, T, G, HPG, HD = q.shape
    bsel, bwin, bcmp, covT = tables
    ns, ncp = covT.shape
    nc = kc.shape[1]
    bf = jnp.bfloat16
    qT = q.astype(bf).transpose(0, 2, 3, 4, 1)
    ksel = k_sel.astype(bf).transpose(0, 2, 1, 3)
    vselT = v_sel.astype(bf).transpose(0, 2, 3, 1)
    kwin = k_win.astype(bf).transpose(0, 2, 1, 3)
    vwinT = v_win.astype(bf).transpose(0, 2, 3, 1)
    kcp = jnp.pad(kc.astype(bf).transpose(0, 2, 1, 3), ((0, 0), (0, 0), (0, ncp - nc), (0, 0)))
    vcT = jnp.pad(vc.astype(bf).transpose(0, 2, 3, 1), ((0, 0), (0, 0), (0, 0), (0, ncp - nc)))
    gT = ngt.reshape(B, T, 3, G, HPG).transpose(0, 3, 2, 4, 1)
    tq = NSA_TQ
    full = lambda b, g, i: (b, g, 0, 0)
    oT = pl.pallas_call(
        functools.partial(_nsa_prompt_kernel, n_top=min(N_SELECT, ns)),
        grid=(B, G, T // tq),
        in_specs=[
            pl.BlockSpec((1, 1, HPG, HD, tq), lambda b, g, i: (b, g, 0, 0, i)),
            pl.BlockSpec((1, 1, T, HD), full),
            pl.BlockSpec((1, 1, HD, T), full),
            pl.BlockSpec((1, 1, T, HD), full),
            pl.BlockSpec((1, 1, HD, T), full),
            pl.BlockSpec((1, 1, ncp, HD), full),
            pl.BlockSpec((1, 1, HD, ncp), full),
            pl.BlockSpec((ns, ncp), lambda b, g, i: (0, 0)),
            pl.BlockSpec((1, HPG, ncp, tq), lambda b, g, i: (g, 0, 0, i)),
            pl.BlockSpec((1, HPG) + bsel.shape[2:], lambda b, g, i: (g, 0, 0, 0, 0)),
            pl.BlockSpec((1, HPG) + bwin.shape[2:], lambda b, g, i: (g, 0, 0, 0, 0)),
            pl.BlockSpec((1, 1, 3, HPG, tq), lambda b, g, i: (b, g, 0, 0, i)),
        ],
        out_specs=pl.BlockSpec((1, 1, HPG, HD, tq), lambda b, g, i: (b, g, 0, 0, i)),
        out_shape=jax.ShapeDtypeStruct((B, G, HPG, HD, T), jnp.float32),
        scratch_shapes=[pltpu.VMEM((ns, tq), jnp.float32),
                        pltpu.VMEM((HPG, 1, tq), jnp.float32),
                        pltpu.VMEM((HPG, 1, tq), jnp.float32),
                        pltpu.VMEM((HPG, HD, tq), jnp.float32)],
        compiler_params=pltpu.CompilerParams(
            dimension_semantics=("arbitrary", "arbitrary", "arbitrary"),
            vmem_limit_bytes=V7X_VMEM_LIMIT_BYTES),
        name="nsa_prompt",
    )(qT, ksel, vselT, kwin, vwinT, kcp, vcT, covT, bcmp, bsel, bwin, gT)
    return oT.transpose(0, 4, 1, 2, 3).reshape(B, T, G * HPG * HD)


LIN_TB = 256


def _logsigmoid(x):
    return jnp.minimum(x, 0.0) - jnp.log(1.0 + jnp.exp(-jnp.abs(x)))


def _lin_attn_kernel(*refs, mode, chunk):
    if mode == "gla":
        (q_ref, k_ref, v_ref, r_ref, a_ref, wa_ref, ba_ref, gn_ref, o_ref, st_ref,
         q_s, k_s, g_s, b_s, sT_ref) = refs
    else:
        (q_ref, k_ref, v_ref, r_ref, llb_ref, l1m_ref, gn_ref, o_ref, st_ref,
         q_s, k_s, g_s, b_s, sT_ref) = refs
    tb, dk = q_s.shape
    C = chunk
    t = pl.program_id(2)

    if mode == "gla":
        q_s[...] = q_ref[0] * (dk ** -0.5)
        k_s[...] = k_ref[0]
        pre = jnp.dot(a_ref[0].astype(jnp.bfloat16), wa_ref[...], preferred_element_type=jnp.float32) + ba_ref[...]
        g_s[...] = _logsigmoid(pre) / GLA_TAU
    else:
        x = q_ref[0]
        q_s[...] = x * jax.nn.sigmoid(x)
        u = llb_ref[...]
        w = l1m_ref[...] + _logsigmoid(k_ref[0])
        lf = jnp.maximum(u, w) + jnp.log(1.0 + jnp.exp(-jnp.abs(u - w)))
        g_s[...] = lf
        k_s[...] = 1.0 - jnp.exp(lf)

    @pl.when(t == 0)
    def _():
        sT_ref[...] = jnp.zeros(sT_ref.shape, jnp.float32)

    rr = lax.broadcasted_iota(jnp.int32, (C, C), 0)
    cc = lax.broadcasted_iota(jnp.int32, (C, C), 1)
    tril = (rr >= cc).astype(jnp.bfloat16)
    row = lax.broadcasted_iota(jnp.int32, (C, dk), 0)
    gn = gn_ref[...]
    bf = jnp.bfloat16

    def chunk_body(c, carry):
        r0 = pl.multiple_of(c * C, C)
        qc = q_s[pl.ds(r0, C), :]
        kc = k_s[pl.ds(r0, C), :]
        vc = v_ref[0, pl.ds(r0, C), :].astype(bf)
        b = sum(jnp.dot(tril, part, preferred_element_type=jnp.float32) for part in _split3_bf16(g_s[pl.ds(r0, C), :]))
        b_s[...] = b
        bl = b[C - 1:C, :]

        def sbody(s, att):
            bs = b_s[pl.ds(s, 1), :]
            ks = k_s[pl.ds(r0 + s, 1), :]
            e = jnp.exp(jnp.where(row >= s, b - bs, NEG))
            col = jnp.sum(qc * ks * e, axis=-1, keepdims=True)
            return jnp.where(cc == s, col, att)

        att = lax.fori_loop(0, C, sbody, jnp.zeros((C, C), jnp.float32))
        sT = sT_ref[...]
        o = (lax.dot_general((qc * jnp.exp(b)).astype(bf), sT.astype(bf), (((1,), (1,)), ((), ())),
                             preferred_element_type=jnp.float32)
             + jnp.dot(att.astype(bf), vc, preferred_element_type=jnp.float32))
        sT_ref[...] = sT * jnp.exp(bl) + lax.dot_general(
            vc, (kc * jnp.exp(bl - b)).astype(bf), (((0,), (0,)), ((), ())), preferred_element_type=jnp.float32)
        y = o * lax.rsqrt(jnp.mean(o * o, axis=-1, keepdims=True) + EPS) * gn
        r = r_ref[0, pl.ds(r0, C), :]
        gate = jax.nn.sigmoid(r)
        if mode == "gla":
            gate = r * gate
        o_ref[0, pl.ds(r0, C), :] = (y * gate).astype(o_ref.dtype)
        return carry

    lax.fori_loop(0, tb // C, chunk_body, 0)

    @pl.when(t == pl.num_programs(2) - 1)
    def _():
        st_ref[0, 0] = sT_ref[...]


def _lin_attn_prompt(z, B, T, mode, params):
    tb = LIN_TB
    z3 = z.reshape(B, T, z.shape[-1])
    if mode == "gla":
        H, dk, dv = GLA_HEADS, GLA_DK, GLA_DV
        wa, ba, gn = params
        in_specs = [
            pl.BlockSpec((1, tb, dk), lambda b, h, t: (b, t, h)),
            pl.BlockSpec((1, tb, dk), lambda b, h, t: (b, t, H + h)),
            pl.BlockSpec((1, tb, dv), lambda b, h, t: (b, t, (2 * H * dk) // dv + h)),
            pl.BlockSpec((1, tb, dv), lambda b, h, t: (b, t, (2 * H * dk + H * dv) // dv + h)),
            pl.BlockSpec((1, tb, LANE), lambda b, h, t: (b, t, (2 * H * dk + 2 * H * dv) // LANE)),
            pl.BlockSpec((LANE, dk), lambda b, h, t: (0, h)),
            pl.BlockSpec((1, dk), lambda b, h, t: (0, h)),
            pl.BlockSpec((1, dv), lambda b, h, t: (0, 0)),
        ]
        args = [z3, z3, z3, z3, z3, wa, ba, gn]
    else:
        H, dk, dv = HGRN_HEADS, HGRN_EXPAND, HGRN_EXPAND
        llb, l1m, gn = params
        in_specs = [
            pl.BlockSpec((1, tb, dk), lambda b, h, t: (b, t, h)),
            pl.BlockSpec((1, tb, dk), lambda b, h, t: (b, t, H + h)),
            pl.BlockSpec((1, tb, dv), lambda b, h, t: (b, t, 2 * H + h)),
            pl.BlockSpec((1, tb, dv), lambda b, h, t: (b, t, 3 * H + h)),
            pl.BlockSpec((1, dk), lambda b, h, t: (0, h)),
            pl.BlockSpec((1, dk), lambda b, h, t: (0, h)),
            pl.BlockSpec((1, dv), lambda b, h, t: (0, 0)),
        ]
        args = [z3, z3, z3, z3, llb, l1m, gn]
    o, sT = pl.pallas_call(
        functools.partial(_lin_attn_kernel, mode=mode, chunk=CHUNK),
        grid=(B, H, T // tb),
        in_specs=in_specs,
        out_specs=[pl.BlockSpec((1, tb, dv), lambda b, h, t: (b, t, h)),
                   pl.BlockSpec((1, 1, dv, dk), lambda b, h, t: (b, h, 0, 0))],
        out_shape=[jax.ShapeDtypeStruct((B, T, H * dv), jnp.bfloat16),
                   jax.ShapeDtypeStruct((B, H, dv, dk), jnp.float32)],
        scratch_shapes=[pltpu.VMEM((tb, dk), jnp.float32), pltpu.VMEM((tb, dk), jnp.float32),
                        pltpu.VMEM((tb, dk), jnp.float32), pltpu.VMEM((CHUNK, dk), jnp.float32),
                        pltpu.VMEM((dv, dk), jnp.float32)],
        compiler_params=pltpu.CompilerParams(
            dimension_semantics=("arbitrary", "arbitrary", "arbitrary"),
            vmem_limit_bytes=V7X_VMEM_LIMIT_BYTES),
        name="lin_attn_" + mode,
    )(*args)
    return o.reshape(B * T, H * dv), sT.transpose(0, 1, 3, 2)


def _split(z, sizes):
    cuts = [int(c) for c in np.cumsum(sizes)[:-1]]
    return jnp.split(z, cuts, axis=-1)


def _rmsnorm(x, g):
    xf = x.astype(jnp.float32)
    y = xf * lax.rsqrt(jnp.mean(xf * xf, axis=-1, keepdims=True) + EPS)
    return (y * g.astype(jnp.float32)).astype(x.dtype)


def _masked_softmax(s, mask):
    s = jnp.where(mask, s.astype(jnp.float32), NEG)
    return jnp.where(mask, jax.nn.softmax(s, axis=-1), 0.0)


def _t5_bucket(dist):
    d = jnp.maximum(dist, 0)
    exact = REL_BUCKETS // 2
    far = exact + (jnp.log(jnp.maximum(d, 1).astype(jnp.float32) / exact)
                   / math.log(REL_MAX_DIST / exact) * (REL_BUCKETS - exact)).astype(jnp.int32)
    return jnp.where(d < exact, d, jnp.minimum(far, REL_BUCKETS - 1))


def _nsa_compress(k_rows, v_rows, pe, w1, w2, g_kc):
    B, L, G, _ = k_rows.shape
    nc = (L - CMP_BLOCK) // CMP_STRIDE + 1
    idx = jnp.arange(nc)[:, None] * CMP_STRIDE + jnp.arange(CMP_BLOCK)[None, :]

    def phi(rows, pe_, w1_, w2_):
        blk = rows[:, idx] + pe_[None, None, :, None, :]
        flat = blk.transpose(0, 1, 3, 2, 4).reshape(B, nc, G, CMP_BLOCK * HEAD_DIM)
        return jax.nn.silu(flat @ w1_) @ w2_

    kc = _rmsnorm(phi(k_rows, pe[0], w1[0], w2[0]), g_kc)
    vc = phi(v_rows, pe[1], w1[1], w2[1])
    c_end = jnp.arange(nc) * CMP_STRIDE + CMP_BLOCK - 1
    return kc, vc, c_end


def _block_cover(nc, ns):
    c0 = jnp.arange(nc)[:, None] * CMP_STRIDE
    s0 = jnp.arange(ns)[None, :] * SEL_BLOCK
    return ((c0 <= s0 + SEL_BLOCK - 1) & (c0 + CMP_BLOCK - 1 >= s0)).astype(jnp.float32)


def _nsa_cmp_sel(q, q_pos, kc, vc, c_end, ks_t, vs_t, rel_bias, cover):
    B, Tq, G, HPG, _ = q.shape
    L = ks_t.shape[2]
    ns = cover.shape[1]
    s_c = jnp.einsum('btghd,bngd->btghn', q, kc).astype(jnp.float32)
    dist_c = q_pos[:, None] - c_end[None, :]
    bias_c = rel_bias[_t5_bucket(dist_c)].reshape(Tq, -1, G, HPG).transpose(0, 2, 3, 1)
    p_c = _masked_softmax(s_c + bias_c, (dist_c >= 0)[:, None, None, :])
    o_cmp = jnp.einsum('btghn,bngd->btghd', p_c.astype(vc.dtype), vc)
    imp = jnp.einsum('btgn,ns->btgs', p_c.sum(axis=3), cover)
    blk = jnp.arange(ns)[None, :]
    cur = (q_pos // SEL_BLOCK)[:, None]
    forced = ((blk == 0) | (blk == cur) | (blk == cur - 1))[:, None, :]
    valid = (blk <= cur)[:, None, :]
    score = jnp.where(forced, BIG, jnp.where(valid, imp, -BIG))
    n_top = min(N_SELECT, ns)
    _, top = lax.top_k(score, n_top)
    tok = (top[..., None] * SEL_BLOCK + jnp.arange(SEL_BLOCK)).reshape(B, Tq, G, n_top * SEL_BLOCK)
    tok_c = jnp.minimum(tok, L - 1)
    bi = jnp.arange(B)[:, None, None, None]
    gi = jnp.arange(G)[None, None, :, None]
    kg = ks_t[bi, gi, tok_c]
    vg = vs_t[bi, gi, tok_c]
    s_s = jnp.einsum('btghd,btgsd->btghs', q, kg).astype(jnp.float32)
    dist_s = q_pos[None, :, None, None] - tok
    bias_s = rel_bias.reshape(REL_BUCKETS, G, HPG)[_t5_bucket(dist_s), gi]
    p_s = _masked_softmax(s_s + bias_s.transpose(0, 1, 2, 4, 3), (dist_s >= 0)[:, :, :, None, :])
    o_sel = jnp.einsum('btghs,btgsd->btghd', p_s.astype(vg.dtype), vg)
    return o_cmp, o_sel


def _window_attend(q, q_pos, k, v, k_pos, rel_bias):
    N, Qb = q_pos.shape
    Kb = k_pos.shape[1]
    G, HPG = q.shape[3], q.shape[4]
    s = jnp.einsum('bnqghd,bnkgd->bnqghk', q, k).astype(jnp.float32)
    dist = q_pos[:, :, None] - k_pos[:, None, :]
    bias = rel_bias[_t5_bucket(dist)].reshape(N, Qb, Kb, G, HPG).transpose(0, 1, 3, 4, 2)
    mask = ((dist >= 0) & (dist <= WINDOW) & (k_pos[:, None, :] >= 0))[:, :, None, None, :]
    p = _masked_softmax(s + bias, mask)
    return jnp.einsum('bnqghk,bnkgd->bnqghd', p.astype(v.dtype), v)


def _gated_linear_attn(q, k, v, log_a, s0):
    B, T, H, _ = q.shape
    dv = v.shape[-1]
    C = CHUNK if T % CHUNK == 0 else T
    n = T // C

    def chunks(a):
        return a.astype(jnp.float32).reshape(B, n, C, H, a.shape[-1]).transpose(1, 0, 3, 2, 4)

    causal = jnp.tril(jnp.ones((C, C), bool))

    def step(S, inp):
        qc, kc, vc, gc = inp
        b = jnp.cumsum(gc, axis=2)
        diff = jnp.where(causal[:, :, None], b[:, :, :, None, :] - b[:, :, None, :, :], -jnp.inf)
        att = jnp.einsum('bhtk,bhsk,bhtsk->bhts', qc, kc, jnp.exp(diff))
        o = (jnp.einsum('bhtk,bhkv->bhtv', qc * jnp.exp(b), S)
             + jnp.einsum('bhts,bhsv->bhtv', att, vc))
        b_last = b[:, :, -1:, :]
        S = (jnp.exp(b_last[:, :, 0, :])[..., None] * S
             + jnp.einsum('bhsk,bhsv->bhkv', kc * jnp.exp(b_last - b), vc))
        return S, o

    S, o = lax.scan(step, s0.astype(jnp.float32), (chunks(q), chunks(k), chunks(v), chunks(log_a)))
    o = o.transpose(1, 0, 3, 2, 4).reshape(B, T, H, dv)
    return o.astype(v.dtype), S.astype(s0.dtype)


def _mlstm(q, k, v, ig, lf, C0, n0, m0):
    B, T, H, _ = q.shape
    dv = v.shape[-1]
    C = CHUNK if T % CHUNK == 0 else T
    n = T // C

    def chunks(a):
        return a.astype(jnp.float32).reshape(B, n, C, H, a.shape[-1]).transpose(1, 0, 3, 2, 4)

    def gchunks(a):
        return a.astype(jnp.float32).reshape(B, n, C, H).transpose(1, 0, 3, 2)

    causal = jnp.tril(jnp.ones((C, C), bool))

    def step(carry, inp):
        Cs, ns_, m = carry
        qc, kc, vc, ic, fc = inp
        F = jnp.cumsum(fc, axis=-1)
        logw = jnp.where(causal, F[..., :, None] - F[..., None, :] + ic[..., None, :], -jnp.inf)
        from_state = F + m[..., None]
        m_hat = jnp.maximum(from_state, logw.max(-1))
        w = jnp.exp(logw - m_hat[..., None]) * jnp.einsum('bhtk,bhsk->bhts', qc, kc)
        ws = jnp.exp(from_state - m_hat)
        num = ws[..., None] * jnp.einsum('bhtk,bhvk->bhtv', qc, Cs) + jnp.einsum('bhts,bhsv->bhtv', w, vc)
        den = ws * jnp.einsum('bhtk,bhk->bht', qc, ns_) + w.sum(-1)
        h = num / jnp.maximum(jnp.abs(den), jnp.exp(-m_hat))[..., None]
        m_new = m_hat[..., -1]
        ds = jnp.exp(F[..., -1:] - F + ic - m_new[..., None])
        dst = jnp.exp(F[..., -1] + m - m_new)
        Cs = dst[..., None, None] * Cs + jnp.einsum('bhs,bhsv,bhsk->bhvk', ds, vc, kc)
        ns_ = dst[..., None] * ns_ + jnp.einsum('bhs,bhsk->bhk', ds, kc)
        return (Cs, ns_, m_new), h

    carry0 = (C0.astype(jnp.float32), n0.astype(jnp.float32), m0.astype(jnp.float32))
    (Cf, nf, mf), h = lax.scan(step, carry0, (chunks(q), chunks(k), chunks(v), gchunks(ig), gchunks(lf)))
    h = h.transpose(1, 0, 3, 2, 4).reshape(B, T, H, dv)
    return h.astype(v.dtype), Cf.astype(C0.dtype), nf.astype(n0.dtype), mf.astype(m0.dtype)


def _causal_conv(u, buf, w, b):
    T = u.shape[1]
    up = jnp.concatenate([buf, u], axis=1)
    y = b
    for j in range(MLSTM_CONV):
        y = y + up[:, j:j + T] * w[j]
    return jax.nn.silu(y), up[:, -(MLSTM_CONV - 1):]


GROUPS = {
    'nsa': (('nq', 'nkv', 'ngt'), 3072),
    'gla': (('gq', 'gk', 'gv', 'gr', 'ga'), 3584),
    'hgrn': (('hq', 'hf', 'hi', 'hg'), 4096),
    'mlstm': (('mqk', 'mv', 'mo', 'mi', 'mf'), 3584),
    'gate': (('mg',), N_BRANCH * D_MODEL),
}
IN_NAMES = ('nq', 'nkv', 'ngt', 'gq', 'gk', 'gv', 'ga', 'gr', 'hq', 'hf', 'hi', 'hg',
            'mqk', 'mv', 'mi', 'mf', 'mo', 'mg')
IN_WIDTH = dict(zip(IN_NAMES, IN_SIZES))


def _group_weights(w_in_l):
    cuts = dict(zip(IN_NAMES, np.cumsum((0,) + IN_SIZES[:-1])))
    out = {}
    for gname, (members, width) in GROUPS.items():
        w = jnp.concatenate([w_in_l[:, int(cuts[m]):int(cuts[m]) + IN_WIDTH[m]] for m in members], axis=1)
        w = w.astype(jnp.bfloat16)
        out[gname] = jnp.pad(w, ((0, 0), (0, width - w.shape[1])))
    return out


def _group_fields(zs, gname, B, T):
    out, off = {}, 0
    for m in GROUPS[gname][0]:
        out[m] = zs[gname][:, off:off + IN_WIDTH[m]].reshape(B, T, IN_WIDTH[m])
        off += IN_WIDTH[m]
    return out


def _token_mixers(zs, B, T, pos0, past, p, rel_bias, lb, tables):
    G, HPG, HD = NSA_KV_HEADS, NSA_HPG, HEAD_DIM
    f = _group_fields(zs, 'nsa', B, T)
    nq, nkv, ngt = f['nq'], f['nkv'], f['ngt']
    f = _group_fields(zs, 'mlstm', B, T)
    mqk, mv, mo, mi, mf = f['mqk'], f['mv'], f['mo'], f['mi'], f['mf']
    q_pos = pos0 + jnp.arange(T)

    q = _rmsnorm(nq.reshape(B, T, G, HPG, HD), p['nsa_gq']) * (HD ** -0.5)
    kv = nkv.reshape(B, T, 6, G, HD)
    rows_new = jnp.stack([kv[:, :, 0], kv[:, :, 1], _rmsnorm(kv[:, :, 2], p['nsa_gk'][1]), kv[:, :, 3]], axis=2)
    win_new = jnp.stack([_rmsnorm(kv[:, :, 4], p['nsa_gk'][2]), kv[:, :, 5]], axis=2)
    rows = rows_new if past is None else jnp.concatenate([past['kv'], rows_new], axis=1)
    kc, vc, c_end = _nsa_compress(rows[:, :, 0], rows[:, :, 1], p['cmp_pe'], p['cmp_w1'], p['cmp_w2'], p['nsa_gk'][0])
    L = rows.shape[1]
    cover = _block_cover(kc.shape[1], -(-L // SEL_BLOCK))
    ks_t = rows[:, :, 2].transpose(0, 2, 1, 3)
    vs_t = rows[:, :, 3].transpose(0, 2, 1, 3)

    def attend(args):
        return _nsa_cmp_sel(args[0], args[1], kc, vc, c_end, ks_t, vs_t, rel_bias, cover)

    if past is None:
        o_nsa = _nsa_prompt(q, rows_new[:, :, 2], rows_new[:, :, 3], win_new[:, :, 0], win_new[:, :, 1],
                            kc, vc, ngt, tables)
        win_state = win_new[:, -min(WINDOW, T):]
    else:
        o_c, o_s = attend((q, q_pos))
        wb = past['win'].shape[1]
        allw = jnp.concatenate([past['win'], win_new], axis=1)
        k_pos = (pos0 - wb + jnp.arange(wb + T))[None, :]
        o_w = _window_attend(q[:, None], q_pos[None, :], allw[:, None, :, 0], allw[:, None, :, 1], k_pos, rel_bias)[:, 0]
        win_state = allw[:, -wb:]
        gts = jax.nn.sigmoid(ngt.astype(jnp.float32)).reshape(B, T, 3, G, HPG, 1)
        o_nsa = (gts[:, :, 0] * o_c + gts[:, :, 1] * o_s + gts[:, :, 2] * o_w).reshape(B, T, -1)

    if past is None:
        wa_p = jnp.pad(p['gla_wa'], ((0, LANE - GLA_GATE_RANK), (0, 0))).astype(jnp.bfloat16)
        o_gla, gla_state = _lin_attn_prompt(zs['gla'], B, T, "gla", (wa_p, p['gla_ba'][None], p['gla_gn'][None]))
        o_hgrn, hgrn_state = _lin_attn_prompt(zs['hgrn'], B, T, "hgrn",
                                              (jnp.log(lb)[None], jnp.log1p(-lb)[None], p['hgrn_gn'][None]))
    else:
        f = _group_fields(zs, 'gla', B, T)
        gq, gk, gv, gr, ga = f['gq'], f['gk'], f['gv'], f['gr'], f['ga']
        g_q = gq.reshape(B, T, GLA_HEADS, GLA_DK) * (GLA_DK ** -0.5)
        g_k = gk.reshape(B, T, GLA_HEADS, GLA_DK)
        g_v = gv.reshape(B, T, GLA_HEADS, GLA_DV)
        log_a = (jax.nn.log_sigmoid((ga @ p['gla_wa'] + p['gla_ba']).astype(jnp.float32)) / GLA_TAU).reshape(B, T, GLA_HEADS, GLA_DK)
        o, gla_state = _gated_linear_attn(g_q, g_k, g_v, log_a, past['gla'])
        o_gla = (_rmsnorm(o, p['gla_gn']) * jax.nn.silu(gr.reshape(B, T, GLA_HEADS, GLA_DV))).reshape(B * T, -1)

        f = _group_fields(zs, 'hgrn', B, T)
        hq, hf, hi, hg = f['hq'], f['hf'], f['hi'], f['hg']
        log_f = jnp.logaddexp(jnp.log(lb), jnp.log1p(-lb) + jax.nn.log_sigmoid(hf.astype(jnp.float32)))
        log_f = log_f.reshape(B, T, HGRN_HEADS, HGRN_EXPAND)
        h_q = jax.nn.silu(hq).reshape(B, T, HGRN_HEADS, HGRN_EXPAND)
        h_v = hi.reshape(B, T, HGRN_HEADS, HGRN_EXPAND)
        o, hgrn_state = _gated_linear_attn(h_q, -jnp.expm1(log_f), h_v, log_f, past['hgrn'])
        o_hgrn = (_rmsnorm(o, p['hgrn_gn']) * jax.nn.sigmoid(hg.reshape(B, T, HGRN_HEADS, HGRN_EXPAND))).reshape(B * T, -1)

    nqk = MLSTM_HEADS * MLSTM_DQK
    buf = jnp.zeros((B, MLSTM_CONV - 1, 2 * nqk), mqk.dtype) if past is None else past['conv']
    u, conv_state = _causal_conv(mqk, buf, p['m_wconv'], p['m_bconv'])
    m_q = u[..., :nqk].reshape(B, T, MLSTM_HEADS, MLSTM_DQK)
    m_k = u[..., nqk:].reshape(B, T, MLSTM_HEADS, MLSTM_DQK) * (MLSTM_DQK ** -0.5)
    m_v = mv.reshape(B, T, MLSTM_HEADS, MLSTM_DV)
    ig = (mi + p['m_bi']).astype(jnp.float32)
    lf = jax.nn.log_sigmoid((mf + p['m_bf']).astype(jnp.float32))
    if past is None:
        C0 = jnp.zeros((B, MLSTM_HEADS, MLSTM_DV, MLSTM_DQK), jnp.float32)
        n0 = jnp.zeros((B, MLSTM_HEADS, MLSTM_DQK), jnp.float32)
        m0 = jnp.zeros((B, MLSTM_HEADS), jnp.float32)
    else:
        C0, n0, m0 = past['mC'], past['mn'], past['mm']
    hm, mC, mn, mm = _mlstm(m_q, m_k, m_v, ig, lf, C0, n0, m0)
    o_mlstm = (jax.nn.sigmoid(mo).reshape(B, T, MLSTM_HEADS, MLSTM_DV) * _rmsnorm(hm, p['m_gn'])).reshape(B, T, -1)

    obs = [o.reshape(B * T, BRANCH_WIDTH).astype(jnp.bfloat16) for o in (o_nsa, o_gla, o_hgrn, o_mlstm)]
    return obs, (rows_new, win_state, gla_state, hgrn_state, mC, mn, mm, conv_state)


def _layer(x, mod, pos0, past, p, rel_bias, lb, tables=None):
    B, T, _ = x.shape
    m = B * T
    mp = _round_up(m, 16)

    def rows(a):
        a = a.reshape(m, a.shape[-1])
        return a if mp == m else jnp.pad(a, ((0, mp - m), (0, 0)))

    sh1, sc1, gt1, sh2, sc2, gt2 = jnp.split(mod, 6, axis=-1)
    if T > 1:
        gates3 = lambda g: (g[:, None, :], T)
    else:
        gates3 = lambda g: (rows(g[:, None, :])[None], None)
    hb = rows(_normmod(x, p['g_mix'], sc1, sh1))
    zs = {g: _proj(hb, p['w_' + g], name="proj_" + g) for g in ('nsa', 'gla', 'hgrn', 'mlstm')}
    gates = _proj(hb, p['w_gate'], out_dtype=jnp.bfloat16, act="sigmoid", name="proj_gate")
    obs, st = _token_mixers({g: z[:m] for g, z in zs.items()}, B, T, pos0, past, p, rel_bias, lb, tables)
    merged = _merge([rows(o) for o in obs], p['w_branch'], gates)
    g3, rpg = gates3(gt1)
    x1 = _mm_res(merged, p['w_out'], rows(x), g3, rows_per_gate=rpg, name="mm_out")
    hb2 = rows(_normmod(x1[:m].reshape(B, T, D_MODEL), p['g_ffn'], sc2, sh2))
    act = _mm_swiglu(hb2, p['w_ffn_in'], name="mm_ffn_in")
    g3, rpg = gates3(gt2)
    x2 = _mm_res(act, p['w_ffn_out'], x1, g3, rows_per_gate=rpg, bk=p['w_ffn_out'].shape[0] // 4, name="mm_ffn_out")
    return x2[:m].reshape(B, T, D_MODEL), st


def kernel(x_prompt, x_sample, cache_nsa_kv, state_nsa_win, state_gla, state_hgrn, state_mlstm_C,
           state_mlstm_n, state_mlstm_m, state_mlstm_conv, page_table, c_prompt, c_sample, rel_bias,
           w_ada, b_ada, g_mix, g_ffn, w_in, nsa_gq, nsa_gk, cmp_pe, cmp_w1, cmp_w2, gla_wa, gla_ba,
           gla_gn, hgrn_lb, hgrn_gn, m_wconv, m_bconv, m_bi, m_bf, m_gn, w_branch, w_out, w_ffn_in, w_ffn_out):
    lb_cum = jnp.cumsum(jax.nn.softmax(hgrn_lb.astype(jnp.float32), axis=0), axis=0)
    lb_all = lb_cum - lb_cum[:1]
    dec_b, n_pages = page_table.shape
    past_len = n_pages * PAGE_SIZE
    n_prompt = c_prompt.shape[0]
    ffp = _round_up(FF_DIM, 1024)
    x_p, x_s = x_prompt, x_sample
    st_prompt, st_sample = [], []
    c_all = jax.nn.silu(jnp.concatenate([c_prompt, c_sample], axis=0))
    tables = _bias_tables(rel_bias, x_prompt.shape[1])
    for l in range(DEPTH):
        w_ffn_in_l = jnp.concatenate(
            [_prep_w(w_ffn_in[l][:, :FF_DIM]), _prep_w(w_ffn_in[l][:, FF_DIM:])], axis=1)
        p = {'g_mix': g_mix[l], 'g_ffn': g_ffn[l],
             'nsa_gq': nsa_gq[l], 'nsa_gk': nsa_gk[l], 'cmp_pe': cmp_pe[l], 'cmp_w1': cmp_w1[l],
             'cmp_w2': cmp_w2[l], 'gla_wa': gla_wa[l], 'gla_ba': gla_ba[l], 'gla_gn': gla_gn[l],
             'hgrn_gn': hgrn_gn[l], 'm_wconv': m_wconv[l], 'm_bconv': m_bconv[l], 'm_bi': m_bi[l],
             'm_bf': m_bf[l], 'm_gn': m_gn[l],
             'w_branch': w_branch[l].astype(jnp.bfloat16),
             'w_out': _prep_w(w_out[l]),
             'w_ffn_in': w_ffn_in_l, 'w_ffn_out': _prep_w(w_ffn_out[l], k_mult=1024)}
        p.update({'w_' + g: w for g, w in _group_weights(w_in[l]).items()})
        assert p['w_ffn_out'].shape[0] == ffp
        mod = _matmul(c_all, _prep_w(w_ada[l]), "mm_ada") + b_ada[l]
        x_p, sp = _layer(x_p, mod[:n_prompt], 0, None, p, rel_bias, lb_all[l], tables)
        past = {'kv': cache_nsa_kv[l][page_table].reshape(dec_b, past_len, 4, NSA_KV_HEADS, HEAD_DIM),
                'win': state_nsa_win[l], 'gla': state_gla[l], 'hgrn': state_hgrn[l],
                'mC': state_mlstm_C[l], 'mn': state_mlstm_n[l], 'mm': state_mlstm_m[l],
                'conv': state_mlstm_conv[l]}
        x_s, ss = _layer(x_s, mod[n_prompt:], past_len, past, p, rel_bias, lb_all[l])
        st_prompt.append(sp)
        st_sample.append(ss)

    def stk(sts, i):
        return jnp.stack([s[i] for s in sts], axis=0)

    outs = [x_p, x_s]
    for i in range(8):
        outs.append(stk(st_prompt, i))
        outs.append(stk(st_sample, i))
    return tuple(outs)
```

```python
import functools
import math

import jax
import jax.numpy as jnp
import numpy as np
from jax import lax
from jax.experimental import pallas as pl
from jax.experimental.pallas import tpu as pltpu

D_MODEL = 4096
DEPTH = 2
PAGE_SIZE = 128
HEAD_DIM = 128
N_BRANCH = 4
BRANCH_WIDTH = D_MODEL // N_BRANCH
NSA_HEADS = BRANCH_WIDTH // HEAD_DIM
NSA_KV_HEADS = 2
NSA_HPG = NSA_HEADS // NSA_KV_HEADS
CMP_BLOCK = 32
CMP_STRIDE = 16
SEL_BLOCK = 64
N_SELECT = 16
WINDOW = 512
Q_BLOCK = 64
BAND = 128
REL_BUCKETS = 32
REL_MAX_DIST = 128
GLA_HEADS = 4
GLA_DK = BRANCH_WIDTH // (2 * GLA_HEADS)
GLA_DV = BRANCH_WIDTH // GLA_HEADS
GLA_GATE_RANK = 16
GLA_TAU = 16.0
HGRN_EXPAND = 128
HGRN_HEADS = BRANCH_WIDTH // HGRN_EXPAND
MLSTM_HEADS = 4
MLSTM_DQK = BRANCH_WIDTH // (2 * MLSTM_HEADS)
MLSTM_DV = BRANCH_WIDTH // MLSTM_HEADS
MLSTM_CONV = 4
CHUNK = 64
FF_DIM = ((8 * D_MODEL + 3 * 256 - 1) // (3 * 256)) * 256
EPS = 1e-6
NEG = -1e30
BIG = 1e6
IN_SIZES = (
    NSA_HEADS * HEAD_DIM, 6 * NSA_KV_HEADS * HEAD_DIM, 3 * NSA_HEADS,
    GLA_HEADS * GLA_DK, GLA_HEADS * GLA_DK, GLA_HEADS * GLA_DV, GLA_GATE_RANK, BRANCH_WIDTH,
    BRANCH_WIDTH, BRANCH_WIDTH, BRANCH_WIDTH, BRANCH_WIDTH,
    2 * MLSTM_HEADS * MLSTM_DQK, MLSTM_HEADS * MLSTM_DV, MLSTM_HEADS, MLSTM_HEADS, BRANCH_WIDTH,
    N_BRANCH * D_MODEL,
)
D_IN = sum(IN_SIZES)

V7X_VMEM_LIMIT_BYTES = 56 * 1024 * 1024
LANE = 128
SUB = 8


def _round_up(n, m):
    return (n + m - 1) // m * m


def _mm_kernel(a_ref, w_ref, o_ref):
    o_ref[...] = jnp.dot(a_ref[...], w_ref[...], preferred_element_type=jnp.float32).astype(o_ref.dtype)


def _mm_k_kernel(a_ref, w_ref, o_ref, acc_ref):
    k = pl.program_id(2)

    @pl.when(k == 0)
    def _():
        acc_ref[...] = jnp.zeros_like(acc_ref)

    acc_ref[...] += jnp.dot(a_ref[...], w_ref[...], preferred_element_type=jnp.float32)

    @pl.when(k == pl.num_programs(2) - 1)
    def _():
        o_ref[...] = acc_ref[...].astype(o_ref.dtype)


def _mm(a, w, *, bm, bn, bk=None, out_dtype=jnp.float32, name="mm"):
    m, kd = a.shape
    n = w.shape[1]
    assert w.shape[0] == kd and m % bm == 0 and n % bn == 0, (a.shape, w.shape, bm, bn)
    params = dict(vmem_limit_bytes=V7X_VMEM_LIMIT_BYTES)
    if bk is None or bk == kd:
        return pl.pallas_call(
            _mm_kernel,
            grid=(n // bn, m // bm),
            in_specs=[pl.BlockSpec((bm, kd), lambda j, i: (i, 0)),
                      pl.BlockSpec((kd, bn), lambda j, i: (0, j))],
            out_specs=pl.BlockSpec((bm, bn), lambda j, i: (i, j)),
            out_shape=jax.ShapeDtypeStruct((m, n), out_dtype),
            compiler_params=pltpu.CompilerParams(dimension_semantics=("arbitrary", "arbitrary"), **params),
            name=name,
        )(a, w)
    assert kd % bk == 0
    return pl.pallas_call(
        _mm_k_kernel,
        grid=(n // bn, m // bm, kd // bk),
        in_specs=[pl.BlockSpec((bm, bk), lambda j, i, k: (i, k)),
                  pl.BlockSpec((bk, bn), lambda j, i, k: (k, j))],
        out_specs=pl.BlockSpec((bm, bn), lambda j, i, k: (i, j)),
        out_shape=jax.ShapeDtypeStruct((m, n), out_dtype),
        scratch_shapes=[pltpu.VMEM((bm, bn), jnp.float32)],
        compiler_params=pltpu.CompilerParams(
            dimension_semantics=("arbitrary", "arbitrary", "arbitrary"), **params),
        name=name,
    )(a, w)


def _matmul(a, w_bf16, name):
    m, kd = a.shape
    a = a.astype(jnp.bfloat16)
    mp = _round_up(m, 16)
    if mp != m:
        a = jnp.pad(a, ((0, mp - m), (0, 0)))
    bm, bn = _tiles(mp, w_bf16.shape[1])
    out = _mm(a, w_bf16, bm=bm, bn=bn, name=name)
    return out[:m] if mp != m else out


def _tiles(m, n):
    if m >= 1024:
        return 1024, 512
    return m, (1024 if n % 1024 == 0 else 512)


def _cparams(n_axes):
    return pltpu.CompilerParams(dimension_semantics=("arbitrary",) * n_axes,
                                vmem_limit_bytes=V7X_VMEM_LIMIT_BYTES)


def _proj_kernel(a_ref, w_ref, o_ref, *, act):
    y = jnp.dot(a_ref[...], w_ref[...], preferred_element_type=jnp.float32)
    if act == "sigmoid":
        y = jax.nn.sigmoid(y)
    o_ref[...] = y.astype(o_ref.dtype)


def _proj(a, w, *, out_dtype=jnp.float32, act=None, name="proj"):
    m, kd = a.shape
    n = w.shape[1]
    bm, bn = _tiles(m, n)
    return pl.pallas_call(
        functools.partial(_proj_kernel, act=act),
        grid=(n // bn, m // bm),
        in_specs=[pl.BlockSpec((bm, kd), lambda j, i: (i, 0)),
                  pl.BlockSpec((kd, bn), lambda j, i: (0, j))],
        out_specs=pl.BlockSpec((bm, bn), lambda j, i: (i, j)),
        out_shape=jax.ShapeDtypeStruct((m, n), out_dtype),
        compiler_params=_cparams(2),
        name=name,
    )(a, w)


def _mm_res_kernel(a_ref, w_ref, x_ref, gt_ref, o_ref, acc_ref):
    k = pl.program_id(2)

    @pl.when(k == 0)
    def _():
        acc_ref[...] = jnp.zeros_like(acc_ref)

    acc_ref[...] += jnp.dot(a_ref[...], w_ref[...], preferred_element_type=jnp.float32)

    @pl.when(k == pl.num_programs(2) - 1)
    def _():
        o_ref[...] = x_ref[...] + gt_ref[0] * acc_ref[...]


def _mm_res(a, w, x, gt3, *, rows_per_gate, bk=None, name="mm_res"):
    m, kd = a.shape
    n = w.shape[1]
    bm, bn = _tiles(m, n)
    bk = kd if bk is None else bk
    if rows_per_gate is None:
        gt_spec = pl.BlockSpec((1, bm, bn), lambda j, i, k: (0, i, j))
    else:
        tiles_per_gate = rows_per_gate // bm
        gt_spec = pl.BlockSpec((1, 1, bn), lambda j, i, k: (i // tiles_per_gate, 0, j))
    return pl.pallas_call(
        _mm_res_kernel,
        grid=(n // bn, m // bm, kd // bk),
        in_specs=[pl.BlockSpec((bm, bk), lambda j, i, k: (i, k)),
                  pl.BlockSpec((bk, bn), lambda j, i, k: (k, j)),
                  pl.BlockSpec((bm, bn), lambda j, i, k: (i, j)),
                  gt_spec],
        out_specs=pl.BlockSpec((bm, bn), lambda j, i, k: (i, j)),
        out_shape=jax.ShapeDtypeStruct((m, n), jnp.float32),
        scratch_shapes=[pltpu.VMEM((bm, bn), jnp.float32)],
        compiler_params=_cparams(3),
        name=name,
    )(a, w, x, gt3)


def _mm_swiglu_kernel(a_ref, wg_ref, wu_ref, o_ref):
    a = a_ref[...]
    g = jnp.dot(a, wg_ref[...], preferred_element_type=jnp.float32)
    u = jnp.dot(a, wu_ref[...], preferred_element_type=jnp.float32)
    o_ref[...] = (g * jax.nn.sigmoid(g) * u).astype(o_ref.dtype)


def _mm_swiglu(a, w_gu, name="mm_swiglu"):
    m, kd = a.shape
    f = w_gu.shape[1] // 2
    bm, bn = _tiles(m, f)
    nb = f // bn
    return pl.pallas_call(
        _mm_swiglu_kernel,
        grid=(nb, m // bm),
        in_specs=[pl.BlockSpec((bm, kd), lambda j, i: (i, 0)),
                  pl.BlockSpec((kd, bn), lambda j, i: (0, j)),
                  pl.BlockSpec((kd, bn), lambda j, i: (0, j + nb))],
        out_specs=pl.BlockSpec((bm, bn), lambda j, i: (i, j)),
        out_shape=jax.ShapeDtypeStruct((m, f), jnp.bfloat16),
        compiler_params=_cparams(2),
        name=name,
    )(a, w_gu, w_gu)


def _merge_kernel(*refs):
    obs, wb_ref, gates, o_ref = refs[:N_BRANCH], refs[N_BRANCH], refs[N_BRANCH + 1:2 * N_BRANCH + 1], refs[-1]
    acc = None
    for br in range(N_BRANCH):
        y = gates[br][...].astype(jnp.float32) * jnp.dot(obs[br][...], wb_ref[br],
                                                         preferred_element_type=jnp.float32)
        acc = y if acc is None else acc + y
    o_ref[...] = acc.astype(o_ref.dtype)


def _merge(obs, wb, gates, name="merge"):
    m, wd = obs[0].shape
    n = wb.shape[2]
    bm, bn = _tiles(m, n)
    nb = n // bn
    gate_specs = [pl.BlockSpec((bm, bn), functools.partial(lambda j, i, br: (i, br * nb + j), br=br))
                  for br in range(N_BRANCH)]
    return pl.pallas_call(
        _merge_kernel,
        grid=(nb, m // bm),
        in_specs=([pl.BlockSpec((bm, wd), lambda j, i: (i, 0))] * N_BRANCH
                  + [pl.BlockSpec((N_BRANCH, wd, bn), lambda j, i: (0, 0, j))] + gate_specs),
        out_specs=pl.BlockSpec((bm, bn), lambda j, i: (i, j)),
        out_shape=jax.ShapeDtypeStruct((m, n), jnp.bfloat16),
        compiler_params=_cparams(2),
        name=name,
    )(*obs, wb, *([gates] * N_BRANCH))


def _normmod_kernel(x_ref, g_ref, sc_ref, sh_ref, o_ref):
    x = x_ref[0]
    y = x * lax.rsqrt(jnp.mean(x * x, axis=-1, keepdims=True) + EPS) * g_ref[...]
    o_ref[0] = (y * (1.0 + sc_ref[0]) + sh_ref[0]).astype(o_ref.dtype)


def _normmod(x, g, sc, sh):
    B, T, D = x.shape
    tt = min(T, 256)
    row = pl.BlockSpec((1, 1, D), lambda b, t: (b, 0, 0))
    return pl.pallas_call(
        _normmod_kernel,
        grid=(B, T // tt),
        in_specs=[pl.BlockSpec((1, tt, D), lambda b, t: (b, t, 0)),
                  pl.BlockSpec((1, D), lambda b, t: (0, 0)), row, row],
        out_specs=pl.BlockSpec((1, tt, D), lambda b, t: (b, t, 0)),
        out_shape=jax.ShapeDtypeStruct((B, T, D), jnp.bfloat16),
        compiler_params=_cparams(2),
        name="normmod",
    )(x, g[None], sc[:, None], sh[:, None])


def _prep_w(w, n_mult=1024, k_mult=None):
    kd, n = w.shape
    np_ = _round_up(n, n_mult)
    kp = kd if k_mult is None else _round_up(kd, k_mult)
    w = w.astype(jnp.bfloat16)
    if np_ != n or kp != kd:
        w = jnp.pad(w, ((0, kp - kd), (0, np_ - n)))
    return w


NSA_TQ = 128
NSA_KB = 128
SEL_PER_KB = NSA_KB // SEL_BLOCK
WIN_TILES = WINDOW // NSA_KB + 1


def _split3_bf16(x):
    hi = x.astype(jnp.bfloat16)
    r1 = x - hi.astype(jnp.float32)
    mid = r1.astype(jnp.bfloat16)
    lo = (r1 - mid.astype(jnp.float32)).astype(jnp.bfloat16)
    return hi, mid, lo


def _nsa_prompt_kernel(qT_ref, ksel_ref, vselT_ref, kwin_ref, vwinT_ref, kc_ref, vcT_ref, covT_ref,
                       bcmp_ref, bsel_ref, bwin_ref, gate_ref, o_ref,
                       mt_ref, m_ref, l_ref, acc_ref, *, n_top):
    hpg = qT_ref.shape[2]
    tq = qT_ref.shape[4]
    ns = covT_ref.shape[0]
    qb = pl.program_id(2)
    t0 = qb * tq
    gates = jax.nn.sigmoid(gate_ref[0, 0])

    kc = kc_ref[0, 0]
    vcT = vcT_ref[0, 0]
    p_sum = jnp.zeros((kc.shape[0], tq), jnp.float32)
    for h in range(hpg):
        bias = bcmp_ref[0, h]
        s = jnp.dot(kc, qT_ref[0, 0, h], preferred_element_type=jnp.float32) + bias
        e = jnp.exp(s - jnp.max(s, axis=0, keepdims=True))
        p = jnp.where(bias > 0.5 * NEG, e / jnp.sum(e, axis=0, keepdims=True), 0.0)
        p_sum = p_sum + p
        o_c = jnp.dot(vcT, p.astype(jnp.bfloat16), preferred_element_type=jnp.float32)
        o_ref[0, 0, h] = gates[0, h:h + 1, :] * o_c
    cov = covT_ref[...]
    imp = sum(jnp.dot(cov, part, preferred_element_type=jnp.float32) for part in _split3_bf16(p_sum))

    blk = lax.broadcasted_iota(jnp.int32, (ns, tq), 0)
    cur = (t0 + lax.broadcasted_iota(jnp.int32, (ns, tq), 1)) // SEL_BLOCK
    forced = (blk == 0) | (blk == cur) | (blk == cur - 1)
    score = jnp.where(forced, BIG, jnp.where(blk <= cur, imp, -BIG))
    rank = jnp.zeros((ns, tq), jnp.int32)
    for jp in range(ns):
        row = score[jp:jp + 1, :]
        beats = (row > score) | ((row == score) & (blk > jp))
        rank = rank + beats.astype(jnp.int32)
    mt_ref[...] = (rank < n_top).astype(jnp.float32)

    sub = lax.broadcasted_iota(jnp.int32, (NSA_KB, tq), 0)

    def attend(k_ref, vT_ref, bias_ref, n_tiles, lo, use_sel):
        m_ref[...] = jnp.full(m_ref.shape, NEG, jnp.float32)
        l_ref[...] = jnp.zeros(l_ref.shape, jnp.float32)
        acc_ref[...] = jnp.zeros(acc_ref.shape, jnp.float32)

        def body(kb, carry):
            koff = pl.multiple_of(kb * NSA_KB, NSA_KB)
            k_blk = k_ref[0, 0, pl.ds(koff, NSA_KB), :]
            vT_blk = vT_ref[0, 0, :, pl.ds(koff, NSA_KB)]
            tile = jnp.minimum(qb - kb, n_tiles - 1)
            if use_sel:
                r0 = mt_ref[pl.ds(kb * SEL_PER_KB, 1), :]
                r1 = mt_ref[pl.ds(kb * SEL_PER_KB + 1, 1), :]
                selm = jnp.where(sub < SEL_BLOCK, r0, r1) > 0.5
            for h in range(hpg):
                s = jnp.dot(k_blk, qT_ref[0, 0, h], preferred_element_type=jnp.float32) + bias_ref[0, h, tile]
                if use_sel:
                    s = jnp.where(selm, s, NEG)
                m_old = m_ref[h]
                m_new = jnp.maximum(m_old, jnp.max(s, axis=0, keepdims=True))
                alpha = jnp.exp(m_old - m_new)
                p = jnp.exp(s - m_new)
                l_ref[h] = alpha * l_ref[h] + jnp.sum(p, axis=0, keepdims=True)
                acc_ref[h] = alpha * acc_ref[h] + jnp.dot(vT_blk, p.astype(jnp.bfloat16),
                                                          preferred_element_type=jnp.float32)
                m_ref[h] = m_new
            return carry

        lax.fori_loop(lo, qb + 1, body, 0)

    attend(ksel_ref, vselT_ref, bsel_ref, bsel_ref.shape[2], 0, True)
    for h in range(hpg):
        o_ref[0, 0, h] += gates[1, h:h + 1, :] * (acc_ref[h] / l_ref[h])

    attend(kwin_ref, vwinT_ref, bwin_ref, bwin_ref.shape[2], jnp.maximum(qb - (WIN_TILES - 1), 0), False)
    for h in range(hpg):
        o_ref[0, 0, h] += gates[2, h:h + 1, :] * (acc_ref[h] / l_ref[h])


def _bias_tables(rel_bias, T):
    G, HPG = NSA_KV_HEADS, NSA_HPG
    nc = (T - CMP_BLOCK) // CMP_STRIDE + 1
    ncp = _round_up(nc, LANE)

    def lookup(dist, valid):
        b = rel_bias[_t5_bucket(dist)]
        b = jnp.where(valid[..., None], b, NEG)
        return jnp.moveaxis(b, -1, 0).reshape((G, HPG) + dist.shape)

    c = jnp.arange(NSA_KB)[:, None]
    i = jnp.arange(NSA_TQ)[None, :]
    sel_d = jnp.stack([dlt + i - c for dlt in (0, NSA_KB, 2 * NSA_KB)])
    bsel = lookup(sel_d, sel_d >= 0)
    win_d = jnp.stack([dlt * NSA_KB + i - c for dlt in range(WIN_TILES)])
    bwin = lookup(win_d, (win_d >= 0) & (win_d <= WINDOW))
    n = jnp.arange(ncp)[:, None]
    t = jnp.arange(T)[None, :]
    cmp_d = t - (n * CMP_STRIDE + CMP_BLOCK - 1)
    bcmp = lookup(cmp_d, (cmp_d >= 0) & (n < nc))
    ns = -(-T // SEL_BLOCK)
    covT = _block_cover(nc, ns).T
    covT = jnp.pad(covT, ((0, 0), (0, ncp - nc))).astype(jnp.bfloat16)
    return bsel, bwin, bcmp, covT


def _nsa_prompt(q, k_sel, v_sel, k_win, v_win, kc, vc, ngt, tables):
    B, T, G, HPG, HD = q.shape
    bsel, bwin, bcmp, covT = tables
    ns, ncp = covT.shape
    nc = kc.shape[1]
    bf = jnp.bfloat16
    qT = q.astype(bf).transpose(0, 2, 3, 4, 1)
    ksel = k_sel.astype(bf).transpose(0, 2, 1, 3)
    vselT = v_sel.astype(bf).transpose(0, 2, 3, 1)
    kwin = k_win.astype(bf).transpose(0, 2, 1, 3)
    vwinT = v_win.astype(bf).transpose(0, 2, 3, 1)
    kcp = jnp.pad(kc.astype(bf).transpose(0, 2, 1, 3), ((0, 0), (0, 0), (0, ncp - nc), (0, 0)))
    vcT = jnp.pad(vc.astype(bf).transpose(0, 2, 3, 1), ((0, 0), (0, 0), (0, 0), (0, ncp - nc)))
    gT = ngt.reshape(B, T, 3, G, HPG).transpose(0, 3, 2, 4, 1)
    tq = NSA_TQ
    full = lambda b, g, i: (b, g, 0, 0)
    oT = pl.pallas_call(
        functools.partial(_nsa_prompt_kernel, n_top=min(N_SELECT, ns)),
        grid=(B, G, T // tq),
        in_specs=[
            pl.BlockSpec((1, 1, HPG, HD, tq), lambda b, g, i: (b, g, 0, 0, i)),
            pl.BlockSpec((1, 1, T, HD), full),
            pl.BlockSpec((1, 1, HD, T), full),
            pl.BlockSpec((1, 1, T, HD), full),
            pl.BlockSpec((1, 1, HD, T), full),
            pl.BlockSpec((1, 1, ncp, HD), full),
            pl.BlockSpec((1, 1, HD, ncp), full),
            pl.BlockSpec((ns, ncp), lambda b, g, i: (0, 0)),
            pl.BlockSpec((1, HPG, ncp, tq), lambda b, g, i: (g, 0, 0, i)),
            pl.BlockSpec((1, HPG) + bsel.shape[2:], lambda b, g, i: (g, 0, 0, 0, 0)),
            pl.BlockSpec((1, HPG) + bwin.shape[2:], lambda b, g, i: (g, 0, 0, 0, 0)),
            pl.BlockSpec((1, 1, 3, HPG, tq), lambda b, g, i: (b, g, 0, 0, i)),
        ],
        out_specs=pl.BlockSpec((1, 1, HPG, HD, tq), lambda b, g, i: (b, g, 0, 0, i)),
        out_shape=jax.ShapeDtypeStruct((B, G, HPG, HD, T), jnp.float32),
        scratch_shapes=[pltpu.VMEM((ns, tq), jnp.float32),
                        pltpu.VMEM((HPG, 1, tq), jnp.float32),
                        pltpu.VMEM((HPG, 1, tq), jnp.float32),
                        pltpu.VMEM((HPG, HD, tq), jnp.float32)],
        compiler_params=pltpu.CompilerParams(
            dimension_semantics=("arbitrary", "arbitrary", "arbitrary"),
            vmem_limit_bytes=V7X_VMEM_LIMIT_BYTES),
        name="nsa_prompt",
    )(qT, ksel, vselT, kwin, vwinT, kcp, vcT, covT, bcmp, bsel, bwin, gT)
    return oT.transpose(0, 4, 1, 2, 3).reshape(B, T, G * HPG * HD)


LIN_TB = 256


def _logsigmoid(x):
    return jnp.minimum(x, 0.0) - jnp.log(1.0 + jnp.exp(-jnp.abs(x)))


def _lin_attn_kernel(*refs, mode, chunk):
    if mode == "gla":
        (q_ref, k_ref, v_ref, r_ref, a_ref, wa_ref, ba_ref, gn_ref, o_ref, st_ref,
         q_s, k_s, g_s, sT_ref) = refs
    else:
        (q_ref, k_ref, v_ref, r_ref, llb_ref, l1m_ref, gn_ref, o_ref, st_ref,
         q_s, k_s, g_s, sT_ref) = refs
    tb, dk = q_s.shape
    C = chunk
    t = pl.program_id(2)

    if mode == "gla":
        q_s[...] = q_ref[0] * (dk ** -0.5)
        k_s[...] = k_ref[0]
        pre = jnp.dot(a_ref[0].astype(jnp.bfloat16), wa_ref[...], preferred_element_type=jnp.float32) + ba_ref[...]
        g_s[...] = _logsigmoid(pre) / GLA_TAU
    else:
        x = q_ref[0]
        q_s[...] = x * jax.nn.sigmoid(x)
        u = llb_ref[...]
        w = l1m_ref[...] + _logsigmoid(k_ref[0])
        lf = jnp.maximum(u, w) + jnp.log(1.0 + jnp.exp(-jnp.abs(u - w)))
        g_s[...] = lf
        k_s[...] = 1.0 - jnp.exp(lf)

    @pl.when(t == 0)
    def _():
        sT_ref[...] = jnp.zeros(sT_ref.shape, jnp.float32)

    rr = lax.broadcasted_iota(jnp.int32, (C, C), 0)
    cc = lax.broadcasted_iota(jnp.int32, (C, C), 1)
    tril = (rr >= cc).astype(jnp.bfloat16)
    row8 = lax.broadcasted_iota(jnp.int32, (SUB, dk), 0)
    cc8 = lax.broadcasted_iota(jnp.int32, (SUB, C), 1)
    gn = gn_ref[...]
    bf = jnp.bfloat16

    def chunk_body(c, carry):
        r0 = pl.multiple_of(c * C, C)
        qc = q_s[pl.ds(r0, C), :]
        kc = k_s[pl.ds(r0, C), :]
        vc = v_ref[0, pl.ds(r0, C), :].astype(bf)
        b = sum(jnp.dot(tril, part, preferred_element_type=jnp.float32) for part in _split3_bf16(g_s[pl.ds(r0, C), :]))
        bl = b[C - 1:C, :]

        nt = C // SUB
        b_t = [b[i * SUB:(i + 1) * SUB] for i in range(nt)]
        q_t = [qc[i * SUB:(i + 1) * SUB] for i in range(nt)]
        att_t = [jnp.zeros((SUB, C), jnp.float32) for _ in range(nt)]
        for s in range(C):
            bs = b[s:s + 1, :]
            ks = kc[s:s + 1, :]
            for i in range(s // SUB, nt):
                d = b_t[i] - bs
                if i == s // SUB and s % SUB:
                    d = jnp.where(row8 >= s % SUB, d, NEG)
                col = jnp.sum(q_t[i] * ks * jnp.exp(d), axis=-1, keepdims=True)
                att_t[i] = jnp.where(cc8 == s, col, att_t[i])
        att = jnp.concatenate(att_t, axis=0)
        sT = sT_ref[...]
        o = (lax.dot_general((qc * jnp.exp(b)).astype(bf), sT.astype(bf), (((1,), (1,)), ((), ())),
                             preferred_element_type=jnp.float32)
             + jnp.dot(att.astype(bf), vc, preferred_element_type=jnp.float32))
        sT_ref[...] = sT * jnp.exp(bl) + lax.dot_general(
            vc, (kc * jnp.exp(bl - b)).astype(bf), (((0,), (0,)), ((), ())), preferred_element_type=jnp.float32)
        y = o * lax.rsqrt(jnp.mean(o * o, axis=-1, keepdims=True) + EPS) * gn
        r = r_ref[0, pl.ds(r0, C), :]
        gate = jax.nn.sigmoid(r)
        if mode == "gla":
            gate = r * gate
        o_ref[0, pl.ds(r0, C), :] = (y * gate).astype(o_ref.dtype)
        return carry

    lax.fori_loop(0, tb // C, chunk_body, 0)

    @pl.when(t == pl.num_programs(2) - 1)
    def _():
        st_ref[0, 0] = sT_ref[...]


def _lin_attn_prompt(z, B, T, mode, params):
    tb = LIN_TB
    z3 = z.reshape(B, T, z.shape[-1])
    if mode == "gla":
        H, dk, dv = GLA_HEADS, GLA_DK, GLA_DV
        wa, ba, gn = params
        in_specs = [
            pl.BlockSpec((1, tb, dk), lambda b, h, t: (b, t, h)),
            pl.BlockSpec((1, tb, dk), lambda b, h, t: (b, t, H + h)),
            pl.BlockSpec((1, tb, dv), lambda b, h, t: (b, t, (2 * H * dk) // dv + h)),
            pl.BlockSpec((1, tb, dv), lambda b, h, t: (b, t, (2 * H * dk + H * dv) // dv + h)),
            pl.BlockSpec((1, tb, LANE), lambda b, h, t: (b, t, (2 * H * dk + 2 * H * dv) // LANE)),
            pl.BlockSpec((LANE, dk), lambda b, h, t: (0, h)),
            pl.BlockSpec((1, dk), lambda b, h, t: (0, h)),
            pl.BlockSpec((1, dv), lambda b, h, t: (0, 0)),
        ]
        args = [z3, z3, z3, z3, z3, wa, ba, gn]
    else:
        H, dk, dv = HGRN_HEADS, HGRN_EXPAND, HGRN_EXPAND
        llb, l1m, gn = params
        in_specs = [
            pl.BlockSpec((1, tb, dk), lambda b, h, t: (b, t, h)),
            pl.BlockSpec((1, tb, dk), lambda b, h, t: (b, t, H + h)),
            pl.BlockSpec((1, tb, dv), lambda b, h, t: (b, t, 2 * H + h)),
            pl.BlockSpec((1, tb, dv), lambda b, h, t: (b, t, 3 * H + h)),
            pl.BlockSpec((1, dk), lambda b, h, t: (0, h)),
            pl.BlockSpec((1, dk), lambda b, h, t: (0, h)),
            pl.BlockSpec((1, dv), lambda b, h, t: (0, 0)),
        ]
        args = [z3, z3, z3, z3, llb, l1m, gn]
    o, sT = pl.pallas_call(
        functools.partial(_lin_attn_kernel, mode=mode, chunk=CHUNK),
        grid=(B, H, T // tb),
        in_specs=in_specs,
        out_specs=[pl.BlockSpec((1, tb, dv), lambda b, h, t: (b, t, h)),
                   pl.BlockSpec((1, 1, dv, dk), lambda b, h, t: (b, h, 0, 0))],
        out_shape=[jax.ShapeDtypeStruct((B, T, H * dv), jnp.bfloat16),
                   jax.ShapeDtypeStruct((B, H, dv, dk), jnp.float32)],
        scratch_shapes=[pltpu.VMEM((tb, dk), jnp.float32), pltpu.VMEM((tb, dk), jnp.float32),
                        pltpu.VMEM((tb, dk), jnp.float32), pltpu.VMEM((dv, dk), jnp.float32)],
        compiler_params=pltpu.CompilerParams(
            dimension_semantics=("arbitrary", "arbitrary", "arbitrary"),
            vmem_limit_bytes=V7X_VMEM_LIMIT_BYTES),
        name="lin_attn_" + mode,
    )(*args)
    return o.reshape(B * T, H * dv), sT.transpose(0, 1, 3, 2)


def _split(z, sizes):
    cuts = [int(c) for c in np.cumsum(sizes)[:-1]]
    return jnp.split(z, cuts, axis=-1)


def _rmsnorm(x, g):
    xf = x.astype(jnp.float32)
    y = xf * lax.rsqrt(jnp.mean(xf * xf, axis=-1, keepdims=True) + EPS)
    return (y * g.astype(jnp.float32)).astype(x.dtype)


def _masked_softmax(s, mask):
    s = jnp.where(mask, s.astype(jnp.float32), NEG)
    return jnp.where(mask, jax.nn.softmax(s, axis=-1), 0.0)


def _t5_bucket(dist):
    d = jnp.maximum(dist, 0)
    exact = REL_BUCKETS // 2
    far = exact + (jnp.log(jnp.maximum(d, 1).astype(jnp.float32) / exact)
                   / math.log(REL_MAX_DIST / exact) * (REL_BUCKETS - exact)).astype(jnp.int32)
    return jnp.where(d < exact, d, jnp.minimum(far, REL_BUCKETS - 1))


def _nsa_compress(k_rows, v_rows, pe, w1, w2, g_kc):
    B, L, G, _ = k_rows.shape
    nc = (L - CMP_BLOCK) // CMP_STRIDE + 1
    idx = jnp.arange(nc)[:, None] * CMP_STRIDE + jnp.arange(CMP_BLOCK)[None, :]

    def phi(rows, pe_, w1_, w2_):
        blk = rows[:, idx] + pe_[None, None, :, None, :]
        flat = blk.transpose(0, 1, 3, 2, 4).reshape(B, nc, G, CMP_BLOCK * HEAD_DIM)
        return jax.nn.silu(flat @ w1_) @ w2_

    kc = _rmsnorm(phi(k_rows, pe[0], w1[0], w2[0]), g_kc)
    vc = phi(v_rows, pe[1], w1[1], w2[1])
    c_end = jnp.arange(nc) * CMP_STRIDE + CMP_BLOCK - 1
    return kc, vc, c_end


def _block_cover(nc, ns):
    c0 = jnp.arange(nc)[:, None] * CMP_STRIDE
    s0 = jnp.arange(ns)[None, :] * SEL_BLOCK
    return ((c0 <= s0 + SEL_BLOCK - 1) & (c0 + CMP_BLOCK - 1 >= s0)).astype(jnp.float32)


def _nsa_cmp_sel(q, q_pos, kc, vc, c_end, ks_t, vs_t, rel_bias, cover):
    B, Tq, G, HPG, _ = q.shape
    L = ks_t.shape[2]
    ns = cover.shape[1]
    s_c = jnp.einsum('btghd,bngd->btghn', q, kc).astype(jnp.float32)
    dist_c = q_pos[:, None] - c_end[None, :]
    bias_c = rel_bias[_t5_bucket(dist_c)].reshape(Tq, -1, G, HPG).transpose(0, 2, 3, 1)
    p_c = _masked_softmax(s_c + bias_c, (dist_c >= 0)[:, None, None, :])
    o_cmp = jnp.einsum('btghn,bngd->btghd', p_c.astype(vc.dtype), vc)
    imp = jnp.einsum('btgn,ns->btgs', p_c.sum(axis=3), cover)
    blk = jnp.arange(ns)[None, :]
    cur = (q_pos // SEL_BLOCK)[:, None]
    forced = ((blk == 0) | (blk == cur) | (blk == cur - 1))[:, None, :]
    valid = (blk <= cur)[:, None, :]
    score = jnp.where(forced, BIG, jnp.where(valid, imp, -BIG))
    n_top = min(N_SELECT, ns)
    _, top = lax.top_k(score, n_top)
    tok = (top[..., None] * SEL_BLOCK + jnp.arange(SEL_BLOCK)).reshape(B, Tq, G, n_top * SEL_BLOCK)
    tok_c = jnp.minimum(tok, L - 1)
    bi = jnp.arange(B)[:, None, None, None]
    gi = jnp.arange(G)[None, None, :, None]
    kg = ks_t[bi, gi, tok_c]
    vg = vs_t[bi, gi, tok_c]
    s_s = jnp.einsum('btghd,btgsd->btghs', q, kg).astype(jnp.float32)
    dist_s = q_pos[None, :, None, None] - tok
    bias_s = rel_bias.reshape(REL_BUCKETS, G, HPG)[_t5_bucket(dist_s), gi]
    p_s = _masked_softmax(s_s + bias_s.transpose(0, 1, 2, 4, 3), (dist_s >= 0)[:, :, :, None, :])
    o_sel = jnp.einsum('btghs,btgsd->btghd', p_s.astype(vg.dtype), vg)
    return o_cmp, o_sel


def _window_attend(q, q_pos, k, v, k_pos, rel_bias):
    N, Qb = q_pos.shape
    Kb = k_pos.shape[1]
    G, HPG = q.shape[3], q.shape[4]
    s = jnp.einsum('bnqghd,bnkgd->bnqghk', q, k).astype(jnp.float32)
    dist = q_pos[:, :, None] - k_pos[:, None, :]
    bias = rel_bias[_t5_bucket(dist)].reshape(N, Qb, Kb, G, HPG).transpose(0, 1, 3, 4, 2)
    mask = ((dist >= 0) & (dist <= WINDOW) & (k_pos[:, None, :] >= 0))[:, :, None, None, :]
    p = _masked_softmax(s + bias, mask)
    return jnp.einsum('bnqghk,bnkgd->bnqghd', p.astype(v.dtype), v)


def _gated_linear_attn(q, k, v, log_a, s0):
    B, T, H, _ = q.shape
    dv = v.shape[-1]
    C = CHUNK if T % CHUNK == 0 else T
    n = T // C

    def chunks(a):
        return a.astype(jnp.float32).reshape(B, n, C, H, a.shape[-1]).transpose(1, 0, 3, 2, 4)

    causal = jnp.tril(jnp.ones((C, C), bool))

    def step(S, inp):
        qc, kc, vc, gc = inp
        b = jnp.cumsum(gc, axis=2)
        diff = jnp.where(causal[:, :, None], b[:, :, :, None, :] - b[:, :, None, :, :], -jnp.inf)
        att = jnp.einsum('bhtk,bhsk,bhtsk->bhts', qc, kc, jnp.exp(diff))
        o = (jnp.einsum('bhtk,bhkv->bhtv', qc * jnp.exp(b), S)
             + jnp.einsum('bhts,bhsv->bhtv', att, vc))
        b_last = b[:, :, -1:, :]
        S = (jnp.exp(b_last[:, :, 0, :])[..., None] * S
             + jnp.einsum('bhsk,bhsv->bhkv', kc * jnp.exp(b_last - b), vc))
        return S, o

    S, o = lax.scan(step, s0.astype(jnp.float32), (chunks(q), chunks(k), chunks(v), chunks(log_a)))
    o = o.transpose(1, 0, 3, 2, 4).reshape(B, T, H, dv)
    return o.astype(v.dtype), S.astype(s0.dtype)


def _mlstm(q, k, v, ig, lf, C0, n0, m0):
    B, T, H, _ = q.shape
    dv = v.shape[-1]
    C = CHUNK if T % CHUNK == 0 else T
    n = T // C

    def chunks(a):
        return a.astype(jnp.float32).reshape(B, n, C, H, a.shape[-1]).transpose(1, 0, 3, 2, 4)

    def gchunks(a):
        return a.astype(jnp.float32).reshape(B, n, C, H).transpose(1, 0, 3, 2)

    causal = jnp.tril(jnp.ones((C, C), bool))

    def step(carry, inp):
        Cs, ns_, m = carry
        qc, kc, vc, ic, fc = inp
        F = jnp.cumsum(fc, axis=-1)
        logw = jnp.where(causal, F[..., :, None] - F[..., None, :] + ic[..., None, :], -jnp.inf)
        from_state = F + m[..., None]
        m_hat = jnp.maximum(from_state, logw.max(-1))
        w = jnp.exp(logw - m_hat[..., None]) * jnp.einsum('bhtk,bhsk->bhts', qc, kc)
        ws = jnp.exp(from_state - m_hat)
        num = ws[..., None] * jnp.einsum('bhtk,bhvk->bhtv', qc, Cs) + jnp.einsum('bhts,bhsv->bhtv', w, vc)
        den = ws * jnp.einsum('bhtk,bhk->bht', qc, ns_) + w.sum(-1)
        h = num / jnp.maximum(jnp.abs(den), jnp.exp(-m_hat))[..., None]
        m_new = m_hat[..., -1]
        ds = jnp.exp(F[..., -1:] - F + ic - m_new[..., None])
        dst = jnp.exp(F[..., -1] + m - m_new)
        Cs = dst[..., None, None] * Cs + jnp.einsum('bhs,bhsv,bhsk->bhvk', ds, vc, kc)
        ns_ = dst[..., None] * ns_ + jnp.einsum('bhs,bhsk->bhk', ds, kc)
        return (Cs, ns_, m_new), h

    carry0 = (C0.astype(jnp.float32), n0.astype(jnp.float32), m0.astype(jnp.float32))
    (Cf, nf, mf), h = lax.scan(step, carry0, (chunks(q), chunks(k), chunks(v), gchunks(ig), gchunks(lf)))
    h = h.transpose(1, 0, 3, 2, 4).reshape(B, T, H, dv)
    return h.astype(v.dtype), Cf.astype(C0.dtype), nf.astype(n0.dtype), mf.astype(m0.dtype)


def _causal_conv(u, buf, w, b):
    T = u.shape[1]
    up = jnp.concatenate([buf, u], axis=1)
    y = b
    for j in range(MLSTM_CONV):
        y = y + up[:, j:j + T] * w[j]
    return jax.nn.silu(y), up[:, -(MLSTM_CONV - 1):]


GROUPS = {
    'nsa': (('nq', 'nkv', 'ngt'), 3072),
    'gla': (('gq', 'gk', 'gv', 'gr', 'ga'), 3584),
    'hgrn': (('hq', 'hf', 'hi', 'hg'), 4096),
    'mlstm': (('mqk', 'mv', 'mo', 'mi', 'mf'), 3584),
    'gate': (('mg',), N_BRANCH * D_MODEL),
}
IN_NAMES = ('nq', 'nkv', 'ngt', 'gq', 'gk', 'gv', 'ga', 'gr', 'hq', 'hf', 'hi', 'hg',
            'mqk', 'mv', 'mi', 'mf', 'mo', 'mg')
IN_WIDTH = dict(zip(IN_NAMES, IN_SIZES))


def _group_weights(w_in_l):
    cuts = dict(zip(IN_NAMES, np.cumsum((0,) + IN_SIZES[:-1])))
    out = {}
    for gname, (members, width) in GROUPS.items():
        w = jnp.concatenate([w_in_l[:, int(cuts[m]):int(cuts[m]) + IN_WIDTH[m]] for m in members], axis=1)
        w = w.astype(jnp.bfloat16)
        out[gname] = jnp.pad(w, ((0, 0), (0, width - w.shape[1])))
    return out


def _group_fields(zs, gname, B, T):
    out, off = {}, 0
    for m in GROUPS[gname][0]:
        out[m] = zs[gname][:, off:off + IN_WIDTH[m]].reshape(B, T, IN_WIDTH[m])
        off += IN_WIDTH[m]
    return out


def _token_mixers(zs, B, T, pos0, past, p, rel_bias, lb, tables):
    G, HPG, HD = NSA_KV_HEADS, NSA_HPG, HEAD_DIM
    f = _group_fields(zs, 'nsa', B, T)
    nq, nkv, ngt = f['nq'], f['nkv'], f['ngt']
    f = _group_fields(zs, 'mlstm', B, T)
    mqk, mv, mo, mi, mf = f['mqk'], f['mv'], f['mo'], f['mi'], f['mf']
    q_pos = pos0 + jnp.arange(T)

    q = _rmsnorm(nq.reshape(B, T, G, HPG, HD), p['nsa_gq']) * (HD ** -0.5)
    kv = nkv.reshape(B, T, 6, G, HD)
    rows_new = jnp.stack([kv[:, :, 0], kv[:, :, 1], _rmsnorm(kv[:, :, 2], p['nsa_gk'][1]), kv[:, :, 3]], axis=2)
    win_new = jnp.stack([_rmsnorm(kv[:, :, 4], p['nsa_gk'][2]), kv[:, :, 5]], axis=2)
    rows = rows_new if past is None else jnp.concatenate([past['kv'], rows_new], axis=1)
    kc, vc, c_end = _nsa_compress(rows[:, :, 0], rows[:, :, 1], p['cmp_pe'], p['cmp_w1'], p['cmp_w2'], p['nsa_gk'][0])
    L = rows.shape[1]
    cover = _block_cover(kc.shape[1], -(-L // SEL_BLOCK))
    ks_t = rows[:, :, 2].transpose(0, 2, 1, 3)
    vs_t = rows[:, :, 3].transpose(0, 2, 1, 3)

    def attend(args):
        return _nsa_cmp_sel(args[0], args[1], kc, vc, c_end, ks_t, vs_t, rel_bias, cover)

    if past is None:
        o_nsa = _nsa_prompt(q, rows_new[:, :, 2], rows_new[:, :, 3], win_new[:, :, 0], win_new[:, :, 1],
                            kc, vc, ngt, tables)
        win_state = win_new[:, -min(WINDOW, T):]
    else:
        o_c, o_s = attend((q, q_pos))
        wb = past['win'].shape[1]
        allw = jnp.concatenate([past['win'], win_new], axis=1)
        k_pos = (pos0 - wb + jnp.arange(wb + T))[None, :]
        o_w = _window_attend(q[:, None], q_pos[None, :], allw[:, None, :, 0], allw[:, None, :, 1], k_pos, rel_bias)[:, 0]
        win_state = allw[:, -wb:]
        gts = jax.nn.sigmoid(ngt.astype(jnp.float32)).reshape(B, T, 3, G, HPG, 1)
        o_nsa = (gts[:, :, 0] * o_c + gts[:, :, 1] * o_s + gts[:, :, 2] * o_w).reshape(B, T, -1)

    if past is None:
        wa_p = jnp.pad(p['gla_wa'], ((0, LANE - GLA_GATE_RANK), (0, 0))).astype(jnp.bfloat16)
        o_gla, gla_state = _lin_attn_prompt(zs['gla'], B, T, "gla", (wa_p, p['gla_ba'][None], p['gla_gn'][None]))
        o_hgrn, hgrn_state = _lin_attn_prompt(zs['hgrn'], B, T, "hgrn",
                                              (jnp.log(lb)[None], jnp.log1p(-lb)[None], p['hgrn_gn'][None]))
    else:
        f = _group_fields(zs, 'gla', B, T)
        gq, gk, gv, gr, ga = f['gq'], f['gk'], f['gv'], f['gr'], f['ga']
        g_q = gq.reshape(B, T, GLA_HEADS, GLA_DK) * (GLA_DK ** -0.5)
        g_k = gk.reshape(B, T, GLA_HEADS, GLA_DK)
        g_v = gv.reshape(B, T, GLA_HEADS, GLA_DV)
        log_a = (jax.nn.log_sigmoid((ga @ p['gla_wa'] + p['gla_ba']).astype(jnp.float32)) / GLA_TAU).reshape(B, T, GLA_HEADS, GLA_DK)
        o, gla_state = _gated_linear_attn(g_q, g_k, g_v, log_a, past['gla'])
        o_gla = (_rmsnorm(o, p['gla_gn']) * jax.nn.silu(gr.reshape(B, T, GLA_HEADS, GLA_DV))).reshape(B * T, -1)

        f = _group_fields(zs, 'hgrn', B, T)
        hq, hf, hi, hg = f['hq'], f['hf'], f['hi'], f['hg']
        log_f = jnp.logaddexp(jnp.log(lb), jnp.log1p(-lb) + jax.nn.log_sigmoid(hf.astype(jnp.float32)))
        log_f = log_f.reshape(B, T, HGRN_HEADS, HGRN_EXPAND)
        h_q = jax.nn.silu(hq).reshape(B, T, HGRN_HEADS, HGRN_EXPAND)
        h_v = hi.reshape(B, T, HGRN_HEADS, HGRN_EXPAND)
        o, hgrn_state = _gated_linear_attn(h_q, -jnp.expm1(log_f), h_v, log_f, past['hgrn'])
        o_hgrn = (_rmsnorm(o, p['hgrn_gn']) * jax.nn.sigmoid(hg.reshape(B, T, HGRN_HEADS, HGRN_EXPAND))).reshape(B * T, -1)

    nqk = MLSTM_HEADS * MLSTM_DQK
    buf = jnp.zeros((B, MLSTM_CONV - 1, 2 * nqk), mqk.dtype) if past is None else past['conv']
    u, conv_state = _causal_conv(mqk, buf, p['m_wconv'], p['m_bconv'])
    m_q = u[..., :nqk].reshape(B, T, MLSTM_HEADS, MLSTM_DQK)
    m_k = u[..., nqk:].reshape(B, T, MLSTM_HEADS, MLSTM_DQK) * (MLSTM_DQK ** -0.5)
    m_v = mv.reshape(B, T, MLSTM_HEADS, MLSTM_DV)
    ig = (mi + p['m_bi']).astype(jnp.float32)
    lf = jax.nn.log_sigmoid((mf + p['m_bf']).astype(jnp.float32))
    if past is None:
        C0 = jnp.zeros((B, MLSTM_HEADS, MLSTM_DV, MLSTM_DQK), jnp.float32)
        n0 = jnp.zeros((B, MLSTM_HEADS, MLSTM_DQK), jnp.float32)
        m0 = jnp.zeros((B, MLSTM_HEADS), jnp.float32)
    else:
        C0, n0, m0 = past['mC'], past['mn'], past['mm']
    hm, mC, mn, mm = _mlstm(m_q, m_k, m_v, ig, lf, C0, n0, m0)
    o_mlstm = (jax.nn.sigmoid(mo).reshape(B, T, MLSTM_HEADS, MLSTM_DV) * _rmsnorm(hm, p['m_gn'])).reshape(B, T, -1)

    obs = [o.reshape(B * T, BRANCH_WIDTH).astype(jnp.bfloat16) for o in (o_nsa, o_gla, o_hgrn, o_mlstm)]
    return obs, (rows_new, win_state, gla_state, hgrn_state, mC, mn, mm, conv_state)


def _layer(x, mod, pos0, past, p, rel_bias, lb, tables=None):
    B, T, _ = x.shape
    m = B * T
    mp = _round_up(m, 16)

    def rows(a):
        a = a.reshape(m, a.shape[-1])
        return a if mp == m else jnp.pad(a, ((0, mp - m), (0, 0)))

    sh1, sc1, gt1, sh2, sc2, gt2 = jnp.split(mod, 6, axis=-1)
    if T > 1:
        gates3 = lambda g: (g[:, None, :], T)
    else:
        gates3 = lambda g: (rows(g[:, None, :])[None], None)
    hb = rows(_normmod(x, p['g_mix'], sc1, sh1))
    zs = {g: _proj(hb, p['w_' + g], name="proj_" + g) for g in ('nsa', 'gla', 'hgrn', 'mlstm')}
    gates = _proj(hb, p['w_gate'], out_dtype=jnp.bfloat16, act="sigmoid", name="proj_gate")
    obs, st = _token_mixers({g: z[:m] for g, z in zs.items()}, B, T, pos0, past, p, rel_bias, lb, tables)
    merged = _merge([rows(o) for o in obs], p['w_branch'], gates)
    g3, rpg = gates3(gt1)
    x1 = _mm_res(merged, p['w_out'], rows(x), g3, rows_per_gate=rpg, name="mm_out")
    hb2 = rows(_normmod(x1[:m].reshape(B, T, D_MODEL), p['g_ffn'], sc2, sh2))
    act = _mm_swiglu(hb2, p['w_ffn_in'], name="mm_ffn_in")
    g3, rpg = gates3(gt2)
    x2 = _mm_res(act, p['w_ffn_out'], x1, g3, rows_per_gate=rpg, bk=p['w_ffn_out'].shape[0] // 4, name="mm_ffn_out")
    return x2[:m].reshape(B, T, D_MODEL), st


def kernel(x_prompt, x_sample, cache_nsa_kv, state_nsa_win, state_gla, state_hgrn, state_mlstm_C,
           state_mlstm_n, state_mlstm_m, state_mlstm_conv, page_table, c_prompt, c_sample, rel_bias,
           w_ada, b_ada, g_mix, g_ffn, w_in, nsa_gq, nsa_gk, cmp_pe, cmp_w1, cmp_w2, gla_wa, gla_ba,
           gla_gn, hgrn_lb, hgrn_gn, m_wconv, m_bconv, m_bi, m_bf, m_gn, w_branch, w_out, w_ffn_in, w_ffn_out):
    lb_cum = jnp.cumsum(jax.nn.softmax(hgrn_lb.astype(jnp.float32), axis=0), axis=0)
    lb_all = lb_cum - lb_cum[:1]
    dec_b, n_pages = page_table.shape
    past_len = n_pages * PAGE_SIZE
    n_prompt = c_prompt.shape[0]
    ffp = _round_up(FF_DIM, 1024)
    x_p, x_s = x_prompt, x_sample
    st_prompt, st_sample = [], []
    c_all = jax.nn.silu(jnp.concatenate([c_prompt, c_sample], axis=0))
    tables = _bias_tables(rel_bias, x_prompt.shape[1])
    for l in range(DEPTH):
        w_ffn_in_l = jnp.concatenate(
            [_prep_w(w_ffn_in[l][:, :FF_DIM]), _prep_w(w_ffn_in[l][:, FF_DIM:])], axis=1)
        p = {'g_mix': g_mix[l], 'g_ffn': g_ffn[l],
             'nsa_gq': nsa_gq[l], 'nsa_gk': nsa_gk[l], 'cmp_pe': cmp_pe[l], 'cmp_w1': cmp_w1[l],
             'cmp_w2': cmp_w2[l], 'gla_wa': gla_wa[l], 'gla_ba': gla_ba[l], 'gla_gn': gla_gn[l],
             'hgrn_gn': hgrn_gn[l], 'm_wconv': m_wconv[l], 'm_bconv': m_bconv[l], 'm_bi': m_bi[l],
             'm_bf': m_bf[l], 'm_gn': m_gn[l],
             'w_branch': w_branch[l].astype(jnp.bfloat16),
             'w_out': _prep_w(w_out[l]),
             'w_ffn_in': w_ffn_in_l, 'w_ffn_out': _prep_w(w_ffn_out[l], k_mult=1024)}
        p.update({'w_' + g: w for g, w in _group_weights(w_in[l]).items()})
        assert p['w_ffn_out'].shape[0] == ffp
        mod = _matmul(c_all, _prep_w(w_ada[l]), "mm_ada") + b_ada[l]
        x_p, sp = _layer(x_p, mod[:n_prompt], 0, None, p, rel_bias, lb_all[l], tables)
        past = {'kv': cache_nsa_kv[l][page_table].reshape(dec_b, past_len, 4, NSA_KV_HEADS, HEAD_DIM),
                'win': state_nsa_win[l], 'gla': state_gla[l], 'hgrn': state_hgrn[l],
                'mC': state_mlstm_C[l], 'mn': state_mlstm_n[l], 'mm': state_mlstm_m[l],
                'conv': state_mlstm_conv[l]}
        x_s, ss = _layer(x_s, mod[n_prompt:], past_len, past, p, rel_bias, lb_all[l])
        st_prompt.append(sp)
        st_sample.append(ss)

    def stk(sts, i):
        return jnp.stack([s[i] for s in sts], axis=0)

    outs = [x_p, x_s]
    for i in range(8):
        outs.append(stk(st_prompt, i))
        outs.append(stk(st_sample, i))
    return tuple(outs)
```

```python
import functools
import math

import jax
import jax.numpy as jnp
import numpy as np
from jax import lax
from jax.experimental import pallas as pl
from jax.experimental.pallas import tpu as pltpu

D_MODEL = 4096
DEPTH = 2
PAGE_SIZE = 128
HEAD_DIM = 128
N_BRANCH = 4
BRANCH_WIDTH = D_MODEL // N_BRANCH
NSA_HEADS = BRANCH_WIDTH // HEAD_DIM
NSA_KV_HEADS = 2
NSA_HPG = NSA_HEADS // NSA_KV_HEADS
CMP_BLOCK = 32
CMP_STRIDE = 16
SEL_BLOCK = 64
N_SELECT = 16
WINDOW = 512
Q_BLOCK = 64
BAND = 128
REL_BUCKETS = 32
REL_MAX_DIST = 128
GLA_HEADS = 4
GLA_DK = BRANCH_WIDTH // (2 * GLA_HEADS)
GLA_DV = BRANCH_WIDTH // GLA_HEADS
GLA_GATE_RANK = 16
GLA_TAU = 16.0
HGRN_EXPAND = 128
HGRN_HEADS = BRANCH_WIDTH // HGRN_EXPAND
MLSTM_HEADS = 4
MLSTM_DQK = BRANCH_WIDTH // (2 * MLSTM_HEADS)
MLSTM_DV = BRANCH_WIDTH // MLSTM_HEADS
MLSTM_CONV = 4
CHUNK = 64
FF_DIM = ((8 * D_MODEL + 3 * 256 - 1) // (3 * 256)) * 256
EPS = 1e-6
NEG = -1e30
BIG = 1e6
IN_SIZES = (
    NSA_HEADS * HEAD_DIM, 6 * NSA_KV_HEADS * HEAD_DIM, 3 * NSA_HEADS,
    GLA_HEADS * GLA_DK, GLA_HEADS * GLA_DK, GLA_HEADS * GLA_DV, GLA_GATE_RANK, BRANCH_WIDTH,
    BRANCH_WIDTH, BRANCH_WIDTH, BRANCH_WIDTH, BRANCH_WIDTH,
    2 * MLSTM_HEADS * MLSTM_DQK, MLSTM_HEADS * MLSTM_DV, MLSTM_HEADS, MLSTM_HEADS, BRANCH_WIDTH,
    N_BRANCH * D_MODEL,
)
D_IN = sum(IN_SIZES)

V7X_VMEM_LIMIT_BYTES = 56 * 1024 * 1024
LANE = 128
SUB = 8


def _round_up(n, m):
    return (n + m - 1) // m * m


def _mm_kernel(a_ref, w_ref, o_ref):
    o_ref[...] = jnp.dot(a_ref[...], w_ref[...], preferred_element_type=jnp.float32).astype(o_ref.dtype)


def _mm_k_kernel(a_ref, w_ref, o_ref, acc_ref):
    k = pl.program_id(2)

    @pl.when(k == 0)
    def _():
        acc_ref[...] = jnp.zeros_like(acc_ref)

    acc_ref[...] += jnp.dot(a_ref[...], w_ref[...], preferred_element_type=jnp.float32)

    @pl.when(k == pl.num_programs(2) - 1)
    def _():
        o_ref[...] = acc_ref[...].astype(o_ref.dtype)


def _mm(a, w, *, bm, bn, bk=None, out_dtype=jnp.float32, name="mm"):
    m, kd = a.shape
    n = w.shape[1]
    assert w.shape[0] == kd and m % bm == 0 and n % bn == 0, (a.shape, w.shape, bm, bn)
    params = dict(vmem_limit_bytes=V7X_VMEM_LIMIT_BYTES)
    if bk is None or bk == kd:
        return pl.pallas_call(
            _mm_kernel,
            grid=(n // bn, m // bm),
            in_specs=[pl.BlockSpec((bm, kd), lambda j, i: (i, 0)),
                      pl.BlockSpec((kd, bn), lambda j, i: (0, j))],
            out_specs=pl.BlockSpec((bm, bn), lambda j, i: (i, j)),
            out_shape=jax.ShapeDtypeStruct((m, n), out_dtype),
            compiler_params=pltpu.CompilerParams(dimension_semantics=("arbitrary", "arbitrary"), **params),
            name=name,
        )(a, w)
    assert kd % bk == 0
    return pl.pallas_call(
        _mm_k_kernel,
        grid=(n // bn, m // bm, kd // bk),
        in_specs=[pl.BlockSpec((bm, bk), lambda j, i, k: (i, k)),
                  pl.BlockSpec((bk, bn), lambda j, i, k: (k, j))],
        out_specs=pl.BlockSpec((bm, bn), lambda j, i, k: (i, j)),
        out_shape=jax.ShapeDtypeStruct((m, n), out_dtype),
        scratch_shapes=[pltpu.VMEM((bm, bn), jnp.float32)],
        compiler_params=pltpu.CompilerParams(
            dimension_semantics=("arbitrary", "arbitrary", "arbitrary"), **params),
        name=name,
    )(a, w)


def _matmul(a, w_bf16, name):
    m, kd = a.shape
    a = a.astype(jnp.bfloat16)
    mp = _round_up(m, 16)
    if mp != m:
        a = jnp.pad(a, ((0, mp - m), (0, 0)))
    bm, bn = _tiles(mp, w_bf16.shape[1])
    out = _mm(a, w_bf16, bm=bm, bn=bn, name=name)
    return out[:m] if mp != m else out


def _tiles(m, n):
    if m >= 1024:
        return 1024, 512
    return m, (1024 if n % 1024 == 0 else 512)


def _cparams(n_axes):
    return pltpu.CompilerParams(dimension_semantics=("arbitrary",) * n_axes,
                                vmem_limit_bytes=V7X_VMEM_LIMIT_BYTES)


def _proj_kernel(a_ref, w_ref, o_ref, *, act):
    y = jnp.dot(a_ref[...], w_ref[...], preferred_element_type=jnp.float32)
    if act == "sigmoid":
        y = jax.nn.sigmoid(y)
    o_ref[...] = y.astype(o_ref.dtype)


def _proj(a, w, *, out_dtype=jnp.float32, act=None, name="proj"):
    m, kd = a.shape
    n = w.shape[1]
    bm, bn = _tiles(m, n)
    return pl.pallas_call(
        functools.partial(_proj_kernel, act=act),
        grid=(n // bn, m // bm),
        in_specs=[pl.BlockSpec((bm, kd), lambda j, i: (i, 0)),
                  pl.BlockSpec((kd, bn), lambda j, i: (0, j))],
        out_specs=pl.BlockSpec((bm, bn), lambda j, i: (i, j)),
        out_shape=jax.ShapeDtypeStruct((m, n), out_dtype),
        compiler_params=_cparams(2),
        name=name,
    )(a, w)


def _mm_res_kernel(a_ref, w_ref, x_ref, gt_ref, o_ref, acc_ref):
    k = pl.program_id(2)

    @pl.when(k == 0)
    def _():
        acc_ref[...] = jnp.zeros_like(acc_ref)

    acc_ref[...] += jnp.dot(a_ref[...], w_ref[...], preferred_element_type=jnp.float32)

    @pl.when(k == pl.num_programs(2) - 1)
    def _():
        o_ref[...] = x_ref[...] + gt_ref[0] * acc_ref[...]


def _mm_res(a, w, x, gt3, *, rows_per_gate, bk=None, name="mm_res"):
    m, kd = a.shape
    n = w.shape[1]
    bm, bn = _tiles(m, n)
    bk = kd if bk is None else bk
    if rows_per_gate is None:
        gt_spec = pl.BlockSpec((1, bm, bn), lambda j, i, k: (0, i, j))
    else:
        tiles_per_gate = rows_per_gate // bm
        gt_spec = pl.BlockSpec((1, 1, bn), lambda j, i, k: (i // tiles_per_gate, 0, j))
    return pl.pallas_call(
        _mm_res_kernel,
        grid=(n // bn, m // bm, kd // bk),
        in_specs=[pl.BlockSpec((bm, bk), lambda j, i, k: (i, k)),
                  pl.BlockSpec((bk, bn), lambda j, i, k: (k, j)),
                  pl.BlockSpec((bm, bn), lambda j, i, k: (i, j)),
                  gt_spec],
        out_specs=pl.BlockSpec((bm, bn), lambda j, i, k: (i, j)),
        out_shape=jax.ShapeDtypeStruct((m, n), jnp.float32),
        scratch_shapes=[pltpu.VMEM((bm, bn), jnp.float32)],
        compiler_params=_cparams(3),
        name=name,
    )(a, w, x, gt3)


def _mm_swiglu_kernel(a_ref, wg_ref, wu_ref, o_ref):
    a = a_ref[...]
    g = jnp.dot(a, wg_ref[...], preferred_element_type=jnp.float32)
    u = jnp.dot(a, wu_ref[...], preferred_element_type=jnp.float32)
    o_ref[...] = (g * jax.nn.sigmoid(g) * u).astype(o_ref.dtype)


def _mm_swiglu(a, w_gu, name="mm_swiglu"):
    m, kd = a.shape
    f = w_gu.shape[1] // 2
    bm, bn = _tiles(m, f)
    nb = f // bn
    return pl.pallas_call(
        _mm_swiglu_kernel,
        grid=(nb, m // bm),
        in_specs=[pl.BlockSpec((bm, kd), lambda j, i: (i, 0)),
                  pl.BlockSpec((kd, bn), lambda j, i: (0, j)),
                  pl.BlockSpec((kd, bn), lambda j, i: (0, j + nb))],
        out_specs=pl.BlockSpec((bm, bn), lambda j, i: (i, j)),
        out_shape=jax.ShapeDtypeStruct((m, f), jnp.bfloat16),
        compiler_params=_cparams(2),
        name=name,
    )(a, w_gu, w_gu)


def _merge_kernel(*refs):
    obs, wb_ref, gates, o_ref = refs[:N_BRANCH], refs[N_BRANCH], refs[N_BRANCH + 1:2 * N_BRANCH + 1], refs[-1]
    acc = None
    for br in range(N_BRANCH):
        y = gates[br][...].astype(jnp.float32) * jnp.dot(obs[br][...], wb_ref[br],
                                                         preferred_element_type=jnp.float32)
        acc = y if acc is None else acc + y
    o_ref[...] = acc.astype(o_ref.dtype)


def _merge(obs, wb, gates, name="merge"):
    m, wd = obs[0].shape
    n = wb.shape[2]
    bm, bn = _tiles(m, n)
    nb = n // bn
    gate_specs = [pl.BlockSpec((bm, bn), functools.partial(lambda j, i, br: (i, br * nb + j), br=br))
                  for br in range(N_BRANCH)]
    return pl.pallas_call(
        _merge_kernel,
        grid=(nb, m // bm),
        in_specs=([pl.BlockSpec((bm, wd), lambda j, i: (i, 0))] * N_BRANCH
                  + [pl.BlockSpec((N_BRANCH, wd, bn), lambda j, i: (0, 0, j))] + gate_specs),
        out_specs=pl.BlockSpec((bm, bn), lambda j, i: (i, j)),
        out_shape=jax.ShapeDtypeStruct((m, n), jnp.bfloat16),
        compiler_params=_cparams(2),
        name=name,
    )(*obs, wb, *([gates] * N_BRANCH))


def _normmod_kernel(x_ref, g_ref, sc_ref, sh_ref, o_ref):
    x = x_ref[0]
    y = x * lax.rsqrt(jnp.mean(x * x, axis=-1, keepdims=True) + EPS) * g_ref[...]
    o_ref[0] = (y * (1.0 + sc_ref[0]) + sh_ref[0]).astype(o_ref.dtype)


def _normmod(x, g, sc, sh):
    B, T, D = x.shape
    tt = min(T, 256)
    row = pl.BlockSpec((1, 1, D), lambda b, t: (b, 0, 0))
    return pl.pallas_call(
        _normmod_kernel,
        grid=(B, T // tt),
        in_specs=[pl.BlockSpec((1, tt, D), lambda b, t: (b, t, 0)),
                  pl.BlockSpec((1, D), lambda b, t: (0, 0)), row, row],
        out_specs=pl.BlockSpec((1, tt, D), lambda b, t: (b, t, 0)),
        out_shape=jax.ShapeDtypeStruct((B, T, D), jnp.bfloat16),
        compiler_params=_cparams(2),
        name="normmod",
    )(x, g[None], sc[:, None], sh[:, None])


def _prep_w(w, n_mult=1024, k_mult=None):
    kd, n = w.shape
    np_ = _round_up(n, n_mult)
    kp = kd if k_mult is None else _round_up(kd, k_mult)
    w = w.astype(jnp.bfloat16)
    if np_ != n or kp != kd:
        w = jnp.pad(w, ((0, kp - kd), (0, np_ - n)))
    return w


NSA_TQ = 128
NSA_KB = 128
SEL_PER_KB = NSA_KB // SEL_BLOCK
WIN_TILES = WINDOW // NSA_KB + 1


def _split3_bf16(x):
    hi = x.astype(jnp.bfloat16)
    r1 = x - hi.astype(jnp.float32)
    mid = r1.astype(jnp.bfloat16)
    lo = (r1 - mid.astype(jnp.float32)).astype(jnp.bfloat16)
    return hi, mid, lo


def _nsa_prompt_kernel(qT_ref, ksel_ref, vselT_ref, kwin_ref, vwinT_ref, kc_ref, vcT_ref, covT_ref,
                       bcmp_ref, bsel_ref, bwin_ref, gate_ref, o_ref,
                       mt_ref, m_ref, l_ref, acc_ref, *, n_top):
    hpg = qT_ref.shape[2]
    tq = qT_ref.shape[4]
    ns = covT_ref.shape[0]
    qb = pl.program_id(2)
    t0 = qb * tq
    gates = jax.nn.sigmoid(gate_ref[0, 0])

    kc = kc_ref[0, 0]
    vcT = vcT_ref[0, 0]
    p_sum = jnp.zeros((kc.shape[0], tq), jnp.float32)
    for h in range(hpg):
        bias = bcmp_ref[0, h]
        s = jnp.dot(kc, qT_ref[0, 0, h], preferred_element_type=jnp.float32) + bias
        e = jnp.exp(s - jnp.max(s, axis=0, keepdims=True))
        p = jnp.where(bias > 0.5 * NEG, e / jnp.sum(e, axis=0, keepdims=True), 0.0)
        p_sum = p_sum + p
        o_c = jnp.dot(vcT, p.astype(jnp.bfloat16), preferred_element_type=jnp.float32)
        o_ref[0, 0, h] = gates[0, h:h + 1, :] * o_c
    cov = covT_ref[...]
    imp = sum(jnp.dot(cov, part, preferred_element_type=jnp.float32) for part in _split3_bf16(p_sum))

    blk = lax.broadcasted_iota(jnp.int32, (ns, tq), 0)
    cur = (t0 + lax.broadcasted_iota(jnp.int32, (ns, tq), 1)) // SEL_BLOCK
    forced = (blk == 0) | (blk == cur) | (blk == cur - 1)
    score = jnp.where(forced, BIG, jnp.where(blk <= cur, imp, -BIG))
    rank = jnp.zeros((ns, tq), jnp.int32)
    for jp in range(ns):
        row = score[jp:jp + 1, :]
        beats = (row > score) | ((row == score) & (blk > jp))
        rank = rank + beats.astype(jnp.int32)
    mt_ref[...] = (rank < n_top).astype(jnp.float32)

    sub = lax.broadcasted_iota(jnp.int32, (NSA_KB, tq), 0)

    def attend(k_ref, vT_ref, bias_ref, n_tiles, lo, use_sel):
        m_ref[...] = jnp.full(m_ref.shape, NEG, jnp.float32)
        l_ref[...] = jnp.zeros(l_ref.shape, jnp.float32)
        acc_ref[...] = jnp.zeros(acc_ref.shape, jnp.float32)

        def body(kb, carry):
            koff = pl.multiple_of(kb * NSA_KB, NSA_KB)
            k_blk = k_ref[0, 0, pl.ds(koff, NSA_KB), :]
            vT_blk = vT_ref[0, 0, :, pl.ds(koff, NSA_KB)]
            tile = jnp.minimum(qb - kb, n_tiles - 1)
            if use_sel:
                r0 = mt_ref[pl.ds(kb * SEL_PER_KB, 1), :]
                r1 = mt_ref[pl.ds(kb * SEL_PER_KB + 1, 1), :]
                selm = jnp.where(sub < SEL_BLOCK, r0, r1) > 0.5
            for h in range(hpg):
                s = jnp.dot(k_blk, qT_ref[0, 0, h], preferred_element_type=jnp.float32) + bias_ref[0, h, tile]
                if use_sel:
                    s = jnp.where(selm, s, NEG)
                m_old = m_ref[h]
                m_new = jnp.maximum(m_old, jnp.max(s, axis=0, keepdims=True))
                alpha = jnp.exp(m_old - m_new)
                p = jnp.exp(s - m_new)
                l_ref[h] = alpha * l_ref[h] + jnp.sum(p, axis=0, keepdims=True)
                acc_ref[h] = alpha * acc_ref[h] + jnp.dot(vT_blk, p.astype(jnp.bfloat16),
                                                          preferred_element_type=jnp.float32)
                m_ref[h] = m_new
            return carry

        lax.fori_loop(lo, qb + 1, body, 0)

    attend(ksel_ref, vselT_ref, bsel_ref, bsel_ref.shape[2], 0, True)
    for h in range(hpg):
        o_ref[0, 0, h] += gates[1, h:h + 1, :] * (acc_ref[h] / l_ref[h])

    attend(kwin_ref, vwinT_ref, bwin_ref, bwin_ref.shape[2], jnp.maximum(qb - (WIN_TILES - 1), 0), False)
    for h in range(hpg):
        o_ref[0, 0, h] += gates[2, h:h + 1, :] * (acc_ref[h] / l_ref[h])


def _bias_tables(rel_bias, T):
    G, HPG = NSA_KV_HEADS, NSA_HPG
    nc = (T - CMP_BLOCK) // CMP_STRIDE + 1
    ncp = _round_up(nc, LANE)

    def lookup(dist, valid):
        b = rel_bias[_t5_bucket(dist)]
        b = jnp.where(valid[..., None], b, NEG)
        return jnp.moveaxis(b, -1, 0).reshape((G, HPG) + dist.shape)

    c = jnp.arange(NSA_KB)[:, None]
    i = jnp.arange(NSA_TQ)[None, :]
    sel_d = jnp.stack([dlt + i - c for dlt in (0, NSA_KB, 2 * NSA_KB)])
    bsel = lookup(sel_d, sel_d >= 0)
    win_d = jnp.stack([dlt * NSA_KB + i - c for dlt in range(WIN_TILES)])
    bwin = lookup(win_d, (win_d >= 0) & (win_d <= WINDOW))
    n = jnp.arange(ncp)[:, None]
    t = jnp.arange(T)[None, :]
    cmp_d = t - (n * CMP_STRIDE + CMP_BLOCK - 1)
    bcmp = lookup(cmp_d, (cmp_d >= 0) & (n < nc))
    ns = -(-T // SEL_BLOCK)
    covT = _block_cover(nc, ns).T
    covT = jnp.pad(covT, ((0, 0), (0, ncp - nc))).astype(jnp.bfloat16)
    return bsel, bwin, bcmp, covT


def _nsa_prompt(q, k_sel, v_sel, k_win, v_win, kc, vc, ngt, tables):
    B, T, G, HPG, HD = q.shape
    bsel, bwin, bcmp, covT = tables
    ns, ncp = covT.shape
    nc = kc.shape[1]
    bf = jnp.bfloat16
    qT = q.astype(bf).transpose(0, 2, 3, 4, 1)
    ksel = k_sel.astype(bf).transpose(0, 2, 1, 3)
    vselT = v_sel.astype(bf).transpose(0, 2, 3, 1)
    kwin = k_win.astype(bf).transpose(0, 2, 1, 3)
    vwinT = v_win.astype(bf).transpose(0, 2, 3, 1)
    kcp = jnp.pad(kc.astype(bf).transpose(0, 2, 1, 3), ((0, 0), (0, 0), (0, ncp - nc), (0, 0)))
    vcT = jnp.pad(vc.astype(bf).transpose(0, 2, 3, 1), ((0, 0), (0, 0), (0, 0), (0, ncp - nc)))
    gT = ngt.reshape(B, T, 3, G, HPG).transpose(0, 3, 2, 4, 1)
    tq = NSA_TQ
    full = lambda b, g, i: (b, g, 0, 0)
    oT = pl.pallas_call(
        functools.partial(_nsa_prompt_kernel, n_top=min(N_SELECT, ns)),
        grid=(B, G, T // tq),
        in_specs=[
            pl.BlockSpec((1, 1, HPG, HD, tq), lambda b, g, i: (b, g, 0, 0, i)),
            pl.BlockSpec((1, 1, T, HD), full),
            pl.BlockSpec((1, 1, HD, T), full),
            pl.BlockSpec((1, 1, T, HD), full),
            pl.BlockSpec((1, 1, HD, T), full),
            pl.BlockSpec((1, 1, ncp, HD), full),
            pl.BlockSpec((1, 1, HD, ncp), full),
            pl.BlockSpec((ns, ncp), lambda b, g, i: (0, 0)),
            pl.BlockSpec((1, HPG, ncp, tq), lambda b, g, i: (g, 0, 0, i)),
            pl.BlockSpec((1, HPG) + bsel.shape[2:], lambda b, g, i: (g, 0, 0, 0, 0)),
            pl.BlockSpec((1, HPG) + bwin.shape[2:], lambda b, g, i: (g, 0, 0, 0, 0)),
            pl.BlockSpec((1, 1, 3, HPG, tq), lambda b, g, i: (b, g, 0, 0, i)),
        ],
        out_specs=pl.BlockSpec((1, 1, HPG, HD, tq), lambda b, g, i: (b, g, 0, 0, i)),
        out_shape=jax.ShapeDtypeStruct((B, G, HPG, HD, T), jnp.float32),
        scratch_shapes=[pltpu.VMEM((ns, tq), jnp.float32),
                        pltpu.VMEM((HPG, 1, tq), jnp.float32),
                        pltpu.VMEM((HPG, 1, tq), jnp.float32),
                        pltpu.VMEM((HPG, HD, tq), jnp.float32)],
        compiler_params=pltpu.CompilerParams(
            dimension_semantics=("arbitrary", "arbitrary", "arbitrary"),
            vmem_limit_bytes=V7X_VMEM_LIMIT_BYTES),
        name="nsa_prompt",
    )(qT, ksel, vselT, kwin, vwinT, kcp, vcT, covT, bcmp, bsel, bwin, gT)
    return oT.transpose(0, 4, 1, 2, 3).reshape(B, T, G * HPG * HD)


LIN_TB = 256


def _logsigmoid(x):
    return jnp.minimum(x, 0.0) - jnp.log(1.0 + jnp.exp(-jnp.abs(x)))


def _lin_attn_kernel(*refs, mode, chunk):
    if mode == "gla":
        (q_ref, k_ref, v_ref, r_ref, a_ref, wa_ref, ba_ref, gn_ref, o_ref, st_ref,
         q_s, k_s, g_s, sT_ref) = refs
    else:
        (q_ref, k_ref, v_ref, r_ref, llb_ref, l1m_ref, gn_ref, o_ref, st_ref,
         q_s, k_s, g_s, sT_ref) = refs
    tb, dk = q_s.shape
    C = chunk
    t = pl.program_id(2)

    if mode == "gla":
        q_s[...] = q_ref[0] * (dk ** -0.5)
        k_s[...] = k_ref[0]
        pre = jnp.dot(a_ref[0].astype(jnp.bfloat16), wa_ref[...], preferred_element_type=jnp.float32) + ba_ref[...]
        g_s[...] = _logsigmoid(pre) / GLA_TAU
    else:
        x = q_ref[0]
        q_s[...] = x * jax.nn.sigmoid(x)
        u = llb_ref[...]
        w = l1m_ref[...] + _logsigmoid(k_ref[0])
        lf = jnp.maximum(u, w) + jnp.log(1.0 + jnp.exp(-jnp.abs(u - w)))
        g_s[...] = lf
        k_s[...] = 1.0 - jnp.exp(lf)

    @pl.when(t == 0)
    def _():
        sT_ref[...] = jnp.zeros(sT_ref.shape, jnp.float32)

    rr = lax.broadcasted_iota(jnp.int32, (C, C), 0)
    cc = lax.broadcasted_iota(jnp.int32, (C, C), 1)
    tril = (rr >= cc).astype(jnp.bfloat16)
    row8 = lax.broadcasted_iota(jnp.int32, (SUB, dk), 0)
    cc8 = lax.broadcasted_iota(jnp.int32, (SUB, C), 1)
    gn = gn_ref[...]
    bf = jnp.bfloat16

    def chunk_body(c, carry):
        r0 = pl.multiple_of(c * C, C)
        qc = q_s[pl.ds(r0, C), :]
        kc = k_s[pl.ds(r0, C), :]
        vc = v_ref[0, pl.ds(r0, C), :].astype(bf)
        b = sum(jnp.dot(tril, part, preferred_element_type=jnp.float32) for part in _split3_bf16(g_s[pl.ds(r0, C), :]))
        bl = b[C - 1:C, :]

        nt = C // SUB
        b_t = [b[i * SUB:(i + 1) * SUB] for i in range(nt)]
        q_t = [qc[i * SUB:(i + 1) * SUB] for i in range(nt)]
        att_t = [jnp.zeros((SUB, C), jnp.float32) for _ in range(nt)]
        for s in range(C):
            bs = b[s:s + 1, :]
            ks = kc[s:s + 1, :]
            for i in range(s // SUB, nt):
                d = b_t[i] - bs
                if i == s // SUB and s % SUB:
                    d = jnp.where(row8 >= s % SUB, d, NEG)
                col = jnp.sum(q_t[i] * ks * jnp.exp(d), axis=-1, keepdims=True)
                att_t[i] = jnp.where(cc8 == s, col, att_t[i])
        att = jnp.concatenate(att_t, axis=0)
        sT = sT_ref[...]
        o = (lax.dot_general((qc * jnp.exp(b)).astype(bf), sT.astype(bf), (((1,), (1,)), ((), ())),
                             preferred_element_type=jnp.float32)
             + jnp.dot(att.astype(bf), vc, preferred_element_type=jnp.float32))
        sT_ref[...] = sT * jnp.exp(bl) + lax.dot_general(
            vc, (kc * jnp.exp(bl - b)).astype(bf), (((0,), (0,)), ((), ())), preferred_element_type=jnp.float32)
        y = o * lax.rsqrt(jnp.mean(o * o, axis=-1, keepdims=True) + EPS) * gn
        r = r_ref[0, pl.ds(r0, C), :]
        gate = jax.nn.sigmoid(r)
        if mode == "gla":
            gate = r * gate
        o_ref[0, pl.ds(r0, C), :] = (y * gate).astype(o_ref.dtype)
        return carry

    lax.fori_loop(0, tb // C, chunk_body, 0)

    @pl.when(t == pl.num_programs(2) - 1)
    def _():
        st_ref[0, 0] = sT_ref[...]


def _lin_attn_prompt(z, B, T, mode, params):
    tb = LIN_TB
    z3 = z.reshape(B, T, z.shape[-1])
    if mode == "gla":
        H, dk, dv = GLA_HEADS, GLA_DK, GLA_DV
        wa, ba, gn = params
        in_specs = [
            pl.BlockSpec((1, tb, dk), lambda b, h, t: (b, t, h)),
            pl.BlockSpec((1, tb, dk), lambda b, h, t: (b, t, H + h)),
            pl.BlockSpec((1, tb, dv), lambda b, h, t: (b, t, (2 * H * dk) // dv + h)),
            pl.BlockSpec((1, tb, dv), lambda b, h, t: (b, t, (2 * H * dk + H * dv) // dv + h)),
            pl.BlockSpec((1, tb, LANE), lambda b, h, t: (b, t, (2 * H * dk + 2 * H * dv) // LANE)),
            pl.BlockSpec((LANE, dk), lambda b, h, t: (0, h)),
            pl.BlockSpec((1, dk), lambda b, h, t: (0, h)),
            pl.BlockSpec((1, dv), lambda b, h, t: (0, 0)),
        ]
        args = [z3, z3, z3, z3, z3, wa, ba, gn]
    else:
        H, dk, dv = HGRN_HEADS, HGRN_EXPAND, HGRN_EXPAND
        llb, l1m, gn = params
        in_specs = [
            pl.BlockSpec((1, tb, dk), lambda b, h, t: (b, t, h)),
            pl.BlockSpec((1, tb, dk), lambda b, h, t: (b, t, H + h)),
            pl.BlockSpec((1, tb, dv), lambda b, h, t: (b, t, 2 * H + h)),
            pl.BlockSpec((1, tb, dv), lambda b, h, t: (b, t, 3 * H + h)),
            pl.BlockSpec((1, dk), lambda b, h, t: (0, h)),
            pl.BlockSpec((1, dk), lambda b, h, t: (0, h)),
            pl.BlockSpec((1, dv), lambda b, h, t: (0, 0)),
        ]
        args = [z3, z3, z3, z3, llb, l1m, gn]
    o, sT = pl.pallas_call(
        functools.partial(_lin_attn_kernel, mode=mode, chunk=CHUNK),
        grid=(B, H, T // tb),
        in_specs=in_specs,
        out_specs=[pl.BlockSpec((1, tb, dv), lambda b, h, t: (b, t, h)),
                   pl.BlockSpec((1, 1, dv, dk), lambda b, h, t: (b, h, 0, 0))],
        out_shape=[jax.ShapeDtypeStruct((B, T, H * dv), jnp.bfloat16),
                   jax.ShapeDtypeStruct((B, H, dv, dk), jnp.float32)],
        scratch_shapes=[pltpu.VMEM((tb, dk), jnp.float32), pltpu.VMEM((tb, dk), jnp.float32),
                        pltpu.VMEM((tb, dk), jnp.float32), pltpu.VMEM((dv, dk), jnp.float32)],
        compiler_params=pltpu.CompilerParams(
            dimension_semantics=("arbitrary", "arbitrary", "arbitrary"),
            vmem_limit_bytes=V7X_VMEM_LIMIT_BYTES),
        name="lin_attn_" + mode,
    )(*args)
    return o.reshape(B * T, H * dv), sT.transpose(0, 1, 3, 2)


def _mlstm_kernel(xq_ref, xk_ref, v_ref, og_ref, gi_ref, wq_ref, wk_ref, bq_ref, bk_ref, bibf_ref, gn_ref,
                  o_ref, c_out, n_out, m_out, xe_q, xe_k, q_s, k_s, c_s, n_s, m_s, *, chunk):
    tb, dk = q_s.shape
    C = chunk
    t = pl.program_id(2)
    h = pl.program_id(1)
    W = MLSTM_CONV

    @pl.when(t == 0)
    def _():
        xe_q[0:SUB, :] = jnp.zeros((SUB, dk), jnp.float32)
        xe_k[0:SUB, :] = jnp.zeros((SUB, dk), jnp.float32)
        c_s[...] = jnp.zeros(c_s.shape, jnp.float32)
        n_s[...] = jnp.zeros(n_s.shape, jnp.float32)
        m_s[...] = jnp.zeros(m_s.shape, jnp.float32)

    for xe, x_ref, w_ref, b_ref, dst, scale in ((xe_q, xq_ref, wq_ref, bq_ref, q_s, 1.0),
                                                (xe_k, xk_ref, wk_ref, bk_ref, k_s, dk ** -0.5)):
        xe[SUB:SUB + tb, :] = x_ref[0]
        y = b_ref[...]
        for j in range(W):
            y = y + xe[SUB - (W - 1) + j:SUB - (W - 1) + j + tb, :] * w_ref[j:j + 1, :]
        dst[...] = (y * jax.nn.sigmoid(y)) * scale
        xe[SUB - (W - 1):SUB, :] = xe[SUB + tb - (W - 1):SUB + tb, :]

    lane = lax.broadcasted_iota(jnp.int32, (tb, LANE), 1)
    g = gi_ref[0] + bibf_ref[...]
    i_all = jnp.sum(jnp.where(lane == h, g, 0.0), axis=1, keepdims=True)
    f_all = _logsigmoid(jnp.sum(jnp.where(lane == MLSTM_HEADS + h, g, 0.0), axis=1, keepdims=True))
    rr = lax.broadcasted_iota(jnp.int32, (C, C), 0)
    cc = lax.broadcasted_iota(jnp.int32, (C, C), 1)
    causal = rr >= cc
    eye = rr == cc
    gn = gn_ref[...]
    bf = jnp.bfloat16

    def to_row(col):
        return jnp.sum(jnp.where(eye, col, 0.0), axis=0, keepdims=True)

    m = m_s[0:1, 0:1]
    for c in range(tb // C):
        r0 = c * C
        qc = q_s[r0:r0 + C, :]
        kc = k_s[r0:r0 + C, :]
        vc = v_ref[0, r0:r0 + C, :]
        i_col = i_all[r0:r0 + C]
        f_col = f_all[r0:r0 + C]
        f_row = to_row(f_col)
        F_col = jnp.sum(jnp.where(causal, f_row, 0.0), axis=1, keepdims=True)
        F_row = to_row(F_col)
        i_row = to_row(i_col)
        logw = jnp.where(causal, F_col - F_row + i_row, NEG)
        from_state = F_col + m
        m_hat = jnp.maximum(from_state, jnp.max(logw, axis=1, keepdims=True))
        qk = lax.dot_general(qc.astype(bf), kc.astype(bf), (((1,), (1,)), ((), ())),
                             preferred_element_type=jnp.float32)
        w = jnp.exp(logw - m_hat) * qk
        ws = jnp.exp(from_state - m_hat)
        cs = c_s[...]
        ns = n_s[0:1, :]
        num = (ws * lax.dot_general(qc.astype(bf), cs.astype(bf), (((1,), (1,)), ((), ())),
                                    preferred_element_type=jnp.float32)
               + jnp.dot(w.astype(bf), vc.astype(bf), preferred_element_type=jnp.float32))
        den = ws * jnp.sum(qc * ns, axis=1, keepdims=True) + jnp.sum(w, axis=1, keepdims=True)
        hh = num / jnp.maximum(jnp.abs(den), jnp.exp(-m_hat))
        m_new = m_hat[C - 1:C, :]
        F_last = F_col[C - 1:C, :]
        ds = jnp.exp(F_last - F_col + i_col - m_new)
        dst = jnp.exp(F_last + m - m_new)
        c_s[...] = dst * cs + lax.dot_general((vc * ds).astype(bf), kc.astype(bf), (((0,), (0,)), ((), ())),
                                              preferred_element_type=jnp.float32)
        n_s[...] = jnp.broadcast_to(dst * ns + jnp.sum(ds * kc, axis=0, keepdims=True), n_s.shape)
        m = m_new
        y = hh * lax.rsqrt(jnp.mean(hh * hh, axis=-1, keepdims=True) + EPS) * gn
        o_ref[0, r0:r0 + C, :] = (jax.nn.sigmoid(og_ref[0, r0:r0 + C, :]) * y).astype(o_ref.dtype)
    m_s[...] = jnp.broadcast_to(m, m_s.shape)

    @pl.when(t == pl.num_programs(2) - 1)
    def _():
        c_out[0, 0] = c_s[...]
        n_out[0, 0] = n_s[...]
        m_out[0, 0] = m_s[...]


def _mlstm_prompt(z, B, T, p):
    H, dk, dv = MLSTM_HEADS, MLSTM_DQK, MLSTM_DV
    tb = LIN_TB
    z3 = z.reshape(B, T, z.shape[-1])
    nqk = H * dk
    bibf = jnp.pad(jnp.concatenate([p['m_bi'], p['m_bf']]), (0, LANE - 2 * H))[None]
    wconv, bconv = p['m_wconv'], p['m_bconv'][None]
    o, c_f, n_f, m_f = pl.pallas_call(
        functools.partial(_mlstm_kernel, chunk=CHUNK),
        grid=(B, H, T // tb),
        in_specs=[
            pl.BlockSpec((1, tb, dk), lambda b, h, t: (b, t, h)),
            pl.BlockSpec((1, tb, dk), lambda b, h, t: (b, t, H + h)),
            pl.BlockSpec((1, tb, dv), lambda b, h, t: (b, t, (2 * nqk) // dv + h)),
            pl.BlockSpec((1, tb, dv), lambda b, h, t: (b, t, (2 * nqk + H * dv) // dv + h)),
            pl.BlockSpec((1, tb, LANE), lambda b, h, t: (b, t, (2 * nqk + 2 * H * dv) // LANE)),
            pl.BlockSpec((MLSTM_CONV, dk), lambda b, h, t: (0, h)),
            pl.BlockSpec((MLSTM_CONV, dk), lambda b, h, t: (0, H + h)),
            pl.BlockSpec((1, dk), lambda b, h, t: (0, h)),
            pl.BlockSpec((1, dk), lambda b, h, t: (0, H + h)),
            pl.BlockSpec((1, LANE), lambda b, h, t: (0, 0)),
            pl.BlockSpec((1, dv), lambda b, h, t: (0, 0)),
        ],
        out_specs=[pl.BlockSpec((1, tb, dv), lambda b, h, t: (b, t, h)),
                   pl.BlockSpec((1, 1, dv, dk), lambda b, h, t: (b, h, 0, 0)),
                   pl.BlockSpec((1, 1, SUB, dk), lambda b, h, t: (b, h, 0, 0)),
                   pl.BlockSpec((1, 1, SUB, LANE), lambda b, h, t: (b, h, 0, 0))],
        out_shape=[jax.ShapeDtypeStruct((B, T, H * dv), jnp.bfloat16),
                   jax.ShapeDtypeStruct((B, H, dv, dk), jnp.float32),
                   jax.ShapeDtypeStruct((B, H, SUB, dk), jnp.float32),
                   jax.ShapeDtypeStruct((B, H, SUB, LANE), jnp.float32)],
        scratch_shapes=[pltpu.VMEM((SUB + tb, dk), jnp.float32), pltpu.VMEM((SUB + tb, dk), jnp.float32),
                        pltpu.VMEM((tb, dk), jnp.float32), pltpu.VMEM((tb, dk), jnp.float32),
                        pltpu.VMEM((dv, dk), jnp.float32), pltpu.VMEM((SUB, dk), jnp.float32),
                        pltpu.VMEM((SUB, LANE), jnp.float32)],
        compiler_params=_cparams(3),
        name="mlstm_prompt",
    )(z3, z3, z3, z3, z3, wconv, wconv, bconv, bconv, bibf, p['m_gn'][None])
    return o.reshape(B * T, H * dv), c_f, n_f[:, :, 0, :], m_f[:, :, 0, 0]


CMP_PAGES = 16
CMP_PER_PAGE = PAGE_SIZE // CMP_STRIDE
HEAD_ROWS = SUB


def _cmp_sample_kernel(pt_ref, *refs, n_pages):
    del pt_ref
    pages = refs[:CMP_PAGES + 1]
    w1_ref, w2_ref, pe_ref, gk_ref, o_ref, xs_ref = refs[CMP_PAGES + 1:]
    for u in range(CMP_PAGES + 1):
        xs_ref[u * PAGE_SIZE:(u + 1) * PAGE_SIZE, :] = pages[u][0]
    nblk = CMP_PAGES * CMP_PER_PAGE
    acc = jnp.zeros((nblk, HEAD_DIM), jnp.float32)
    for j in range(0, CMP_BLOCK, 2):
        xa = xs_ref[pl.ds(j, nblk, stride=CMP_STRIDE), :] + pe_ref[0, j:j + 1, :]
        xb = xs_ref[pl.ds(j + 1, nblk, stride=CMP_STRIDE), :] + pe_ref[0, j + 1:j + 2, :]
        x2 = jnp.concatenate([xa, xb], axis=1).astype(jnp.bfloat16)
        w = w1_ref[0, j * HEAD_DIM:(j + 2) * HEAD_DIM, :]
        acc = acc + jnp.dot(x2, w, preferred_element_type=jnp.float32)
    y = jnp.dot((acc * jax.nn.sigmoid(acc)).astype(jnp.bfloat16), w2_ref[0], preferred_element_type=jnp.float32)
    yn = y * lax.rsqrt(jnp.mean(y * y, axis=-1, keepdims=True) + EPS) * gk_ref[...]
    is_key = pl.program_id(1) < NSA_KV_HEADS
    o_ref[0, 0] = jnp.where(is_key, yn, y)


def _cmp_sample(cache_rows, page_table, layer_off, w1, w2, pe, gk):
    B, n_pages = page_table.shape
    G = NSA_KV_HEADS
    assert n_pages % CMP_PAGES == 0
    nblk = CMP_PAGES * CMP_PER_PAGE

    def page_spec(u):
        def imap(b, c, grp, pt):
            page = jnp.minimum(grp * CMP_PAGES + u, n_pages - 1)
            return (layer_off + pt[b * n_pages + page], 0, c)
        return pl.BlockSpec((1, PAGE_SIZE, HEAD_DIM), imap)

    return pl.pallas_call(
        functools.partial(_cmp_sample_kernel, n_pages=n_pages),
        grid_spec=pltpu.PrefetchScalarGridSpec(
            num_scalar_prefetch=1,
            grid=(B, 2 * G, n_pages // CMP_PAGES),
            in_specs=[page_spec(u) for u in range(CMP_PAGES + 1)] + [
                pl.BlockSpec((1, CMP_BLOCK * HEAD_DIM, HEAD_DIM), lambda b, c, grp, pt: (c // G, 0, 0)),
                pl.BlockSpec((1, HEAD_DIM, HEAD_DIM), lambda b, c, grp, pt: (c // G, 0, 0)),
                pl.BlockSpec((1, CMP_BLOCK, HEAD_DIM), lambda b, c, grp, pt: (c // G, 0, 0)),
                pl.BlockSpec((1, HEAD_DIM), lambda b, c, grp, pt: (0, 0)),
            ],
            out_specs=pl.BlockSpec((1, 1, nblk, HEAD_DIM), lambda b, c, grp, pt: (b, c, grp, 0)),
            scratch_shapes=[pltpu.VMEM(((CMP_PAGES + 1) * PAGE_SIZE, HEAD_DIM), jnp.float32)],
        ),
        out_shape=jax.ShapeDtypeStruct((B, 2 * G, n_pages * CMP_PER_PAGE, HEAD_DIM), jnp.float32),
        compiler_params=_cparams(3),
        name="nsa_cmp_sample",
    )(page_table.reshape(-1), *([cache_rows] * (CMP_PAGES + 1)),
      w1.astype(jnp.bfloat16), w2.astype(jnp.bfloat16), pe, gk[None])


def _topk_sample_kernel(q_ref, kc_ref, vc_ref, bias_ref, cov_ref, ocmp_ref, top_ref, *, cur, n_top):
    nsp = cov_ref.shape[1]
    q = q_ref[0, 0]
    bias = bias_ref[0]
    s = lax.dot_general(q, kc_ref[0, 0].astype(jnp.bfloat16), (((1,), (1,)), ((), ())),
                        preferred_element_type=jnp.float32) + bias
    e = jnp.exp(s - jnp.max(s, axis=-1, keepdims=True))
    p = jnp.where(bias > 0.5 * NEG, e / jnp.sum(e, axis=-1, keepdims=True), 0.0)
    ocmp_ref[0, 0] = jnp.dot(p.astype(jnp.bfloat16), vc_ref[0, 0].astype(jnp.bfloat16),
                             preferred_element_type=jnp.float32)
    head = lax.broadcasted_iota(jnp.int32, p.shape, 0)
    p_sum = jnp.sum(jnp.where(head < NSA_HPG, p, 0.0), axis=0, keepdims=True)
    p_sum = jnp.broadcast_to(p_sum, (SUB, p_sum.shape[1]))
    cov = cov_ref[...]
    imp = sum(jnp.dot(part, cov, preferred_element_type=jnp.float32) for part in _split3_bf16(p_sum))[0:1]
    blk = lax.broadcasted_iota(jnp.int32, (1, nsp), 1)
    forced = (blk == 0) | (blk == cur) | (blk == cur - 1)
    score = jnp.where(forced, BIG, jnp.where(blk <= cur, imp, -BIG))
    score = jnp.where(blk <= cur, score, -2.0 * BIG)
    ii = lax.broadcasted_iota(jnp.int32, (nsp, nsp), 0)
    jj = lax.broadcasted_iota(jnp.int32, (nsp, nsp), 1)
    eye = ii == jj

    def to_col(row):
        return jnp.sum(jnp.where(eye, row, 0.0), axis=1, keepdims=True)

    col = to_col(score)
    beats = (col > score) | ((col == score) & (ii < jj))
    rank = jnp.sum(beats.astype(jnp.float32), axis=0, keepdims=True)
    sel = jnp.where((rank < n_top) & (blk < cur), 1.0, 0.0)
    pos = jnp.sum(jnp.where(ii < jj, to_col(sel), 0.0), axis=0, keepdims=True)
    lane = lax.broadcasted_iota(jnp.int32, (1, LANE), 1)
    out = jnp.zeros((1, LANE), jnp.float32)
    blk_f = blk.astype(jnp.float32)
    for k in range(n_top - 1):
        idx = jnp.sum(jnp.where((sel > 0.5) & (pos == k), blk_f, 0.0), axis=1, keepdims=True)
        out = jnp.where(lane == k, idx, out)
    top_ref[0, 0] = jnp.broadcast_to(out, (SUB, LANE)).astype(jnp.int32)


def _topk_sample(q8, kcvc, bias_c, cov, cur):
    B, G = q8.shape[:2]
    nc = kcvc.shape[2]
    nsp = cov.shape[1]
    return pl.pallas_call(
        functools.partial(_topk_sample_kernel, cur=cur, n_top=N_SELECT),
        grid=(B, G),
        in_specs=[pl.BlockSpec((1, 1, HEAD_ROWS, HEAD_DIM), lambda b, g: (b, g, 0, 0)),
                  pl.BlockSpec((1, 1, nc, HEAD_DIM), lambda b, g: (b, g, 0, 0)),
                  pl.BlockSpec((1, 1, nc, HEAD_DIM), lambda b, g: (b, G + g, 0, 0)),
                  pl.BlockSpec((1, HEAD_ROWS, nc), lambda b, g: (g, 0, 0)),
                  pl.BlockSpec((nc, nsp), lambda b, g: (0, 0))],
        out_specs=[pl.BlockSpec((1, 1, HEAD_ROWS, HEAD_DIM), lambda b, g: (b, g, 0, 0)),
                   pl.BlockSpec((1, 1, SUB, LANE), lambda b, g: (b, g, 0, 0))],
        out_shape=[jax.ShapeDtypeStruct((B, G, HEAD_ROWS, HEAD_DIM), jnp.float32),
                   jax.ShapeDtypeStruct((B, G, SUB, LANE), jnp.int32)],
        compiler_params=_cparams(2),
        name="nsa_topk_sample",
    )(q8, kcvc, kcvc, bias_c, cov)


def _sel_sample_kernel(pt_ref, top_ref, q_ref, ks_ref, vs_ref, new_ref, kw_ref, vw_ref, ocmp_ref,
                       bsel_ref, bwin_ref, b0_ref, gate_ref, o_ref, m_ref, l_ref, acc_ref, *, cur):
    del pt_ref
    b, g, k = pl.program_id(0), pl.program_id(1), pl.program_id(2)
    nk = pl.num_programs(2)
    q = q_ref[0, 0]
    qf = q.astype(jnp.float32)
    b0 = b0_ref[0]
    bf = jnp.bfloat16

    @pl.when(k == 0)
    def _():
        m_ref[...] = jnp.sum(qf * new_ref[0, 0, 0:1, :], axis=-1, keepdims=True) + b0
        l_ref[...] = jnp.ones(l_ref.shape, jnp.float32)
        acc_ref[...] = jnp.broadcast_to(new_ref[0, 0, 1:2, :], acc_ref.shape)

    j = top_ref[(b * NSA_KV_HEADS + g) * LANE + k]
    tile = jnp.minimum(cur - j, bsel_ref.shape[1]) - 1
    s = lax.dot_general(q, ks_ref[0].astype(bf), (((1,), (1,)), ((), ())),
                        preferred_element_type=jnp.float32) + bsel_ref[0, tile]
    m_old = m_ref[...]
    m_new = jnp.maximum(m_old, jnp.max(s, axis=-1, keepdims=True))
    alpha = jnp.exp(m_old - m_new)
    p = jnp.exp(s - m_new)
    l_ref[...] = alpha * l_ref[...] + jnp.sum(p, axis=-1, keepdims=True)
    acc_ref[...] = alpha * acc_ref[...] + jnp.dot(p.astype(bf), vs_ref[0].astype(bf),
                                                  preferred_element_type=jnp.float32)
    m_ref[...] = m_new

    @pl.when(k == nk - 1)
    def _():
        gates = jax.nn.sigmoid(gate_ref[0, 0])
        o_sel = acc_ref[...] / l_ref[...]
        sw = lax.dot_general(q, kw_ref[0].astype(bf), (((1,), (1,)), ((), ())),
                             preferred_element_type=jnp.float32) + bwin_ref[0]
        sn = jnp.sum(qf * new_ref[0, 0, 2:3, :], axis=-1, keepdims=True) + b0
        mw = jnp.maximum(jnp.max(sw, axis=-1, keepdims=True), sn)
        pw = jnp.exp(sw - mw)
        pn = jnp.exp(sn - mw)
        o_w = (jnp.dot(pw.astype(bf), vw_ref[0].astype(bf), preferred_element_type=jnp.float32)
               + pn * new_ref[0, 0, 3:4, :]) / (jnp.sum(pw, axis=-1, keepdims=True) + pn)
        o_ref[0, 0] = gates[0] * ocmp_ref[0, 0] + gates[1] * o_sel + gates[2] * o_w


def _sel_sample(cache_half, page_table, layer_off, top, q8, new_rows, win_rows, ocmp, bsel, bwin, b0, gate8, cur):
    B, n_pages = page_table.shape
    G = NSA_KV_HEADS
    per_page = PAGE_SIZE // SEL_BLOCK
    wb = win_rows.shape[1]

    def sel_spec(kind):
        def imap(b, g, k, pt, tp):
            j = tp[(b * G + g) * LANE + k]
            page = layer_off + pt[b * n_pages + j // per_page]
            return (page * per_page + j % per_page, 0, kind * G + g)
        return pl.BlockSpec((1, SEL_BLOCK, HEAD_DIM), imap)

    bg = lambda b, g, k, pt, tp: (b, g, 0, 0)
    return pl.pallas_call(
        functools.partial(_sel_sample_kernel, cur=cur),
        grid_spec=pltpu.PrefetchScalarGridSpec(
            num_scalar_prefetch=2,
            grid=(B, G, N_SELECT - 1),
            in_specs=[
                pl.BlockSpec((1, 1, HEAD_ROWS, HEAD_DIM), bg),
                sel_spec(2), sel_spec(3),
                pl.BlockSpec((1, 1, SUB, HEAD_DIM), bg),
                pl.BlockSpec((1, wb, HEAD_DIM), lambda b, g, k, pt, tp: (b, 0, g)),
                pl.BlockSpec((1, wb, HEAD_DIM), lambda b, g, k, pt, tp: (b, 0, G + g)),
                pl.BlockSpec((1, 1, HEAD_ROWS, HEAD_DIM), bg),
                pl.BlockSpec((1,) + bsel.shape[1:], lambda b, g, k, pt, tp: (g, 0, 0, 0)),
                pl.BlockSpec((1,) + bwin.shape[1:], lambda b, g, k, pt, tp: (g, 0, 0)),
                pl.BlockSpec((1,) + b0.shape[1:], lambda b, g, k, pt, tp: (g, 0, 0)),
                pl.BlockSpec((1, 1, 3, HEAD_ROWS, HEAD_DIM), lambda b, g, k, pt, tp: (b, g, 0, 0, 0)),
            ],
            out_specs=pl.BlockSpec((1, 1, HEAD_ROWS, HEAD_DIM), bg),
            scratch_shapes=[pltpu.VMEM((HEAD_ROWS, 1), jnp.float32), pltpu.VMEM((HEAD_ROWS, 1), jnp.float32),
                            pltpu.VMEM((HEAD_ROWS, HEAD_DIM), jnp.float32)],
        ),
        out_shape=jax.ShapeDtypeStruct((B, G, HEAD_ROWS, HEAD_DIM), jnp.float32),
        compiler_params=_cparams(3),
        name="nsa_sel_sample",
    )(page_table.reshape(-1), top.reshape(-1), q8, cache_half, cache_half, new_rows, win_rows, win_rows,
      ocmp, bsel, bwin, b0, gate8)


def _nsa_sample(z_nsa, cache_l, layer_off, page_table, win_l, p, rel_bias):
    B, n_pages = page_table.shape
    G, HPG, HD = NSA_KV_HEADS, NSA_HPG, HEAD_DIM
    past_len = n_pages * PAGE_SIZE
    cur = past_len // SEL_BLOCK
    nq = z_nsa[:, :G * HPG * HD].reshape(B, G, HPG, HD)
    kv = z_nsa[:, G * HPG * HD:G * HPG * HD + 6 * G * HD].reshape(B, 6, G, HD)
    ngt = z_nsa[:, G * HPG * HD + 6 * G * HD:][:, :3 * G * HPG].reshape(B, 3, G, HPG)
    q = _rmsnorm(nq, p['nsa_gq']) * (HD ** -0.5)
    rows_new = jnp.stack([kv[:, 0], kv[:, 1], _rmsnorm(kv[:, 2], p['nsa_gk'][1]), kv[:, 3]], axis=1)
    win_new = jnp.stack([_rmsnorm(kv[:, 4], p['nsa_gk'][2]), kv[:, 5]], axis=1)
    pad_heads = lambda a: jnp.pad(a, ((0, 0), (0, 0), (0, HEAD_ROWS - HPG), (0, 0)))
    q8 = pad_heads(q).astype(jnp.bfloat16)
    new_rows = jnp.stack([rows_new[:, 2], rows_new[:, 3], win_new[:, 0], win_new[:, 1]], axis=2)
    new_rows = jnp.pad(new_rows, ((0, 0), (0, 0), (0, SUB - 4), (0, 0)))
    gate8 = jnp.pad(ngt.transpose(0, 2, 1, 3)[..., None], ((0, 0), (0, 0), (0, 0), (0, HEAD_ROWS - HPG), (0, 0)))
    gate8 = jnp.broadcast_to(gate8, (B, G, 3, HEAD_ROWS, HD))

    def lookup(dist, valid):
        t = jnp.where(valid[..., None], rel_bias[_t5_bucket(dist)], NEG)
        t = jnp.moveaxis(t, -1, 0).reshape((G, HPG) + dist.shape)
        return jnp.pad(t, ((0, 0), (0, HEAD_ROWS - HPG)) + ((0, 0),) * dist.ndim)

    nc = n_pages * CMP_PER_PAGE
    n = jnp.arange(nc)
    bias_c = lookup(past_len - (n * CMP_STRIDE + CMP_BLOCK - 1), n < nc - 1)
    ns = cur + 1
    nsp = _round_up(ns, LANE)
    cov = jnp.pad(_block_cover(nc - 1, ns), ((0, 1), (0, nsp - ns))).astype(jnp.bfloat16)
    jj = jnp.arange(SEL_BLOCK)
    sel_d = jnp.stack([SEL_BLOCK * d - jj for d in (1, 2, 3)])
    bsel = lookup(sel_d, sel_d >= 0).transpose(0, 2, 1, 3)
    wb = win_l.shape[1]
    bwin = lookup(wb - jnp.arange(wb), jnp.ones((wb,), bool))
    b0 = lookup(jnp.zeros((1,), jnp.int32), jnp.ones((1,), bool))

    pool_rows = cache_l.shape[0]
    cache_rows = cache_l.reshape(pool_rows, PAGE_SIZE, 4 * G * HD)
    kcvc = _cmp_sample(cache_rows, page_table, layer_off, p['cmp_w1'], p['cmp_w2'], p['cmp_pe'], p['nsa_gk'][0])
    ocmp, top = _topk_sample(q8, kcvc, bias_c, cov, cur)
    cache_half = cache_l.reshape(pool_rows * (PAGE_SIZE // SEL_BLOCK), SEL_BLOCK, 4 * G * HD)
    win_rows = win_l.reshape(B, wb, 2 * G * HD)
    o = _sel_sample(cache_half, page_table, layer_off, top[:, :, 0, :], q8, new_rows, win_rows, ocmp,
                    bsel, bwin, b0, gate8, cur)
    o_nsa = o[:, :, :HPG, :].reshape(B, G * HPG * HD)
    win_state = jnp.concatenate([win_l[:, 1:], win_new[:, None]], axis=1)
    return o_nsa, rows_new[:, None], win_state


def _split(z, sizes):
    cuts = [int(c) for c in np.cumsum(sizes)[:-1]]
    return jnp.split(z, cuts, axis=-1)


def _rmsnorm(x, g):
    xf = x.astype(jnp.float32)
    y = xf * lax.rsqrt(jnp.mean(xf * xf, axis=-1, keepdims=True) + EPS)
    return (y * g.astype(jnp.float32)).astype(x.dtype)


def _masked_softmax(s, mask):
    s = jnp.where(mask, s.astype(jnp.float32), NEG)
    return jnp.where(mask, jax.nn.softmax(s, axis=-1), 0.0)


def _t5_bucket(dist):
    d = jnp.maximum(dist, 0)
    exact = REL_BUCKETS // 2
    far = exact + (jnp.log(jnp.maximum(d, 1).astype(jnp.float32) / exact)
                   / math.log(REL_MAX_DIST / exact) * (REL_BUCKETS - exact)).astype(jnp.int32)
    return jnp.where(d < exact, d, jnp.minimum(far, REL_BUCKETS - 1))


def _nsa_compress(k_rows, v_rows, pe, w1, w2, g_kc):
    B, L, G, _ = k_rows.shape
    nc = (L - CMP_BLOCK) // CMP_STRIDE + 1
    idx = jnp.arange(nc)[:, None] * CMP_STRIDE + jnp.arange(CMP_BLOCK)[None, :]

    def phi(rows, pe_, w1_, w2_):
        blk = rows[:, idx] + pe_[None, None, :, None, :]
        flat = blk.transpose(0, 1, 3, 2, 4).reshape(B, nc, G, CMP_BLOCK * HEAD_DIM)
        return jax.nn.silu(flat @ w1_) @ w2_

    kc = _rmsnorm(phi(k_rows, pe[0], w1[0], w2[0]), g_kc)
    vc = phi(v_rows, pe[1], w1[1], w2[1])
    c_end = jnp.arange(nc) * CMP_STRIDE + CMP_BLOCK - 1
    return kc, vc, c_end


def _block_cover(nc, ns):
    c0 = jnp.arange(nc)[:, None] * CMP_STRIDE
    s0 = jnp.arange(ns)[None, :] * SEL_BLOCK
    return ((c0 <= s0 + SEL_BLOCK - 1) & (c0 + CMP_BLOCK - 1 >= s0)).astype(jnp.float32)


def _nsa_cmp_sel(q, q_pos, kc, vc, c_end, ks_t, vs_t, rel_bias, cover):
    B, Tq, G, HPG, _ = q.shape
    L = ks_t.shape[2]
    ns = cover.shape[1]
    s_c = jnp.einsum('btghd,bngd->btghn', q, kc).astype(jnp.float32)
    dist_c = q_pos[:, None] - c_end[None, :]
    bias_c = rel_bias[_t5_bucket(dist_c)].reshape(Tq, -1, G, HPG).transpose(0, 2, 3, 1)
    p_c = _masked_softmax(s_c + bias_c, (dist_c >= 0)[:, None, None, :])
    o_cmp = jnp.einsum('btghn,bngd->btghd', p_c.astype(vc.dtype), vc)
    imp = jnp.einsum('btgn,ns->btgs', p_c.sum(axis=3), cover)
    blk = jnp.arange(ns)[None, :]
    cur = (q_pos // SEL_BLOCK)[:, None]
    forced = ((blk == 0) | (blk == cur) | (blk == cur - 1))[:, None, :]
    valid = (blk <= cur)[:, None, :]
    score = jnp.where(forced, BIG, jnp.where(valid, imp, -BIG))
    n_top = min(N_SELECT, ns)
    _, top = lax.top_k(score, n_top)
    tok = (top[..., None] * SEL_BLOCK + jnp.arange(SEL_BLOCK)).reshape(B, Tq, G, n_top * SEL_BLOCK)
    tok_c = jnp.minimum(tok, L - 1)
    bi = jnp.arange(B)[:, None, None, None]
    gi = jnp.arange(G)[None, None, :, None]
    kg = ks_t[bi, gi, tok_c]
    vg = vs_t[bi, gi, tok_c]
    s_s = jnp.einsum('btghd,btgsd->btghs', q, kg).astype(jnp.float32)
    dist_s = q_pos[None, :, None, None] - tok
    bias_s = rel_bias.reshape(REL_BUCKETS, G, HPG)[_t5_bucket(dist_s), gi]
    p_s = _masked_softmax(s_s + bias_s.transpose(0, 1, 2, 4, 3), (dist_s >= 0)[:, :, :, None, :])
    o_sel = jnp.einsum('btghs,btgsd->btghd', p_s.astype(vg.dtype), vg)
    return o_cmp, o_sel


def _window_attend(q, q_pos, k, v, k_pos, rel_bias):
    N, Qb = q_pos.shape
    Kb = k_pos.shape[1]
    G, HPG = q.shape[3], q.shape[4]
    s = jnp.einsum('bnqghd,bnkgd->bnqghk', q, k).astype(jnp.float32)
    dist = q_pos[:, :, None] - k_pos[:, None, :]
    bias = rel_bias[_t5_bucket(dist)].reshape(N, Qb, Kb, G, HPG).transpose(0, 1, 3, 4, 2)
    mask = ((dist >= 0) & (dist <= WINDOW) & (k_pos[:, None, :] >= 0))[:, :, None, None, :]
    p = _masked_softmax(s + bias, mask)
    return jnp.einsum('bnqghk,bnkgd->bnqghd', p.astype(v.dtype), v)


def _gated_linear_attn(q, k, v, log_a, s0):
    B, T, H, _ = q.shape
    dv = v.shape[-1]
    C = CHUNK if T % CHUNK == 0 else T
    n = T // C

    def chunks(a):
        return a.astype(jnp.float32).reshape(B, n, C, H, a.shape[-1]).transpose(1, 0, 3, 2, 4)

    causal = jnp.tril(jnp.ones((C, C), bool))

    def step(S, inp):
        qc, kc, vc, gc = inp
        b = jnp.cumsum(gc, axis=2)
        diff = jnp.where(causal[:, :, None], b[:, :, :, None, :] - b[:, :, None, :, :], -jnp.inf)
        att = jnp.einsum('bhtk,bhsk,bhtsk->bhts', qc, kc, jnp.exp(diff))
        o = (jnp.einsum('bhtk,bhkv->bhtv', qc * jnp.exp(b), S)
             + jnp.einsum('bhts,bhsv->bhtv', att, vc))
        b_last = b[:, :, -1:, :]
        S = (jnp.exp(b_last[:, :, 0, :])[..., None] * S
             + jnp.einsum('bhsk,bhsv->bhkv', kc * jnp.exp(b_last - b), vc))
        return S, o

    S, o = lax.scan(step, s0.astype(jnp.float32), (chunks(q), chunks(k), chunks(v), chunks(log_a)))
    o = o.transpose(1, 0, 3, 2, 4).reshape(B, T, H, dv)
    return o.astype(v.dtype), S.astype(s0.dtype)


def _mlstm(q, k, v, ig, lf, C0, n0, m0):
    B, T, H, _ = q.shape
    dv = v.shape[-1]
    C = CHUNK if T % CHUNK == 0 else T
    n = T // C

    def chunks(a):
        return a.astype(jnp.float32).reshape(B, n, C, H, a.shape[-1]).transpose(1, 0, 3, 2, 4)

    def gchunks(a):
        return a.astype(jnp.float32).reshape(B, n, C, H).transpose(1, 0, 3, 2)

    causal = jnp.tril(jnp.ones((C, C), bool))

    def step(carry, inp):
        Cs, ns_, m = carry
        qc, kc, vc, ic, fc = inp
        F = jnp.cumsum(fc, axis=-1)
        logw = jnp.where(causal, F[..., :, None] - F[..., None, :] + ic[..., None, :], -jnp.inf)
        from_state = F + m[..., None]
        m_hat = jnp.maximum(from_state, logw.max(-1))
        w = jnp.exp(logw - m_hat[..., None]) * jnp.einsum('bhtk,bhsk->bhts', qc, kc)
        ws = jnp.exp(from_state - m_hat)
        num = ws[..., None] * jnp.einsum('bhtk,bhvk->bhtv', qc, Cs) + jnp.einsum('bhts,bhsv->bhtv', w, vc)
        den = ws * jnp.einsum('bhtk,bhk->bht', qc, ns_) + w.sum(-1)
        h = num / jnp.maximum(jnp.abs(den), jnp.exp(-m_hat))[..., None]
        m_new = m_hat[..., -1]
        ds = jnp.exp(F[..., -1:] - F + ic - m_new[..., None])
        dst = jnp.exp(F[..., -1] + m - m_new)
        Cs = dst[..., None, None] * Cs + jnp.einsum('bhs,bhsv,bhsk->bhvk', ds, vc, kc)
        ns_ = dst[..., None] * ns_ + jnp.einsum('bhs,bhsk->bhk', ds, kc)
        return (Cs, ns_, m_new), h

    carry0 = (C0.astype(jnp.float32), n0.astype(jnp.float32), m0.astype(jnp.float32))
    (Cf, nf, mf), h = lax.scan(step, carry0, (chunks(q), chunks(k), chunks(v), gchunks(ig), gchunks(lf)))
    h = h.transpose(1, 0, 3, 2, 4).reshape(B, T, H, dv)
    return h.astype(v.dtype), Cf.astype(C0.dtype), nf.astype(n0.dtype), mf.astype(m0.dtype)


def _causal_conv(u, buf, w, b):
    T = u.shape[1]
    up = jnp.concatenate([buf, u], axis=1)
    y = b
    for j in range(MLSTM_CONV):
        y = y + up[:, j:j + T] * w[j]
    return jax.nn.silu(y), up[:, -(MLSTM_CONV - 1):]


GROUPS = {
    'nsa': (('nq', 'nkv', 'ngt'), 3072),
    'gla': (('gq', 'gk', 'gv', 'gr', 'ga'), 3584),
    'hgrn': (('hq', 'hf', 'hi', 'hg'), 4096),
    'mlstm': (('mqk', 'mv', 'mo', 'mi', 'mf'), 3584),
    'gate': (('mg',), N_BRANCH * D_MODEL),
}
IN_NAMES = ('nq', 'nkv', 'ngt', 'gq', 'gk', 'gv', 'ga', 'gr', 'hq', 'hf', 'hi', 'hg',
            'mqk', 'mv', 'mi', 'mf', 'mo', 'mg')
IN_WIDTH = dict(zip(IN_NAMES, IN_SIZES))


def _group_weights(w_in_l):
    cuts = dict(zip(IN_NAMES, np.cumsum((0,) + IN_SIZES[:-1])))
    out = {}
    for gname, (members, width) in GROUPS.items():
        w = jnp.concatenate([w_in_l[:, int(cuts[m]):int(cuts[m]) + IN_WIDTH[m]] for m in members], axis=1)
        w = w.astype(jnp.bfloat16)
        out[gname] = jnp.pad(w, ((0, 0), (0, width - w.shape[1])))
    return out


def _group_fields(zs, gname, B, T):
    out, off = {}, 0
    for m in GROUPS[gname][0]:
        out[m] = zs[gname][:, off:off + IN_WIDTH[m]].reshape(B, T, IN_WIDTH[m])
        off += IN_WIDTH[m]
    return out


def _token_mixers(zs, B, T, pos0, past, p, rel_bias, lb, tables):
    G, HPG, HD = NSA_KV_HEADS, NSA_HPG, HEAD_DIM
    f = _group_fields(zs, 'nsa', B, T)
    nq, nkv, ngt = f['nq'], f['nkv'], f['ngt']
    f = _group_fields(zs, 'mlstm', B, T)
    mqk, mv, mo, mi, mf = f['mqk'], f['mv'], f['mo'], f['mi'], f['mf']
    q_pos = pos0 + jnp.arange(T)

    if past is None:
        q = _rmsnorm(nq.reshape(B, T, G, HPG, HD), p['nsa_gq']) * (HD ** -0.5)
        kv = nkv.reshape(B, T, 6, G, HD)
        rows_new = jnp.stack([kv[:, :, 0], kv[:, :, 1], _rmsnorm(kv[:, :, 2], p['nsa_gk'][1]), kv[:, :, 3]], axis=2)
        win_new = jnp.stack([_rmsnorm(kv[:, :, 4], p['nsa_gk'][2]), kv[:, :, 5]], axis=2)
        kc, vc, _ = _nsa_compress(rows_new[:, :, 0], rows_new[:, :, 1], p['cmp_pe'], p['cmp_w1'], p['cmp_w2'],
                                  p['nsa_gk'][0])
        o_nsa = _nsa_prompt(q, rows_new[:, :, 2], rows_new[:, :, 3], win_new[:, :, 0], win_new[:, :, 1],
                            kc, vc, ngt, tables)
        win_state = win_new[:, -min(WINDOW, T):]
    else:
        o_nsa, rows_new, win_state = _nsa_sample(zs['nsa'], past['cache'], past['layer_off'], past['page_table'],
                                                 past['win'], p, rel_bias)

    if past is None:
        wa_p = jnp.pad(p['gla_wa'], ((0, LANE - GLA_GATE_RANK), (0, 0))).astype(jnp.bfloat16)
        o_gla, gla_state = _lin_attn_prompt(zs['gla'], B, T, "gla", (wa_p, p['gla_ba'][None], p['gla_gn'][None]))
        o_hgrn, hgrn_state = _lin_attn_prompt(zs['hgrn'], B, T, "hgrn",
                                              (jnp.log(lb)[None], jnp.log1p(-lb)[None], p['hgrn_gn'][None]))
    else:
        f = _group_fields(zs, 'gla', B, T)
        gq, gk, gv, gr, ga = f['gq'], f['gk'], f['gv'], f['gr'], f['ga']
        g_q = gq.reshape(B, T, GLA_HEADS, GLA_DK) * (GLA_DK ** -0.5)
        g_k = gk.reshape(B, T, GLA_HEADS, GLA_DK)
        g_v = gv.reshape(B, T, GLA_HEADS, GLA_DV)
        log_a = (jax.nn.log_sigmoid((ga @ p['gla_wa'] + p['gla_ba']).astype(jnp.float32)) / GLA_TAU).reshape(B, T, GLA_HEADS, GLA_DK)
        o, gla_state = _gated_linear_attn(g_q, g_k, g_v, log_a, past['gla'])
        o_gla = (_rmsnorm(o, p['gla_gn']) * jax.nn.silu(gr.reshape(B, T, GLA_HEADS, GLA_DV))).reshape(B * T, -1)

        f = _group_fields(zs, 'hgrn', B, T)
        hq, hf, hi, hg = f['hq'], f['hf'], f['hi'], f['hg']
        log_f = jnp.logaddexp(jnp.log(lb), jnp.log1p(-lb) + jax.nn.log_sigmoid(hf.astype(jnp.float32)))
        log_f = log_f.reshape(B, T, HGRN_HEADS, HGRN_EXPAND)
        h_q = jax.nn.silu(hq).reshape(B, T, HGRN_HEADS, HGRN_EXPAND)
        h_v = hi.reshape(B, T, HGRN_HEADS, HGRN_EXPAND)
        o, hgrn_state = _gated_linear_attn(h_q, -jnp.expm1(log_f), h_v, log_f, past['hgrn'])
        o_hgrn = (_rmsnorm(o, p['hgrn_gn']) * jax.nn.sigmoid(hg.reshape(B, T, HGRN_HEADS, HGRN_EXPAND))).reshape(B * T, -1)

    nqk = MLSTM_HEADS * MLSTM_DQK
    if past is None:
        o_mlstm, mC, mn, mm = _mlstm_prompt(zs['mlstm'], B, T, p)
        conv_state = mqk[:, -(MLSTM_CONV - 1):]
    else:
        u, conv_state = _causal_conv(mqk, past['conv'], p['m_wconv'], p['m_bconv'])
        m_q = u[..., :nqk].reshape(B, T, MLSTM_HEADS, MLSTM_DQK)
        m_k = u[..., nqk:].reshape(B, T, MLSTM_HEADS, MLSTM_DQK) * (MLSTM_DQK ** -0.5)
        m_v = mv.reshape(B, T, MLSTM_HEADS, MLSTM_DV)
        ig = (mi + p['m_bi']).astype(jnp.float32)
        lf = jax.nn.log_sigmoid((mf + p['m_bf']).astype(jnp.float32))
        hm, mC, mn, mm = _mlstm(m_q, m_k, m_v, ig, lf, past['mC'], past['mn'], past['mm'])
        o_mlstm = (jax.nn.sigmoid(mo).reshape(B, T, MLSTM_HEADS, MLSTM_DV) * _rmsnorm(hm, p['m_gn'])).reshape(B, T, -1)

    obs = [o.reshape(B * T, BRANCH_WIDTH).astype(jnp.bfloat16) for o in (o_nsa, o_gla, o_hgrn, o_mlstm)]
    return obs, (rows_new, win_state, gla_state, hgrn_state, mC, mn, mm, conv_state)


def _layer(x, mod, pos0, past, p, rel_bias, lb, tables=None):
    B, T, _ = x.shape
    m = B * T
    mp = _round_up(m, 16)

    def rows(a):
        a = a.reshape(m, a.shape[-1])
        return a if mp == m else jnp.pad(a, ((0, mp - m), (0, 0)))

    sh1, sc1, gt1, sh2, sc2, gt2 = jnp.split(mod, 6, axis=-1)
    if T > 1:
        gates3 = lambda g: (g[:, None, :], T)
    else:
        gates3 = lambda g: (rows(g[:, None, :])[None], None)
    hb = rows(_normmod(x, p['g_mix'], sc1, sh1))
    zs = {g: _proj(hb, p['w_' + g], name="proj_" + g) for g in ('nsa', 'gla', 'hgrn', 'mlstm')}
    gates = _proj(hb, p['w_gate'], out_dtype=jnp.bfloat16, act="sigmoid", name="proj_gate")
    obs, st = _token_mixers({g: z[:m] for g, z in zs.items()}, B, T, pos0, past, p, rel_bias, lb, tables)
    merged = _merge([rows(o) for o in obs], p['w_branch'], gates)
    g3, rpg = gates3(gt1)
    x1 = _mm_res(merged, p['w_out'], rows(x), g3, rows_per_gate=rpg, name="mm_out")
    hb2 = rows(_normmod(x1[:m].reshape(B, T, D_MODEL), p['g_ffn'], sc2, sh2))
    act = _mm_swiglu(hb2, p['w_ffn_in'], name="mm_ffn_in")
    g3, rpg = gates3(gt2)
    x2 = _mm_res(act, p['w_ffn_out'], x1, g3, rows_per_gate=rpg, bk=p['w_ffn_out'].shape[0] // 4, name="mm_ffn_out")
    return x2[:m].reshape(B, T, D_MODEL), st


def kernel(x_prompt, x_sample, cache_nsa_kv, state_nsa_win, state_gla, state_hgrn, state_mlstm_C,
           state_mlstm_n, state_mlstm_m, state_mlstm_conv, page_table, c_prompt, c_sample, rel_bias,
           w_ada, b_ada, g_mix, g_ffn, w_in, nsa_gq, nsa_gk, cmp_pe, cmp_w1, cmp_w2, gla_wa, gla_ba,
           gla_gn, hgrn_lb, hgrn_gn, m_wconv, m_bconv, m_bi, m_bf, m_gn, w_branch, w_out, w_ffn_in, w_ffn_out):
    lb_cum = jnp.cumsum(jax.nn.softmax(hgrn_lb.astype(jnp.float32), axis=0), axis=0)
    lb_all = lb_cum - lb_cum[:1]
    dec_b, n_pages = page_table.shape
    past_len = n_pages * PAGE_SIZE
    n_prompt = c_prompt.shape[0]
    ffp = _round_up(FF_DIM, 1024)
    x_p, x_s = x_prompt, x_sample
    st_prompt, st_sample = [], []
    c_all = jax.nn.silu(jnp.concatenate([c_prompt, c_sample], axis=0))
    tables = _bias_tables(rel_bias, x_prompt.shape[1])
    n_pool = cache_nsa_kv.shape[1]
    cache_all = cache_nsa_kv.reshape((DEPTH * n_pool,) + cache_nsa_kv.shape[2:])
    for l in range(DEPTH):
        w_ffn_in_l = jnp.concatenate(
            [_prep_w(w_ffn_in[l][:, :FF_DIM]), _prep_w(w_ffn_in[l][:, FF_DIM:])], axis=1)
        p = {'g_mix': g_mix[l], 'g_ffn': g_ffn[l],
             'nsa_gq': nsa_gq[l], 'nsa_gk': nsa_gk[l], 'cmp_pe': cmp_pe[l], 'cmp_w1': cmp_w1[l],
             'cmp_w2': cmp_w2[l], 'gla_wa': gla_wa[l], 'gla_ba': gla_ba[l], 'gla_gn': gla_gn[l],
             'hgrn_gn': hgrn_gn[l], 'm_wconv': m_wconv[l], 'm_bconv': m_bconv[l], 'm_bi': m_bi[l],
             'm_bf': m_bf[l], 'm_gn': m_gn[l],
             'w_branch': w_branch[l].astype(jnp.bfloat16),
             'w_out': _prep_w(w_out[l]),
             'w_ffn_in': w_ffn_in_l, 'w_ffn_out': _prep_w(w_ffn_out[l], k_mult=1024)}
        p.update({'w_' + g: w for g, w in _group_weights(w_in[l]).items()})
        assert p['w_ffn_out'].shape[0] == ffp
        mod = _matmul(c_all, _prep_w(w_ada[l]), "mm_ada") + b_ada[l]
        x_p, sp = _layer(x_p, mod[:n_prompt], 0, None, p, rel_bias, lb_all[l], tables)
        past = {'cache': cache_all, 'layer_off': l * n_pool, 'page_table': page_table,
                'win': state_nsa_win[l], 'gla': state_gla[l], 'hgrn': state_hgrn[l],
                'mC': state_mlstm_C[l], 'mn': state_mlstm_n[l], 'mm': state_mlstm_m[l],
                'conv': state_mlstm_conv[l]}
        x_s, ss = _layer(x_s, mod[n_prompt:], past_len, past, p, rel_bias, lb_all[l])
        st_prompt.append(sp)
        st_sample.append(ss)

    def stk(sts, i):
        return jnp.stack([s[i] for s in sts], axis=0)

    outs = [x_p, x_s]
    for i in range(8):
        outs.append(stk(st_prompt, i))
        outs.append(stk(st_sample, i))
    return tuple(outs)
```

```python
import functools
import math

import jax
import jax.numpy as jnp
import numpy as np
from jax import lax
from jax.experimental import pallas as pl
from jax.experimental.pallas import tpu as pltpu

D_MODEL = 4096
DEPTH = 2
PAGE_SIZE = 128
HEAD_DIM = 128
N_BRANCH = 4
BRANCH_WIDTH = D_MODEL // N_BRANCH
NSA_HEADS = BRANCH_WIDTH // HEAD_DIM
NSA_KV_HEADS = 2
NSA_HPG = NSA_HEADS // NSA_KV_HEADS
CMP_BLOCK = 32
CMP_STRIDE = 16
SEL_BLOCK = 64
N_SELECT = 16
WINDOW = 512
Q_BLOCK = 64
BAND = 128
REL_BUCKETS = 32
REL_MAX_DIST = 128
GLA_HEADS = 4
GLA_DK = BRANCH_WIDTH // (2 * GLA_HEADS)
GLA_DV = BRANCH_WIDTH // GLA_HEADS
GLA_GATE_RANK = 16
GLA_TAU = 16.0
HGRN_EXPAND = 128
HGRN_HEADS = BRANCH_WIDTH // HGRN_EXPAND
MLSTM_HEADS = 4
MLSTM_DQK = BRANCH_WIDTH // (2 * MLSTM_HEADS)
MLSTM_DV = BRANCH_WIDTH // MLSTM_HEADS
MLSTM_CONV = 4
CHUNK = 64
FF_DIM = ((8 * D_MODEL + 3 * 256 - 1) // (3 * 256)) * 256
EPS = 1e-6
NEG = -1e30
BIG = 1e6
IN_SIZES = (
    NSA_HEADS * HEAD_DIM, 6 * NSA_KV_HEADS * HEAD_DIM, 3 * NSA_HEADS,
    GLA_HEADS * GLA_DK, GLA_HEADS * GLA_DK, GLA_HEADS * GLA_DV, GLA_GATE_RANK, BRANCH_WIDTH,
    BRANCH_WIDTH, BRANCH_WIDTH, BRANCH_WIDTH, BRANCH_WIDTH,
    2 * MLSTM_HEADS * MLSTM_DQK, MLSTM_HEADS * MLSTM_DV, MLSTM_HEADS, MLSTM_HEADS, BRANCH_WIDTH,
    N_BRANCH * D_MODEL,
)
D_IN = sum(IN_SIZES)

V7X_VMEM_LIMIT_BYTES = 56 * 1024 * 1024
LANE = 128
SUB = 8


def _round_up(n, m):
    return (n + m - 1) // m * m


def _mm_kernel(a_ref, w_ref, o_ref):
    o_ref[...] = jnp.dot(a_ref[...], w_ref[...].astype(jnp.bfloat16),
                         preferred_element_type=jnp.float32).astype(o_ref.dtype)


def _mm_k_kernel(a_ref, w_ref, o_ref, acc_ref):
    k = pl.program_id(2)

    @pl.when(k == 0)
    def _():
        acc_ref[...] = jnp.zeros_like(acc_ref)

    acc_ref[...] += jnp.dot(a_ref[...], w_ref[...], preferred_element_type=jnp.float32)

    @pl.when(k == pl.num_programs(2) - 1)
    def _():
        o_ref[...] = acc_ref[...].astype(o_ref.dtype)


def _mm(a, w, *, bm, bn, bk=None, out_dtype=jnp.float32, name="mm"):
    m, kd = a.shape
    n = w.shape[1]
    assert w.shape[0] == kd and m % bm == 0 and n % bn == 0, (a.shape, w.shape, bm, bn)
    params = dict(vmem_limit_bytes=V7X_VMEM_LIMIT_BYTES)
    if bk is None or bk == kd:
        return pl.pallas_call(
            _mm_kernel,
            grid=(n // bn, m // bm),
            in_specs=[pl.BlockSpec((bm, kd), lambda j, i: (i, 0)),
                      pl.BlockSpec((kd, bn), lambda j, i: (0, j))],
            out_specs=pl.BlockSpec((bm, bn), lambda j, i: (i, j)),
            out_shape=jax.ShapeDtypeStruct((m, n), out_dtype),
            compiler_params=pltpu.CompilerParams(dimension_semantics=("arbitrary", "arbitrary"), **params),
            name=name,
        )(a, w)
    assert kd % bk == 0
    return pl.pallas_call(
        _mm_k_kernel,
        grid=(n // bn, m // bm, kd // bk),
        in_specs=[pl.BlockSpec((bm, bk), lambda j, i, k: (i, k)),
                  pl.BlockSpec((bk, bn), lambda j, i, k: (k, j))],
        out_specs=pl.BlockSpec((bm, bn), lambda j, i, k: (i, j)),
        out_shape=jax.ShapeDtypeStruct((m, n), out_dtype),
        scratch_shapes=[pltpu.VMEM((bm, bn), jnp.float32)],
        compiler_params=pltpu.CompilerParams(
            dimension_semantics=("arbitrary", "arbitrary", "arbitrary"), **params),
        name=name,
    )(a, w)


def _matmul(a, w_bf16, name):
    m, kd = a.shape
    a = a.astype(jnp.bfloat16)
    mp = _round_up(m, 16)
    if mp != m:
        a = jnp.pad(a, ((0, mp - m), (0, 0)))
    bm, bn = _tiles(mp, w_bf16.shape[1])
    out = _mm(a, w_bf16, bm=bm, bn=bn, name=name)
    return out[:m] if mp != m else out


def _tiles(m, n):
    if m >= 1024:
        return 1024, 512
    return m, (1024 if n % 1024 == 0 else 512)


def _cparams(n_axes):
    return pltpu.CompilerParams(dimension_semantics=("arbitrary",) * n_axes,
                                vmem_limit_bytes=V7X_VMEM_LIMIT_BYTES)


def _proj_kernel(a_ref, w_ref, o_ref, *, act):
    y = jnp.dot(a_ref[...], w_ref[...], preferred_element_type=jnp.float32)
    if act == "sigmoid":
        y = jax.nn.sigmoid(y)
    o_ref[...] = y.astype(o_ref.dtype)


def _proj(a, w, *, out_dtype=jnp.float32, act=None, name="proj"):
    m, kd = a.shape
    n = w.shape[1]
    bm, bn = _tiles(m, n)
    return pl.pallas_call(
        functools.partial(_proj_kernel, act=act),
        grid=(n // bn, m // bm),
        in_specs=[pl.BlockSpec((bm, kd), lambda j, i: (i, 0)),
                  pl.BlockSpec((kd, bn), lambda j, i: (0, j))],
        out_specs=pl.BlockSpec((bm, bn), lambda j, i: (i, j)),
        out_shape=jax.ShapeDtypeStruct((m, n), out_dtype),
        compiler_params=_cparams(2),
        name=name,
    )(a, w)


def _mm_res_kernel(a_ref, w_ref, x_ref, gt_ref, o_ref, acc_ref):
    k = pl.program_id(2)

    @pl.when(k == 0)
    def _():
        acc_ref[...] = jnp.zeros_like(acc_ref)

    acc_ref[...] += jnp.dot(a_ref[...], w_ref[...], preferred_element_type=jnp.float32)

    @pl.when(k == pl.num_programs(2) - 1)
    def _():
        o_ref[...] = x_ref[...] + gt_ref[0] * acc_ref[...]


def _mm_res(a, w, x, gt3, *, rows_per_gate, bk=None, name="mm_res"):
    m, kd = a.shape
    n = w.shape[1]
    bm, bn = _tiles(m, n)
    bk = kd if bk is None else bk
    if rows_per_gate is None:
        gt_spec = pl.BlockSpec((1, bm, bn), lambda j, i, k: (0, i, j))
    else:
        tiles_per_gate = rows_per_gate // bm
        gt_spec = pl.BlockSpec((1, 1, bn), lambda j, i, k: (i // tiles_per_gate, 0, j))
    return pl.pallas_call(
        _mm_res_kernel,
        grid=(n // bn, m // bm, kd // bk),
        in_specs=[pl.BlockSpec((bm, bk), lambda j, i, k: (i, k)),
                  pl.BlockSpec((bk, bn), lambda j, i, k: (k, j)),
                  pl.BlockSpec((bm, bn), lambda j, i, k: (i, j)),
                  gt_spec],
        out_specs=pl.BlockSpec((bm, bn), lambda j, i, k: (i, j)),
        out_shape=jax.ShapeDtypeStruct((m, n), jnp.float32),
        scratch_shapes=[pltpu.VMEM((bm, bn), jnp.float32)],
        compiler_params=_cparams(3),
        name=name,
    )(a, w, x, gt3)


def _mm_swiglu_kernel(a_ref, wg_ref, wu_ref, o_ref):
    a = a_ref[...]
    g = jnp.dot(a, wg_ref[...], preferred_element_type=jnp.float32)
    u = jnp.dot(a, wu_ref[...], preferred_element_type=jnp.float32)
    o_ref[...] = (g * jax.nn.sigmoid(g) * u).astype(o_ref.dtype)


def _mm_swiglu(a, w_gu, name="mm_swiglu"):
    m, kd = a.shape
    f = w_gu.shape[1] // 2
    bm, bn = _tiles(m, f)
    nb = f // bn
    return pl.pallas_call(
        _mm_swiglu_kernel,
        grid=(nb, m // bm),
        in_specs=[pl.BlockSpec((bm, kd), lambda j, i: (i, 0)),
                  pl.BlockSpec((kd, bn), lambda j, i: (0, j)),
                  pl.BlockSpec((kd, bn), lambda j, i: (0, j + nb))],
        out_specs=pl.BlockSpec((bm, bn), lambda j, i: (i, j)),
        out_shape=jax.ShapeDtypeStruct((m, f), jnp.bfloat16),
        compiler_params=_cparams(2),
        name=name,
    )(a, w_gu, w_gu)


def _merge_kernel(*refs):
    obs, wb_ref, gates, o_ref = refs[:N_BRANCH], refs[N_BRANCH], refs[N_BRANCH + 1:2 * N_BRANCH + 1], refs[-1]
    acc = None
    for br in range(N_BRANCH):
        y = gates[br][...].astype(jnp.float32) * jnp.dot(obs[br][...], wb_ref[br],
                                                         preferred_element_type=jnp.float32)
        acc = y if acc is None else acc + y
    o_ref[...] = acc.astype(o_ref.dtype)


def _merge(obs, wb, gates, name="merge"):
    m, wd = obs[0].shape
    n = wb.shape[2]
    bm, bn = _tiles(m, n)
    nb = n // bn
    gate_specs = [pl.BlockSpec((bm, bn), functools.partial(lambda j, i, br: (i, br * nb + j), br=br))
                  for br in range(N_BRANCH)]
    return pl.pallas_call(
        _merge_kernel,
        grid=(nb, m // bm),
        in_specs=([pl.BlockSpec((bm, wd), lambda j, i: (i, 0))] * N_BRANCH
                  + [pl.BlockSpec((N_BRANCH, wd, bn), lambda j, i: (0, 0, j))] + gate_specs),
        out_specs=pl.BlockSpec((bm, bn), lambda j, i: (i, j)),
        out_shape=jax.ShapeDtypeStruct((m, n), jnp.bfloat16),
        compiler_params=_cparams(2),
        name=name,
    )(*obs, wb, *([gates] * N_BRANCH))


def _normmod_kernel(x_ref, g_ref, sc_ref, sh_ref, o_ref):
    x = x_ref[0]
    y = x * lax.rsqrt(jnp.mean(x * x, axis=-1, keepdims=True) + EPS) * g_ref[...]
    o_ref[0] = (y * (1.0 + sc_ref[0]) + sh_ref[0]).astype(o_ref.dtype)


def _normmod(x, g, sc, sh):
    B, T, D = x.shape
    tt = min(T, 256)
    row = pl.BlockSpec((1, 1, D), lambda b, t: (b, 0, 0))
    return pl.pallas_call(
        _normmod_kernel,
        grid=(B, T // tt),
        in_specs=[pl.BlockSpec((1, tt, D), lambda b, t: (b, t, 0)),
                  pl.BlockSpec((1, D), lambda b, t: (0, 0)), row, row],
        out_specs=pl.BlockSpec((1, tt, D), lambda b, t: (b, t, 0)),
        out_shape=jax.ShapeDtypeStruct((B, T, D), jnp.bfloat16),
        compiler_params=_cparams(2),
        name="normmod",
    )(x, g[None], sc[:, None], sh[:, None])


def _prep_w(w, n_mult=1024, k_mult=None):
    kd, n = w.shape
    np_ = _round_up(n, n_mult)
    kp = kd if k_mult is None else _round_up(kd, k_mult)
    w = w.astype(jnp.bfloat16)
    if np_ != n or kp != kd:
        w = jnp.pad(w, ((0, kp - kd), (0, np_ - n)))
    return w


NSA_TQ = 128
NSA_KB = 128
SEL_PER_KB = NSA_KB // SEL_BLOCK
WIN_TILES = WINDOW // NSA_KB + 1


def _split3_bf16(x):
    hi = x.astype(jnp.bfloat16)
    r1 = x - hi.astype(jnp.float32)
    mid = r1.astype(jnp.bfloat16)
    lo = (r1 - mid.astype(jnp.float32)).astype(jnp.bfloat16)
    return hi, mid, lo


def _nsa_prompt_kernel(qT_ref, ksel_ref, vselT_ref, kwin_ref, vwinT_ref, kc_ref, vcT_ref, covT_ref,
                       bcmp_ref, bsel_ref, bwin_ref, gate_ref, o_ref,
                       mt_ref, m_ref, l_ref, acc_ref, *, n_top):
    hpg = qT_ref.shape[2]
    tq = qT_ref.shape[4]
    ns = covT_ref.shape[0]
    qb = pl.program_id(2)
    t0 = qb * tq
    gates = jax.nn.sigmoid(gate_ref[0, 0])

    kc = kc_ref[0, 0]
    vcT = vcT_ref[0, 0]
    p_sum = jnp.zeros((kc.shape[0], tq), jnp.float32)
    for h in range(hpg):
        bias = bcmp_ref[0, h]
        s = jnp.dot(kc, qT_ref[0, 0, h], preferred_element_type=jnp.float32) + bias
        e = jnp.exp(s - jnp.max(s, axis=0, keepdims=True))
        p = jnp.where(bias > 0.5 * NEG, e / jnp.sum(e, axis=0, keepdims=True), 0.0)
        p_sum = p_sum + p
        o_c = jnp.dot(vcT, p.astype(jnp.bfloat16), preferred_element_type=jnp.float32)
        o_ref[0, 0, h] = gates[0, h:h + 1, :] * o_c
    cov = covT_ref[...]
    imp = sum(jnp.dot(cov, part, preferred_element_type=jnp.float32) for part in _split3_bf16(p_sum))

    blk = lax.broadcasted_iota(jnp.int32, (ns, tq), 0)
    cur = (t0 + lax.broadcasted_iota(jnp.int32, (ns, tq), 1)) // SEL_BLOCK
    forced = (blk == 0) | (blk == cur) | (blk == cur - 1)
    score = jnp.where(forced, BIG, jnp.where(blk <= cur, imp, -BIG))
    rank = jnp.zeros((ns, tq), jnp.int32)
    for jp in range(ns):
        row = score[jp:jp + 1, :]
        beats = (row > score) | ((row == score) & (blk > jp))
        rank = rank + beats.astype(jnp.int32)
    mt_ref[...] = (rank < n_top).astype(jnp.float32)

    sub = lax.broadcasted_iota(jnp.int32, (NSA_KB, tq), 0)

    def attend(k_ref, vT_ref, bias_ref, n_tiles, lo, use_sel):
        m_ref[...] = jnp.full(m_ref.shape, NEG, jnp.float32)
        l_ref[...] = jnp.zeros(l_ref.shape, jnp.float32)
        acc_ref[...] = jnp.zeros(acc_ref.shape, jnp.float32)

        def body(kb, carry):
            koff = pl.multiple_of(kb * NSA_KB, NSA_KB)
            k_blk = k_ref[0, 0, pl.ds(koff, NSA_KB), :]
            vT_blk = vT_ref[0, 0, :, pl.ds(koff, NSA_KB)]
            tile = jnp.minimum(qb - kb, n_tiles - 1)
            if use_sel:
                r0 = mt_ref[pl.ds(kb * SEL_PER_KB, 1), :]
                r1 = mt_ref[pl.ds(kb * SEL_PER_KB + 1, 1), :]
                selm = jnp.where(sub < SEL_BLOCK, r0, r1) > 0.5
            for h in range(hpg):
                s = jnp.dot(k_blk, qT_ref[0, 0, h], preferred_element_type=jnp.float32) + bias_ref[0, h, tile]
                if use_sel:
                    s = jnp.where(selm, s, NEG)
                m_old = m_ref[h]
                m_new = jnp.maximum(m_old, jnp.max(s, axis=0, keepdims=True))
                alpha = jnp.exp(m_old - m_new)
                p = jnp.exp(s - m_new)
                l_ref[h] = alpha * l_ref[h] + jnp.sum(p, axis=0, keepdims=True)
                acc_ref[h] = alpha * acc_ref[h] + jnp.dot(vT_blk, p.astype(jnp.bfloat16),
                                                          preferred_element_type=jnp.float32)
                m_ref[h] = m_new
            return carry

        lax.fori_loop(lo, qb + 1, body, 0)

    attend(ksel_ref, vselT_ref, bsel_ref, bsel_ref.shape[2], 0, True)
    for h in range(hpg):
        o_ref[0, 0, h] += gates[1, h:h + 1, :] * (acc_ref[h] / l_ref[h])

    attend(kwin_ref, vwinT_ref, bwin_ref, bwin_ref.shape[2], jnp.maximum(qb - (WIN_TILES - 1), 0), False)
    for h in range(hpg):
        o_ref[0, 0, h] += gates[2, h:h + 1, :] * (acc_ref[h] / l_ref[h])


def _bias_lookup(rel_bias, dist):
    onehot = jax.nn.one_hot(_t5_bucket(dist), REL_BUCKETS, dtype=jnp.float32)
    return jnp.einsum('...k,kh->...h', onehot, rel_bias, precision=lax.Precision.HIGHEST)


def _bias_tables(rel_bias, T):
    G, HPG = NSA_KV_HEADS, NSA_HPG
    nc = (T - CMP_BLOCK) // CMP_STRIDE + 1
    ncp = T // CMP_STRIDE

    def lookup(dist, valid):
        b = _bias_lookup(rel_bias, dist)
        b = jnp.where(valid[..., None], b, NEG)
        return jnp.moveaxis(b, -1, 0).reshape((G, HPG) + dist.shape)

    c = jnp.arange(NSA_KB)[:, None]
    i = jnp.arange(NSA_TQ)[None, :]
    sel_d = jnp.stack([dlt + i - c for dlt in (0, NSA_KB, 2 * NSA_KB)])
    bsel = lookup(sel_d, sel_d >= 0)
    win_d = jnp.stack([dlt * NSA_KB + i - c for dlt in range(WIN_TILES)])
    bwin = lookup(win_d, (win_d >= 0) & (win_d <= WINDOW))
    n = jnp.arange(ncp)[:, None]
    t = jnp.arange(T)[None, :]
    cmp_d = t - (n * CMP_STRIDE + CMP_BLOCK - 1)
    bcmp = lookup(cmp_d, (cmp_d >= 0) & (n < nc))
    ns = -(-T // SEL_BLOCK)
    covT = _block_cover(nc, ns).T
    covT = jnp.pad(covT, ((0, 0), (0, ncp - nc))).astype(jnp.bfloat16)
    return bsel, bwin, bcmp, covT


def _nsa_prompt(q, k_sel, v_sel, k_win, v_win, kc, vc, ngt, tables):
    B, T, G, HPG, HD = q.shape
    bsel, bwin, bcmp, covT = tables
    ns, ncp = covT.shape
    assert kc.shape == (B, G, ncp, HD), kc.shape
    bf = jnp.bfloat16
    qT = q.astype(bf).transpose(0, 2, 3, 4, 1)
    ksel = k_sel.astype(bf).transpose(0, 2, 1, 3)
    vselT = v_sel.astype(bf).transpose(0, 2, 3, 1)
    kwin = k_win.astype(bf).transpose(0, 2, 1, 3)
    vwinT = v_win.astype(bf).transpose(0, 2, 3, 1)
    kcp = kc.astype(bf)
    vcT = vc.astype(bf).transpose(0, 1, 3, 2)
    gT = ngt.reshape(B, T, 3, G, HPG).transpose(0, 3, 2, 4, 1)
    tq = NSA_TQ
    full = lambda b, g, i: (b, g, 0, 0)
    oT = pl.pallas_call(
        functools.partial(_nsa_prompt_kernel, n_top=min(N_SELECT, ns)),
        grid=(B, G, T // tq),
        in_specs=[
            pl.BlockSpec((1, 1, HPG, HD, tq), lambda b, g, i: (b, g, 0, 0, i)),
            pl.BlockSpec((1, 1, T, HD), full),
            pl.BlockSpec((1, 1, HD, T), full),
            pl.BlockSpec((1, 1, T, HD), full),
            pl.BlockSpec((1, 1, HD, T), full),
            pl.BlockSpec((1, 1, ncp, HD), full),
            pl.BlockSpec((1, 1, HD, ncp), full),
            pl.BlockSpec((ns, ncp), lambda b, g, i: (0, 0)),
            pl.BlockSpec((1, HPG, ncp, tq), lambda b, g, i: (g, 0, 0, i)),
            pl.BlockSpec((1, HPG) + bsel.shape[2:], lambda b, g, i: (g, 0, 0, 0, 0)),
            pl.BlockSpec((1, HPG) + bwin.shape[2:], lambda b, g, i: (g, 0, 0, 0, 0)),
            pl.BlockSpec((1, 1, 3, HPG, tq), lambda b, g, i: (b, g, 0, 0, i)),
        ],
        out_specs=pl.BlockSpec((1, 1, HPG, HD, tq), lambda b, g, i: (b, g, 0, 0, i)),
        out_shape=jax.ShapeDtypeStruct((B, G, HPG, HD, T), jnp.float32),
        scratch_shapes=[pltpu.VMEM((ns, tq), jnp.float32),
                        pltpu.VMEM((HPG, 1, tq), jnp.float32),
                        pltpu.VMEM((HPG, 1, tq), jnp.float32),
                        pltpu.VMEM((HPG, HD, tq), jnp.float32)],
        compiler_params=pltpu.CompilerParams(
            dimension_semantics=("arbitrary", "arbitrary", "arbitrary"),
            vmem_limit_bytes=V7X_VMEM_LIMIT_BYTES),
        name="nsa_prompt",
    )(qT, ksel, vselT, kwin, vwinT, kcp, vcT, covT, bcmp, bsel, bwin, gT)
    return oT.transpose(0, 4, 1, 2, 3).reshape(B, T, G * HPG * HD)


LIN_TB = 256


def _logsigmoid(x):
    return jnp.minimum(x, 0.0) - jnp.log(1.0 + jnp.exp(-jnp.abs(x)))


def _lin_attn_kernel(*refs, mode, chunk):
    if mode == "gla":
        (q_ref, k_ref, v_ref, r_ref, a_ref, wa_ref, ba_ref, gn_ref, o_ref, st_ref,
         q_s, k_s, g_s, sT_ref) = refs
    else:
        (q_ref, k_ref, v_ref, r_ref, llb_ref, l1m_ref, gn_ref, o_ref, st_ref,
         q_s, k_s, g_s, sT_ref) = refs
    tb, dk = q_s.shape
    C = chunk
    t = pl.program_id(2)

    if mode == "gla":
        q_s[...] = q_ref[0] * (dk ** -0.5)
        k_s[...] = k_ref[0]
        pre = jnp.dot(a_ref[0].astype(jnp.bfloat16), wa_ref[...], preferred_element_type=jnp.float32) + ba_ref[...]
        g_s[...] = _logsigmoid(pre) / GLA_TAU
    else:
        x = q_ref[0]
        q_s[...] = x * jax.nn.sigmoid(x)
        u = llb_ref[...]
        w = l1m_ref[...] + _logsigmoid(k_ref[0])
        lf = jnp.maximum(u, w) + jnp.log(1.0 + jnp.exp(-jnp.abs(u - w)))
        g_s[...] = lf
        k_s[...] = 1.0 - jnp.exp(lf)

    @pl.when(t == 0)
    def _():
        sT_ref[...] = jnp.zeros(sT_ref.shape, jnp.float32)

    rr = lax.broadcasted_iota(jnp.int32, (C, C), 0)
    cc = lax.broadcasted_iota(jnp.int32, (C, C), 1)
    tril = (rr >= cc).astype(jnp.bfloat16)
    row8 = lax.broadcasted_iota(jnp.int32, (SUB, dk), 0)
    cc8 = lax.broadcasted_iota(jnp.int32, (SUB, C), 1)
    gn = gn_ref[...]
    bf = jnp.bfloat16

    def chunk_body(c, carry):
        r0 = pl.multiple_of(c * C, C)
        qc = q_s[pl.ds(r0, C), :]
        kc = k_s[pl.ds(r0, C), :]
        vc = v_ref[0, pl.ds(r0, C), :].astype(bf)
        b = sum(jnp.dot(tril, part, preferred_element_type=jnp.float32) for part in _split3_bf16(g_s[pl.ds(r0, C), :]))
        bl = b[C - 1:C, :]

        nt = C // SUB
        b_t = [b[i * SUB:(i + 1) * SUB] for i in range(nt)]
        q_t = [qc[i * SUB:(i + 1) * SUB] for i in range(nt)]
        att_t = [jnp.zeros((SUB, C), jnp.float32) for _ in range(nt)]
        for s in range(C):
            bs = b[s:s + 1, :]
            ks = kc[s:s + 1, :]
            for i in range(s // SUB, nt):
                d = b_t[i] - bs
                if i == s // SUB and s % SUB:
                    d = jnp.where(row8 >= s % SUB, d, NEG)
                col = jnp.sum(q_t[i] * ks * jnp.exp(d), axis=-1, keepdims=True)
                att_t[i] = jnp.where(cc8 == s, col, att_t[i])
        att = jnp.concatenate(att_t, axis=0)
        sT = sT_ref[...]
        o = (lax.dot_general((qc * jnp.exp(b)).astype(bf), sT.astype(bf), (((1,), (1,)), ((), ())),
                             preferred_element_type=jnp.float32)
             + jnp.dot(att.astype(bf), vc, preferred_element_type=jnp.float32))
        sT_ref[...] = sT * jnp.exp(bl) + lax.dot_general(
            vc, (kc * jnp.exp(bl - b)).astype(bf), (((0,), (0,)), ((), ())), preferred_element_type=jnp.float32)
        y = o * lax.rsqrt(jnp.mean(o * o, axis=-1, keepdims=True) + EPS) * gn
        r = r_ref[0, pl.ds(r0, C), :]
        gate = jax.nn.sigmoid(r)
        if mode == "gla":
            gate = r * gate
        o_ref[0, pl.ds(r0, C), :] = (y * gate).astype(o_ref.dtype)
        return carry

    lax.fori_loop(0, tb // C, chunk_body, 0)

    @pl.when(t == pl.num_programs(2) - 1)
    def _():
        st_ref[0, 0] = sT_ref[...]


def _lin_attn_prompt(z, B, T, mode, params):
    tb = LIN_TB
    z3 = z.reshape(B, T, z.shape[-1])
    if mode == "gla":
        H, dk, dv = GLA_HEADS, GLA_DK, GLA_DV
        wa, ba, gn = params
        in_specs = [
            pl.BlockSpec((1, tb, dk), lambda b, h, t: (b, t, h)),
            pl.BlockSpec((1, tb, dk), lambda b, h, t: (b, t, H + h)),
            pl.BlockSpec((1, tb, dv), lambda b, h, t: (b, t, (2 * H * dk) // dv + h)),
            pl.BlockSpec((1, tb, dv), lambda b, h, t: (b, t, (2 * H * dk + H * dv) // dv + h)),
            pl.BlockSpec((1, tb, LANE), lambda b, h, t: (b, t, (2 * H * dk + 2 * H * dv) // LANE)),
            pl.BlockSpec((LANE, dk), lambda b, h, t: (0, h)),
            pl.BlockSpec((1, dk), lambda b, h, t: (0, h)),
            pl.BlockSpec((1, dv), lambda b, h, t: (0, 0)),
        ]
        args = [z3, z3, z3, z3, z3, wa, ba, gn]
    else:
        H, dk, dv = HGRN_HEADS, HGRN_EXPAND, HGRN_EXPAND
        llb, l1m, gn = params
        in_specs = [
            pl.BlockSpec((1, tb, dk), lambda b, h, t: (b, t, h)),
            pl.BlockSpec((1, tb, dk), lambda b, h, t: (b, t, H + h)),
            pl.BlockSpec((1, tb, dv), lambda b, h, t: (b, t, 2 * H + h)),
            pl.BlockSpec((1, tb, dv), lambda b, h, t: (b, t, 3 * H + h)),
            pl.BlockSpec((1, dk), lambda b, h, t: (0, h)),
            pl.BlockSpec((1, dk), lambda b, h, t: (0, h)),
            pl.BlockSpec((1, dv), lambda b, h, t: (0, 0)),
        ]
        args = [z3, z3, z3, z3, llb, l1m, gn]
    o, sT = pl.pallas_call(
        functools.partial(_lin_attn_kernel, mode=mode, chunk=CHUNK),
        grid=(B, H, T // tb),
        in_specs=in_specs,
        out_specs=[pl.BlockSpec((1, tb, dv), lambda b, h, t: (b, t, h)),
                   pl.BlockSpec((1, 1, dv, dk), lambda b, h, t: (b, h, 0, 0))],
        out_shape=[jax.ShapeDtypeStruct((B, T, H * dv), jnp.bfloat16),
                   jax.ShapeDtypeStruct((B, H, dv, dk), jnp.float32)],
        scratch_shapes=[pltpu.VMEM((tb, dk), jnp.float32), pltpu.VMEM((tb, dk), jnp.float32),
                        pltpu.VMEM((tb, dk), jnp.float32), pltpu.VMEM((dv, dk), jnp.float32)],
        compiler_params=pltpu.CompilerParams(
            dimension_semantics=("arbitrary", "arbitrary", "arbitrary"),
            vmem_limit_bytes=V7X_VMEM_LIMIT_BYTES),
        name="lin_attn_" + mode,
    )(*args)
    return o.reshape(B * T, H * dv), sT.transpose(0, 1, 3, 2)


def _mlstm_kernel(xq_ref, xk_ref, v_ref, og_ref, gi_ref, wq_ref, wk_ref, bq_ref, bk_ref, bibf_ref, gn_ref,
                  o_ref, c_out, n_out, m_out, xe_q, xe_k, q_s, k_s, c_s, n_s, m_s, *, chunk):
    tb, dk = q_s.shape
    C = chunk
    t = pl.program_id(2)
    h = pl.program_id(1)
    W = MLSTM_CONV

    @pl.when(t == 0)
    def _():
        xe_q[0:SUB, :] = jnp.zeros((SUB, dk), jnp.float32)
        xe_k[0:SUB, :] = jnp.zeros((SUB, dk), jnp.float32)
        c_s[...] = jnp.zeros(c_s.shape, jnp.float32)
        n_s[...] = jnp.zeros(n_s.shape, jnp.float32)
        m_s[...] = jnp.zeros(m_s.shape, jnp.float32)

    for xe, x_ref, w_ref, b_ref, dst, scale in ((xe_q, xq_ref, wq_ref, bq_ref, q_s, 1.0),
                                                (xe_k, xk_ref, wk_ref, bk_ref, k_s, dk ** -0.5)):
        xe[SUB:SUB + tb, :] = x_ref[0]
        y = b_ref[...]
        for j in range(W):
            y = y + xe[SUB - (W - 1) + j:SUB - (W - 1) + j + tb, :] * w_ref[j:j + 1, :]
        dst[...] = (y * jax.nn.sigmoid(y)) * scale
        xe[SUB - (W - 1):SUB, :] = xe[SUB + tb - (W - 1):SUB + tb, :]

    lane = lax.broadcasted_iota(jnp.int32, (tb, LANE), 1)
    g = gi_ref[0] + bibf_ref[...]
    i_all = jnp.sum(jnp.where(lane == h, g, 0.0), axis=1, keepdims=True)
    f_all = _logsigmoid(jnp.sum(jnp.where(lane == MLSTM_HEADS + h, g, 0.0), axis=1, keepdims=True))
    rr = lax.broadcasted_iota(jnp.int32, (C, C), 0)
    cc = lax.broadcasted_iota(jnp.int32, (C, C), 1)
    causal = rr >= cc
    eye = rr == cc
    gn = gn_ref[...]
    bf = jnp.bfloat16

    def to_row(col):
        return jnp.sum(jnp.where(eye, col, 0.0), axis=0, keepdims=True)

    m = m_s[0:1, 0:1]
    for c in range(tb // C):
        r0 = c * C
        qc = q_s[r0:r0 + C, :]
        kc = k_s[r0:r0 + C, :]
        vc = v_ref[0, r0:r0 + C, :]
        i_col = i_all[r0:r0 + C]
        f_col = f_all[r0:r0 + C]
        f_row = to_row(f_col)
        F_col = jnp.sum(jnp.where(causal, f_row, 0.0), axis=1, keepdims=True)
        F_row = to_row(F_col)
        i_row = to_row(i_col)
        logw = jnp.where(causal, F_col - F_row + i_row, NEG)
        from_state = F_col + m
        m_hat = jnp.maximum(from_state, jnp.max(logw, axis=1, keepdims=True))
        qk = lax.dot_general(qc.astype(bf), kc.astype(bf), (((1,), (1,)), ((), ())),
                             preferred_element_type=jnp.float32)
        w = jnp.exp(logw - m_hat) * qk
        ws = jnp.exp(from_state - m_hat)
        cs = c_s[...]
        ns = n_s[0:1, :]
        num = (ws * lax.dot_general(qc.astype(bf), cs.astype(bf), (((1,), (1,)), ((), ())),
                                    preferred_element_type=jnp.float32)
               + jnp.dot(w.astype(bf), vc.astype(bf), preferred_element_type=jnp.float32))
        den = ws * jnp.sum(qc * ns, axis=1, keepdims=True) + jnp.sum(w, axis=1, keepdims=True)
        hh = num / jnp.maximum(jnp.abs(den), jnp.exp(-m_hat))
        m_new = m_hat[C - 1:C, :]
        F_last = F_col[C - 1:C, :]
        ds = jnp.exp(F_last - F_col + i_col - m_new)
        dst = jnp.exp(F_last + m - m_new)
        c_s[...] = dst * cs + lax.dot_general((vc * ds).astype(bf), kc.astype(bf), (((0,), (0,)), ((), ())),
                                              preferred_element_type=jnp.float32)
        n_s[...] = jnp.broadcast_to(dst * ns + jnp.sum(ds * kc, axis=0, keepdims=True), n_s.shape)
        m = m_new
        y = hh * lax.rsqrt(jnp.mean(hh * hh, axis=-1, keepdims=True) + EPS) * gn
        o_ref[0, r0:r0 + C, :] = (jax.nn.sigmoid(og_ref[0, r0:r0 + C, :]) * y).astype(o_ref.dtype)
    m_s[...] = jnp.broadcast_to(m, m_s.shape)

    @pl.when(t == pl.num_programs(2) - 1)
    def _():
        c_out[0, 0] = c_s[...]
        n_out[0, 0] = n_s[...]
        m_out[0, 0] = m_s[...]


def _mlstm_prompt(z, B, T, p):
    H, dk, dv = MLSTM_HEADS, MLSTM_DQK, MLSTM_DV
    tb = LIN_TB
    z3 = z.reshape(B, T, z.shape[-1])
    nqk = H * dk
    bibf = jnp.pad(jnp.concatenate([p['m_bi'], p['m_bf']]), (0, LANE - 2 * H))[None]
    wconv, bconv = p['m_wconv'], p['m_bconv'][None]
    o, c_f, n_f, m_f = pl.pallas_call(
        functools.partial(_mlstm_kernel, chunk=CHUNK),
        grid=(B, H, T // tb),
        in_specs=[
            pl.BlockSpec((1, tb, dk), lambda b, h, t: (b, t, h)),
            pl.BlockSpec((1, tb, dk), lambda b, h, t: (b, t, H + h)),
            pl.BlockSpec((1, tb, dv), lambda b, h, t: (b, t, (2 * nqk) // dv + h)),
            pl.BlockSpec((1, tb, dv), lambda b, h, t: (b, t, (2 * nqk + H * dv) // dv + h)),
            pl.BlockSpec((1, tb, LANE), lambda b, h, t: (b, t, (2 * nqk + 2 * H * dv) // LANE)),
            pl.BlockSpec((MLSTM_CONV, dk), lambda b, h, t: (0, h)),
            pl.BlockSpec((MLSTM_CONV, dk), lambda b, h, t: (0, H + h)),
            pl.BlockSpec((1, dk), lambda b, h, t: (0, h)),
            pl.BlockSpec((1, dk), lambda b, h, t: (0, H + h)),
            pl.BlockSpec((1, LANE), lambda b, h, t: (0, 0)),
            pl.BlockSpec((1, dv), lambda b, h, t: (0, 0)),
        ],
        out_specs=[pl.BlockSpec((1, tb, dv), lambda b, h, t: (b, t, h)),
                   pl.BlockSpec((1, 1, dv, dk), lambda b, h, t: (b, h, 0, 0)),
                   pl.BlockSpec((1, 1, SUB, dk), lambda b, h, t: (b, h, 0, 0)),
                   pl.BlockSpec((1, 1, SUB, LANE), lambda b, h, t: (b, h, 0, 0))],
        out_shape=[jax.ShapeDtypeStruct((B, T, H * dv), jnp.bfloat16),
                   jax.ShapeDtypeStruct((B, H, dv, dk), jnp.float32),
                   jax.ShapeDtypeStruct((B, H, SUB, dk), jnp.float32),
                   jax.ShapeDtypeStruct((B, H, SUB, LANE), jnp.float32)],
        scratch_shapes=[pltpu.VMEM((SUB + tb, dk), jnp.float32), pltpu.VMEM((SUB + tb, dk), jnp.float32),
                        pltpu.VMEM((tb, dk), jnp.float32), pltpu.VMEM((tb, dk), jnp.float32),
                        pltpu.VMEM((dv, dk), jnp.float32), pltpu.VMEM((SUB, dk), jnp.float32),
                        pltpu.VMEM((SUB, LANE), jnp.float32)],
        compiler_params=_cparams(3),
        name="mlstm_prompt",
    )(z3, z3, z3, z3, z3, wconv, wconv, bconv, bconv, bibf, p['m_gn'][None])
    return o.reshape(B * T, H * dv), c_f, n_f[:, :, 0, :], m_f[:, :, 0, 0]


CMP_PAGES = 16
CMP_PER_PAGE = PAGE_SIZE // CMP_STRIDE
HEAD_ROWS = SUB


KV_ROWS = 4 * NSA_KV_HEADS


def _compress_blocks(load, kind, w1_ref, w2_ref, pe_ref, gk_ref, nblk):
    acc = jnp.zeros((nblk, HEAD_DIM), jnp.float32)
    for j in range(0, CMP_BLOCK, 2):
        xa = load(j) + pe_ref[kind, j:j + 1, :]
        xb = load(j + 1) + pe_ref[kind, j + 1:j + 2, :]
        x2 = jnp.concatenate([xa, xb], axis=1).astype(jnp.bfloat16)
        acc = acc + jnp.dot(x2, w1_ref[kind, j * HEAD_DIM:(j + 2) * HEAD_DIM, :], preferred_element_type=jnp.float32)
    y = jnp.dot((acc * jax.nn.sigmoid(acc)).astype(jnp.bfloat16), w2_ref[kind], preferred_element_type=jnp.float32)
    if kind == 0:
        y = y * lax.rsqrt(jnp.mean(y * y, axis=-1, keepdims=True) + EPS) * gk_ref[...]
    return y


def _cmp_sample_kernel(pt_ref, *refs):
    del pt_ref
    pages = refs[:CMP_PAGES + 1]
    w1_ref, w2_ref, pe_ref, gk_ref, o_ref, xs_ref = refs[CMP_PAGES + 1:]
    rows = PAGE_SIZE * KV_ROWS
    for u in range(CMP_PAGES + 1):
        xs_ref[u * rows:(u + 1) * rows, :] = pages[u][...]
    nblk = CMP_PAGES * CMP_PER_PAGE
    for c in range(2 * NSA_KV_HEADS):
        load = functools.partial(lambda j, c: xs_ref[pl.ds(KV_ROWS * j + c, nblk, stride=KV_ROWS * CMP_STRIDE), :], c=c)
        o_ref[0, c] = _compress_blocks(load, c // NSA_KV_HEADS, w1_ref, w2_ref, pe_ref, gk_ref, nblk)


def _cmp_prompt_kernel(*refs):
    srcs = refs[:2 * NSA_KV_HEADS]
    w1_ref, w2_ref, pe_ref, gk_ref, o_ref, xs_ref = refs[2 * NSA_KV_HEADS:]
    T = srcs[0].shape[1]
    nblk = T // CMP_STRIDE
    xs_ref[T:, :] = jnp.zeros((xs_ref.shape[0] - T, HEAD_DIM), jnp.float32)
    load = lambda j: xs_ref[pl.ds(j, nblk, stride=CMP_STRIDE), :]
    for c in range(2 * NSA_KV_HEADS):
        xs_ref[0:T, :] = srcs[c][0]
        o_ref[0, c] = _compress_blocks(load, c // NSA_KV_HEADS, w1_ref, w2_ref, pe_ref, gk_ref, nblk)


def _cmp_prompt(z3, w1, w2, pe, gk):
    B, T, _ = z3.shape
    G = NSA_KV_HEADS
    first = NSA_HEADS
    full = lambda *shape: pl.BlockSpec(shape, lambda b: (0,) * len(shape))
    return pl.pallas_call(
        _cmp_prompt_kernel,
        grid=(B,),
        in_specs=[pl.BlockSpec((1, T, HEAD_DIM), functools.partial(lambda b, c: (b, 0, first + c), c=c))
                  for c in range(2 * G)] + [
            full(2, CMP_BLOCK * HEAD_DIM, HEAD_DIM), full(2, HEAD_DIM, HEAD_DIM), full(2, CMP_BLOCK, HEAD_DIM),
            full(1, HEAD_DIM)],
        out_specs=pl.BlockSpec((1, 2 * G, T // CMP_STRIDE, HEAD_DIM), lambda b: (b, 0, 0, 0)),
        out_shape=jax.ShapeDtypeStruct((B, 2 * G, T // CMP_STRIDE, HEAD_DIM), jnp.float32),
        scratch_shapes=[pltpu.VMEM((T + PAGE_SIZE, HEAD_DIM), jnp.float32)],
        compiler_params=_cparams(1),
        name="nsa_cmp_prompt",
    )(*([z3] * (2 * G)), w1.astype(jnp.bfloat16), w2.astype(jnp.bfloat16), pe, gk[None])


def _cmp_sample(cache2d, page_table, layer_off, w1, w2, pe, gk):
    B, n_pages = page_table.shape
    G = NSA_KV_HEADS
    assert n_pages % CMP_PAGES == 0
    nblk = CMP_PAGES * CMP_PER_PAGE
    rows = PAGE_SIZE * KV_ROWS

    def page_spec(u):
        def imap(b, grp, pt):
            page = jnp.minimum(grp * CMP_PAGES + u, n_pages - 1)
            return (layer_off + pt[b * n_pages + page], 0)
        return pl.BlockSpec((rows, HEAD_DIM), imap)

    full = lambda *shape: pl.BlockSpec(shape, lambda b, grp, pt: (0,) * len(shape))
    return pl.pallas_call(
        _cmp_sample_kernel,
        grid_spec=pltpu.PrefetchScalarGridSpec(
            num_scalar_prefetch=1,
            grid=(B, n_pages // CMP_PAGES),
            in_specs=[page_spec(u) for u in range(CMP_PAGES + 1)] + [
                full(2, CMP_BLOCK * HEAD_DIM, HEAD_DIM), full(2, HEAD_DIM, HEAD_DIM),
                full(2, CMP_BLOCK, HEAD_DIM), full(1, HEAD_DIM)],
            out_specs=pl.BlockSpec((1, 2 * G, nblk, HEAD_DIM), lambda b, grp, pt: (b, 0, grp, 0)),
            scratch_shapes=[pltpu.VMEM(((CMP_PAGES + 1) * rows, HEAD_DIM), jnp.float32)],
        ),
        out_shape=jax.ShapeDtypeStruct((B, 2 * G, n_pages * CMP_PER_PAGE, HEAD_DIM), jnp.float32),
        compiler_params=_cparams(2),
        name="nsa_cmp_sample",
    )(page_table.reshape(-1), *([cache2d] * (CMP_PAGES + 1)),
      w1.astype(jnp.bfloat16), w2.astype(jnp.bfloat16), pe, gk[None])


def _topk_sample_kernel(q_ref, kc_ref, vc_ref, bias_ref, cov_ref, ocmp_ref, top_ref, *, cur, n_top):
    nsp = cov_ref.shape[1]
    q = q_ref[0, 0]
    bias = bias_ref[0]
    s = lax.dot_general(q, kc_ref[0, 0].astype(jnp.bfloat16), (((1,), (1,)), ((), ())),
                        preferred_element_type=jnp.float32) + bias
    e = jnp.exp(s - jnp.max(s, axis=-1, keepdims=True))
    p = jnp.where(bias > 0.5 * NEG, e / jnp.sum(e, axis=-1, keepdims=True), 0.0)
    ocmp_ref[0, 0] = jnp.dot(p.astype(jnp.bfloat16), vc_ref[0, 0].astype(jnp.bfloat16),
                             preferred_element_type=jnp.float32)
    head = lax.broadcasted_iota(jnp.int32, p.shape, 0)
    p_sum = jnp.sum(jnp.where(head < NSA_HPG, p, 0.0), axis=0, keepdims=True)
    p_sum = jnp.broadcast_to(p_sum, (SUB, p_sum.shape[1]))
    cov = cov_ref[...]
    imp = sum(jnp.dot(part, cov, preferred_element_type=jnp.float32) for part in _split3_bf16(p_sum))[0:1]
    blk = lax.broadcasted_iota(jnp.int32, (1, nsp), 1)
    forced = (blk == 0) | (blk == cur) | (blk == cur - 1)
    score = jnp.where(forced, BIG, jnp.where(blk <= cur, imp, -BIG))
    score = jnp.where(blk <= cur, score, -2.0 * BIG)
    ii = lax.broadcasted_iota(jnp.int32, (nsp, nsp), 0)
    jj = lax.broadcasted_iota(jnp.int32, (nsp, nsp), 1)
    eye = ii == jj

    def to_col(row):
        return jnp.sum(jnp.where(eye, row, 0.0), axis=1, keepdims=True)

    col = to_col(score)
    beats = (col > score) | ((col == score) & (ii < jj))
    rank = jnp.sum(beats.astype(jnp.float32), axis=0, keepdims=True)
    sel = jnp.where((rank < n_top) & (blk < cur), 1.0, 0.0)
    pos = jnp.sum(jnp.where(ii < jj, to_col(sel), 0.0), axis=0, keepdims=True)
    lane = lax.broadcasted_iota(jnp.int32, (1, LANE), 1)
    out = jnp.zeros((1, LANE), jnp.float32)
    blk_f = blk.astype(jnp.float32)
    for k in range(n_top - 1):
        idx = jnp.sum(jnp.where((sel > 0.5) & (pos == k), blk_f, 0.0), axis=1, keepdims=True)
        out = jnp.where(lane == k, idx, out)
    top_ref[0, 0] = jnp.broadcast_to(out, (SUB, LANE)).astype(jnp.int32)


def _topk_sample(q8, kcvc, bias_c, cov, cur):
    B, G = q8.shape[:2]
    nc = kcvc.shape[2]
    nsp = cov.shape[1]
    return pl.pallas_call(
        functools.partial(_topk_sample_kernel, cur=cur, n_top=N_SELECT),
        grid=(B, G),
        in_specs=[pl.BlockSpec((1, 1, HEAD_ROWS, HEAD_DIM), lambda b, g: (b, g, 0, 0)),
                  pl.BlockSpec((1, 1, nc, HEAD_DIM), lambda b, g: (b, g, 0, 0)),
                  pl.BlockSpec((1, 1, nc, HEAD_DIM), lambda b, g: (b, G + g, 0, 0)),
                  pl.BlockSpec((1, HEAD_ROWS, nc), lambda b, g: (g, 0, 0)),
                  pl.BlockSpec((nc, nsp), lambda b, g: (0, 0))],
        out_specs=[pl.BlockSpec((1, 1, HEAD_ROWS, HEAD_DIM), lambda b, g: (b, g, 0, 0)),
                   pl.BlockSpec((1, 1, SUB, LANE), lambda b, g: (b, g, 0, 0))],
        out_shape=[jax.ShapeDtypeStruct((B, G, HEAD_ROWS, HEAD_DIM), jnp.float32),
                   jax.ShapeDtypeStruct((B, G, SUB, LANE), jnp.int32)],
        compiler_params=_cparams(2),
        name="nsa_topk_sample",
    )(q8, kcvc, kcvc, bias_c, cov)


def _sel_sample_kernel(pt_ref, top_ref, q_ref, blk_ref, new_ref, kw_ref, vw_ref, ocmp_ref,
                       bsel_ref, bwin_ref, b0_ref, gate_ref, o_ref, m_ref, l_ref, acc_ref, *, cur):
    del pt_ref
    b, g, k = pl.program_id(0), pl.program_id(1), pl.program_id(2)
    nk = pl.num_programs(2)
    q = q_ref[0, 0]
    qf = q.astype(jnp.float32)
    b0 = b0_ref[0]
    bf = jnp.bfloat16

    @pl.when(k == 0)
    def _():
        m_ref[...] = jnp.sum(qf * new_ref[0, 0, 0:1, :], axis=-1, keepdims=True) + b0
        l_ref[...] = jnp.ones(l_ref.shape, jnp.float32)
        acc_ref[...] = jnp.broadcast_to(new_ref[0, 0, 1:2, :], acc_ref.shape)

    j = top_ref[(b * NSA_KV_HEADS + g) * LANE + k]
    tile = jnp.minimum(cur - j, bsel_ref.shape[1]) - 1
    def group_rows(kind):
        out = None
        for gi in range(NSA_KV_HEADS):
            r = blk_ref[pl.ds(kind * NSA_KV_HEADS + gi, SEL_BLOCK, stride=KV_ROWS), :]
            out = r if out is None else jnp.where(g == gi, r, out)
        return out.astype(bf)

    s = lax.dot_general(q, group_rows(2), (((1,), (1,)), ((), ())),
                        preferred_element_type=jnp.float32) + bsel_ref[0, tile]
    m_old = m_ref[...]
    m_new = jnp.maximum(m_old, jnp.max(s, axis=-1, keepdims=True))
    alpha = jnp.exp(m_old - m_new)
    p = jnp.exp(s - m_new)
    l_ref[...] = alpha * l_ref[...] + jnp.sum(p, axis=-1, keepdims=True)
    acc_ref[...] = alpha * acc_ref[...] + jnp.dot(p.astype(bf), group_rows(3),
                                                  preferred_element_type=jnp.float32)
    m_ref[...] = m_new

    @pl.when(k == nk - 1)
    def _():
        gates = jax.nn.sigmoid(gate_ref[0, 0])
        o_sel = acc_ref[...] / l_ref[...]
        sw = lax.dot_general(q, kw_ref[0].astype(bf), (((1,), (1,)), ((), ())),
                             preferred_element_type=jnp.float32) + bwin_ref[0]
        sn = jnp.sum(qf * new_ref[0, 0, 2:3, :], axis=-1, keepdims=True) + b0
        mw = jnp.maximum(jnp.max(sw, axis=-1, keepdims=True), sn)
        pw = jnp.exp(sw - mw)
        pn = jnp.exp(sn - mw)
        o_w = (jnp.dot(pw.astype(bf), vw_ref[0].astype(bf), preferred_element_type=jnp.float32)
               + pn * new_ref[0, 0, 3:4, :]) / (jnp.sum(pw, axis=-1, keepdims=True) + pn)
        o_ref[0, 0] = gates[0] * ocmp_ref[0, 0] + gates[1] * o_sel + gates[2] * o_w


def _sel_sample(cache2d, page_table, layer_off, top, q8, new_rows, win_rows, ocmp, bsel, bwin, b0, gate8, cur):
    B, n_pages = page_table.shape
    G = NSA_KV_HEADS
    per_page = PAGE_SIZE // SEL_BLOCK
    wb = win_rows.shape[1]

    def sel_map(b, g, k, pt, tp):
        j = tp[(b * G + g) * LANE + k]
        page = layer_off + pt[b * n_pages + j // per_page]
        return (page * per_page + j % per_page, 0)

    sel_spec = pl.BlockSpec((SEL_BLOCK * KV_ROWS, HEAD_DIM), sel_map)

    bg = lambda b, g, k, pt, tp: (b, g, 0, 0)
    return pl.pallas_call(
        functools.partial(_sel_sample_kernel, cur=cur),
        grid_spec=pltpu.PrefetchScalarGridSpec(
            num_scalar_prefetch=2,
            grid=(B, G, N_SELECT - 1),
            in_specs=[
                pl.BlockSpec((1, 1, HEAD_ROWS, HEAD_DIM), bg),
                sel_spec,
                pl.BlockSpec((1, 1, SUB, HEAD_DIM), bg),
                pl.BlockSpec((1, wb, HEAD_DIM), lambda b, g, k, pt, tp: (b, 0, g)),
                pl.BlockSpec((1, wb, HEAD_DIM), lambda b, g, k, pt, tp: (b, 0, G + g)),
                pl.BlockSpec((1, 1, HEAD_ROWS, HEAD_DIM), bg),
                pl.BlockSpec((1,) + bsel.shape[1:], lambda b, g, k, pt, tp: (g, 0, 0, 0)),
                pl.BlockSpec((1,) + bwin.shape[1:], lambda b, g, k, pt, tp: (g, 0, 0)),
                pl.BlockSpec((1,) + b0.shape[1:], lambda b, g, k, pt, tp: (g, 0, 0)),
                pl.BlockSpec((1, 1, 3, HEAD_ROWS, HEAD_DIM), lambda b, g, k, pt, tp: (b, g, 0, 0, 0)),
            ],
            out_specs=pl.BlockSpec((1, 1, HEAD_ROWS, HEAD_DIM), bg),
            scratch_shapes=[pltpu.VMEM((HEAD_ROWS, 1), jnp.float32), pltpu.VMEM((HEAD_ROWS, 1), jnp.float32),
                            pltpu.VMEM((HEAD_ROWS, HEAD_DIM), jnp.float32)],
        ),
        out_shape=jax.ShapeDtypeStruct((B, G, HEAD_ROWS, HEAD_DIM), jnp.float32),
        compiler_params=_cparams(3),
        name="nsa_sel_sample",
    )(page_table.reshape(-1), top.reshape(-1), q8, cache2d, new_rows, win_rows, win_rows,
      ocmp, bsel, bwin, b0, gate8)


def _nsa_sample(z_nsa, cache_l, layer_off, page_table, win_l, p, rel_bias):
    B, n_pages = page_table.shape
    G, HPG, HD = NSA_KV_HEADS, NSA_HPG, HEAD_DIM
    past_len = n_pages * PAGE_SIZE
    cur = past_len // SEL_BLOCK
    nq = z_nsa[:, :G * HPG * HD].reshape(B, G, HPG, HD)
    kv = z_nsa[:, G * HPG * HD:G * HPG * HD + 6 * G * HD].reshape(B, 6, G, HD)
    ngt = z_nsa[:, G * HPG * HD + 6 * G * HD:][:, :3 * G * HPG].reshape(B, 3, G, HPG)
    q = _rmsnorm(nq, p['nsa_gq']) * (HD ** -0.5)
    rows_new = jnp.stack([kv[:, 0], kv[:, 1], _rmsnorm(kv[:, 2], p['nsa_gk'][1]), kv[:, 3]], axis=1)
    win_new = jnp.stack([_rmsnorm(kv[:, 4], p['nsa_gk'][2]), kv[:, 5]], axis=1)
    pad_heads = lambda a: jnp.pad(a, ((0, 0), (0, 0), (0, HEAD_ROWS - HPG), (0, 0)))
    q8 = pad_heads(q).astype(jnp.bfloat16)
    new_rows = jnp.stack([rows_new[:, 2], rows_new[:, 3], win_new[:, 0], win_new[:, 1]], axis=2)
    new_rows = jnp.pad(new_rows, ((0, 0), (0, 0), (0, SUB - 4), (0, 0)))
    gate8 = jnp.pad(ngt.transpose(0, 2, 1, 3)[..., None], ((0, 0), (0, 0), (0, 0), (0, HEAD_ROWS - HPG), (0, 0)))
    gate8 = jnp.broadcast_to(gate8, (B, G, 3, HEAD_ROWS, HD))

    def lookup(dist, valid):
        t = jnp.where(valid[..., None], _bias_lookup(rel_bias, dist), NEG)
        t = jnp.moveaxis(t, -1, 0).reshape((G, HPG) + dist.shape)
        return jnp.pad(t, ((0, 0), (0, HEAD_ROWS - HPG)) + ((0, 0),) * dist.ndim)

    nc = n_pages * CMP_PER_PAGE
    n = jnp.arange(nc)
    bias_c = lookup(past_len - (n * CMP_STRIDE + CMP_BLOCK - 1), n < nc - 1)
    ns = cur + 1
    nsp = _round_up(ns, LANE)
    cov = jnp.pad(_block_cover(nc - 1, ns), ((0, 1), (0, nsp - ns))).astype(jnp.bfloat16)
    jj = jnp.arange(SEL_BLOCK)
    sel_d = jnp.stack([SEL_BLOCK * d - jj for d in (1, 2, 3)])
    bsel = lookup(sel_d, sel_d >= 0).transpose(0, 2, 1, 3)
    wb = win_l.shape[1]
    bwin = lookup(wb - jnp.arange(wb), jnp.ones((wb,), bool))
    b0 = lookup(jnp.zeros((1,), jnp.int32), jnp.ones((1,), bool))

    cache2d = cache_l.reshape(-1, HD)
    kcvc = _cmp_sample(cache2d, page_table, layer_off, p['cmp_w1'], p['cmp_w2'], p['cmp_pe'], p['nsa_gk'][0])
    ocmp, top = _topk_sample(q8, kcvc, bias_c, cov, cur)
    win_rows = win_l.reshape(B, wb, 2 * G * HD)
    o = _sel_sample(cache2d, page_table, layer_off, top[:, :, 0, :], q8, new_rows, win_rows, ocmp,
                    bsel, bwin, b0, gate8, cur)
    o_nsa = o[:, :, :HPG, :].reshape(B, G * HPG * HD)
    win_state = jnp.concatenate([win_l[:, 1:], win_new[:, None]], axis=1)
    return o_nsa, rows_new[:, None], win_state


def _split(z, sizes):
    cuts = [int(c) for c in np.cumsum(sizes)[:-1]]
    return jnp.split(z, cuts, axis=-1)


def _rmsnorm(x, g):
    xf = x.astype(jnp.float32)
    y = xf * lax.rsqrt(jnp.mean(xf * xf, axis=-1, keepdims=True) + EPS)
    return (y * g.astype(jnp.float32)).astype(x.dtype)


def _masked_softmax(s, mask):
    s = jnp.where(mask, s.astype(jnp.float32), NEG)
    return jnp.where(mask, jax.nn.softmax(s, axis=-1), 0.0)


def _t5_bucket(dist):
    d = jnp.maximum(dist, 0)
    exact = REL_BUCKETS // 2
    far = exact + (jnp.log(jnp.maximum(d, 1).astype(jnp.float32) / exact)
                   / math.log(REL_MAX_DIST / exact) * (REL_BUCKETS - exact)).astype(jnp.int32)
    return jnp.where(d < exact, d, jnp.minimum(far, REL_BUCKETS - 1))


def _nsa_compress(k_rows, v_rows, pe, w1, w2, g_kc):
    B, L, G, _ = k_rows.shape
    nc = (L - CMP_BLOCK) // CMP_STRIDE + 1
    idx = jnp.arange(nc)[:, None] * CMP_STRIDE + jnp.arange(CMP_BLOCK)[None, :]

    def phi(rows, pe_, w1_, w2_):
        blk = rows[:, idx] + pe_[None, None, :, None, :]
        flat = blk.transpose(0, 1, 3, 2, 4).reshape(B, nc, G, CMP_BLOCK * HEAD_DIM)
        return jax.nn.silu(flat @ w1_) @ w2_

    kc = _rmsnorm(phi(k_rows, pe[0], w1[0], w2[0]), g_kc)
    vc = phi(v_rows, pe[1], w1[1], w2[1])
    c_end = jnp.arange(nc) * CMP_STRIDE + CMP_BLOCK - 1
    return kc, vc, c_end


def _block_cover(nc, ns):
    c0 = jnp.arange(nc)[:, None] * CMP_STRIDE
    s0 = jnp.arange(ns)[None, :] * SEL_BLOCK
    return ((c0 <= s0 + SEL_BLOCK - 1) & (c0 + CMP_BLOCK - 1 >= s0)).astype(jnp.float32)


def _nsa_cmp_sel(q, q_pos, kc, vc, c_end, ks_t, vs_t, rel_bias, cover):
    B, Tq, G, HPG, _ = q.shape
    L = ks_t.shape[2]
    ns = cover.shape[1]
    s_c = jnp.einsum('btghd,bngd->btghn', q, kc).astype(jnp.float32)
    dist_c = q_pos[:, None] - c_end[None, :]
    bias_c = rel_bias[_t5_bucket(dist_c)].reshape(Tq, -1, G, HPG).transpose(0, 2, 3, 1)
    p_c = _masked_softmax(s_c + bias_c, (dist_c >= 0)[:, None, None, :])
    o_cmp = jnp.einsum('btghn,bngd->btghd', p_c.astype(vc.dtype), vc)
    imp = jnp.einsum('btgn,ns->btgs', p_c.sum(axis=3), cover)
    blk = jnp.arange(ns)[None, :]
    cur = (q_pos // SEL_BLOCK)[:, None]
    forced = ((blk == 0) | (blk == cur) | (blk == cur - 1))[:, None, :]
    valid = (blk <= cur)[:, None, :]
    score = jnp.where(forced, BIG, jnp.where(valid, imp, -BIG))
    n_top = min(N_SELECT, ns)
    _, top = lax.top_k(score, n_top)
    tok = (top[..., None] * SEL_BLOCK + jnp.arange(SEL_BLOCK)).reshape(B, Tq, G, n_top * SEL_BLOCK)
    tok_c = jnp.minimum(tok, L - 1)
    bi = jnp.arange(B)[:, None, None, None]
    gi = jnp.arange(G)[None, None, :, None]
    kg = ks_t[bi, gi, tok_c]
    vg = vs_t[bi, gi, tok_c]
    s_s = jnp.einsum('btghd,btgsd->btghs', q, kg).astype(jnp.float32)
    dist_s = q_pos[None, :, None, None] - tok
    bias_s = rel_bias.reshape(REL_BUCKETS, G, HPG)[_t5_bucket(dist_s), gi]
    p_s = _masked_softmax(s_s + bias_s.transpose(0, 1, 2, 4, 3), (dist_s >= 0)[:, :, :, None, :])
    o_sel = jnp.einsum('btghs,btgsd->btghd', p_s.astype(vg.dtype), vg)
    return o_cmp, o_sel


def _window_attend(q, q_pos, k, v, k_pos, rel_bias):
    N, Qb = q_pos.shape
    Kb = k_pos.shape[1]
    G, HPG = q.shape[3], q.shape[4]
    s = jnp.einsum('bnqghd,bnkgd->bnqghk', q, k).astype(jnp.float32)
    dist = q_pos[:, :, None] - k_pos[:, None, :]
    bias = rel_bias[_t5_bucket(dist)].reshape(N, Qb, Kb, G, HPG).transpose(0, 1, 3, 4, 2)
    mask = ((dist >= 0) & (dist <= WINDOW) & (k_pos[:, None, :] >= 0))[:, :, None, None, :]
    p = _masked_softmax(s + bias, mask)
    return jnp.einsum('bnqghk,bnkgd->bnqghd', p.astype(v.dtype), v)


def _gated_linear_attn(q, k, v, log_a, s0):
    B, T, H, _ = q.shape
    dv = v.shape[-1]
    C = CHUNK if T % CHUNK == 0 else T
    n = T // C

    def chunks(a):
        return a.astype(jnp.float32).reshape(B, n, C, H, a.shape[-1]).transpose(1, 0, 3, 2, 4)

    causal = jnp.tril(jnp.ones((C, C), bool))

    def step(S, inp):
        qc, kc, vc, gc = inp
        b = jnp.cumsum(gc, axis=2)
        diff = jnp.where(causal[:, :, None], b[:, :, :, None, :] - b[:, :, None, :, :], -jnp.inf)
        att = jnp.einsum('bhtk,bhsk,bhtsk->bhts', qc, kc, jnp.exp(diff))
        o = (jnp.einsum('bhtk,bhkv->bhtv', qc * jnp.exp(b), S)
             + jnp.einsum('bhts,bhsv->bhtv', att, vc))
        b_last = b[:, :, -1:, :]
        S = (jnp.exp(b_last[:, :, 0, :])[..., None] * S
             + jnp.einsum('bhsk,bhsv->bhkv', kc * jnp.exp(b_last - b), vc))
        return S, o

    S, o = lax.scan(step, s0.astype(jnp.float32), (chunks(q), chunks(k), chunks(v), chunks(log_a)))
    o = o.transpose(1, 0, 3, 2, 4).reshape(B, T, H, dv)
    return o.astype(v.dtype), S.astype(s0.dtype)


def _mlstm(q, k, v, ig, lf, C0, n0, m0):
    B, T, H, _ = q.shape
    dv = v.shape[-1]
    C = CHUNK if T % CHUNK == 0 else T
    n = T // C

    def chunks(a):
        return a.astype(jnp.float32).reshape(B, n, C, H, a.shape[-1]).transpose(1, 0, 3, 2, 4)

    def gchunks(a):
        return a.astype(jnp.float32).reshape(B, n, C, H).transpose(1, 0, 3, 2)

    causal = jnp.tril(jnp.ones((C, C), bool))

    def step(carry, inp):
        Cs, ns_, m = carry
        qc, kc, vc, ic, fc = inp
        F = jnp.cumsum(fc, axis=-1)
        logw = jnp.where(causal, F[..., :, None] - F[..., None, :] + ic[..., None, :], -jnp.inf)
        from_state = F + m[..., None]
        m_hat = jnp.maximum(from_state, logw.max(-1))
        w = jnp.exp(logw - m_hat[..., None]) * jnp.einsum('bhtk,bhsk->bhts', qc, kc)
        ws = jnp.exp(from_state - m_hat)
        num = ws[..., None] * jnp.einsum('bhtk,bhvk->bhtv', qc, Cs) + jnp.einsum('bhts,bhsv->bhtv', w, vc)
        den = ws * jnp.einsum('bhtk,bhk->bht', qc, ns_) + w.sum(-1)
        h = num / jnp.maximum(jnp.abs(den), jnp.exp(-m_hat))[..., None]
        m_new = m_hat[..., -1]
        ds = jnp.exp(F[..., -1:] - F + ic - m_new[..., None])
        dst = jnp.exp(F[..., -1] + m - m_new)
        Cs = dst[..., None, None] * Cs + jnp.einsum('bhs,bhsv,bhsk->bhvk', ds, vc, kc)
        ns_ = dst[..., None] * ns_ + jnp.einsum('bhs,bhsk->bhk', ds, kc)
        return (Cs, ns_, m_new), h

    carry0 = (C0.astype(jnp.float32), n0.astype(jnp.float32), m0.astype(jnp.float32))
    (Cf, nf, mf), h = lax.scan(step, carry0, (chunks(q), chunks(k), chunks(v), gchunks(ig), gchunks(lf)))
    h = h.transpose(1, 0, 3, 2, 4).reshape(B, T, H, dv)
    return h.astype(v.dtype), Cf.astype(C0.dtype), nf.astype(n0.dtype), mf.astype(m0.dtype)


def _causal_conv(u, buf, w, b):
    T = u.shape[1]
    up = jnp.concatenate([buf, u], axis=1)
    y = b
    for j in range(MLSTM_CONV):
        y = y + up[:, j:j + T] * w[j]
    return jax.nn.silu(y), up[:, -(MLSTM_CONV - 1):]


GROUPS = {
    'nsa': (('nq', 'nkv', 'ngt'), 3072),
    'gla': (('gq', 'gk', 'gv', 'gr', 'ga'), 3584),
    'hgrn': (('hq', 'hf', 'hi', 'hg'), 4096),
    'mlstm': (('mqk', 'mv', 'mo', 'mi', 'mf'), 3584),
    'gate': (('mg',), N_BRANCH * D_MODEL),
}
IN_NAMES = ('nq', 'nkv', 'ngt', 'gq', 'gk', 'gv', 'ga', 'gr', 'hq', 'hf', 'hi', 'hg',
            'mqk', 'mv', 'mi', 'mf', 'mo', 'mg')
IN_WIDTH = dict(zip(IN_NAMES, IN_SIZES))


def _group_weights(w_in_l):
    cuts = dict(zip(IN_NAMES, np.cumsum((0,) + IN_SIZES[:-1])))
    out = {}
    for gname, (members, width) in GROUPS.items():
        w = jnp.concatenate([w_in_l[:, int(cuts[m]):int(cuts[m]) + IN_WIDTH[m]] for m in members], axis=1)
        w = w.astype(jnp.bfloat16)
        out[gname] = jnp.pad(w, ((0, 0), (0, width - w.shape[1])))
    return out


def _group_fields(zs, gname, B, T):
    out, off = {}, 0
    for m in GROUPS[gname][0]:
        out[m] = zs[gname][:, off:off + IN_WIDTH[m]].reshape(B, T, IN_WIDTH[m])
        off += IN_WIDTH[m]
    return out


def _token_mixers(zs, B, T, pos0, past, p, rel_bias, lb, tables):
    G, HPG, HD = NSA_KV_HEADS, NSA_HPG, HEAD_DIM
    f = _group_fields(zs, 'nsa', B, T)
    nq, nkv, ngt = f['nq'], f['nkv'], f['ngt']
    f = _group_fields(zs, 'mlstm', B, T)
    mqk, mv, mo, mi, mf = f['mqk'], f['mv'], f['mo'], f['mi'], f['mf']
    q_pos = pos0 + jnp.arange(T)

    if past is None:
        q = _rmsnorm(nq.reshape(B, T, G, HPG, HD), p['nsa_gq']) * (HD ** -0.5)
        kv = nkv.reshape(B, T, 6, G, HD)
        rows_new = jnp.stack([kv[:, :, 0], kv[:, :, 1], _rmsnorm(kv[:, :, 2], p['nsa_gk'][1]), kv[:, :, 3]], axis=2)
        win_new = jnp.stack([_rmsnorm(kv[:, :, 4], p['nsa_gk'][2]), kv[:, :, 5]], axis=2)
        kcvc = _cmp_prompt(zs['nsa'].reshape(B, T, -1), p['cmp_w1'], p['cmp_w2'], p['cmp_pe'], p['nsa_gk'][0])
        o_nsa = _nsa_prompt(q, rows_new[:, :, 2], rows_new[:, :, 3], win_new[:, :, 0], win_new[:, :, 1],
                            kcvc[:, :G], kcvc[:, G:], ngt, tables)
        win_state = win_new[:, -min(WINDOW, T):]
    else:
        o_nsa, rows_new, win_state = _nsa_sample(zs['nsa'], past['cache'], past['layer_off'], past['page_table'],
                                                 past['win'], p, rel_bias)

    if past is None:
        wa_p = jnp.pad(p['gla_wa'], ((0, LANE - GLA_GATE_RANK), (0, 0))).astype(jnp.bfloat16)
        o_gla, gla_state = _lin_attn_prompt(zs['gla'], B, T, "gla", (wa_p, p['gla_ba'][None], p['gla_gn'][None]))
        o_hgrn, hgrn_state = _lin_attn_prompt(zs['hgrn'], B, T, "hgrn",
                                              (jnp.log(lb)[None], jnp.log1p(-lb)[None], p['hgrn_gn'][None]))
    else:
        f = _group_fields(zs, 'gla', B, T)
        gq, gk, gv, gr, ga = f['gq'], f['gk'], f['gv'], f['gr'], f['ga']
        g_q = gq.reshape(B, T, GLA_HEADS, GLA_DK) * (GLA_DK ** -0.5)
        g_k = gk.reshape(B, T, GLA_HEADS, GLA_DK)
        g_v = gv.reshape(B, T, GLA_HEADS, GLA_DV)
        log_a = (jax.nn.log_sigmoid((ga @ p['gla_wa'] + p['gla_ba']).astype(jnp.float32)) / GLA_TAU).reshape(B, T, GLA_HEADS, GLA_DK)
        o, gla_state = _gated_linear_attn(g_q, g_k, g_v, log_a, past['gla'])
        o_gla = (_rmsnorm(o, p['gla_gn']) * jax.nn.silu(gr.reshape(B, T, GLA_HEADS, GLA_DV))).reshape(B * T, -1)

        f = _group_fields(zs, 'hgrn', B, T)
        hq, hf, hi, hg = f['hq'], f['hf'], f['hi'], f['hg']
        log_f = jnp.logaddexp(jnp.log(lb), jnp.log1p(-lb) + jax.nn.log_sigmoid(hf.astype(jnp.float32)))
        log_f = log_f.reshape(B, T, HGRN_HEADS, HGRN_EXPAND)
        h_q = jax.nn.silu(hq).reshape(B, T, HGRN_HEADS, HGRN_EXPAND)
        h_v = hi.reshape(B, T, HGRN_HEADS, HGRN_EXPAND)
        o, hgrn_state = _gated_linear_attn(h_q, -jnp.expm1(log_f), h_v, log_f, past['hgrn'])
        o_hgrn = (_rmsnorm(o, p['hgrn_gn']) * jax.nn.sigmoid(hg.reshape(B, T, HGRN_HEADS, HGRN_EXPAND))).reshape(B * T, -1)

    nqk = MLSTM_HEADS * MLSTM_DQK
    if past is None:
        o_mlstm, mC, mn, mm = _mlstm_prompt(zs['mlstm'], B, T, p)
        conv_state = mqk[:, -(MLSTM_CONV - 1):]
    else:
        u, conv_state = _causal_conv(mqk, past['conv'], p['m_wconv'], p['m_bconv'])
        m_q = u[..., :nqk].reshape(B, T, MLSTM_HEADS, MLSTM_DQK)
        m_k = u[..., nqk:].reshape(B, T, MLSTM_HEADS, MLSTM_DQK) * (MLSTM_DQK ** -0.5)
        m_v = mv.reshape(B, T, MLSTM_HEADS, MLSTM_DV)
        ig = (mi + p['m_bi']).astype(jnp.float32)
        lf = jax.nn.log_sigmoid((mf + p['m_bf']).astype(jnp.float32))
        hm, mC, mn, mm = _mlstm(m_q, m_k, m_v, ig, lf, past['mC'], past['mn'], past['mm'])
        o_mlstm = (jax.nn.sigmoid(mo).reshape(B, T, MLSTM_HEADS, MLSTM_DV) * _rmsnorm(hm, p['m_gn'])).reshape(B, T, -1)

    obs = [o.reshape(B * T, BRANCH_WIDTH).astype(jnp.bfloat16) for o in (o_nsa, o_gla, o_hgrn, o_mlstm)]
    return obs, (rows_new, win_state, gla_state, hgrn_state, mC, mn, mm, conv_state)


def _layer(x, mod, pos0, past, p, rel_bias, lb, tables=None):
    B, T, _ = x.shape
    m = B * T
    mp = _round_up(m, 16)

    def rows(a):
        a = a.reshape(m, a.shape[-1])
        return a if mp == m else jnp.pad(a, ((0, mp - m), (0, 0)))

    sh1, sc1, gt1, sh2, sc2, gt2 = jnp.split(mod, 6, axis=-1)
    if T > 1:
        gates3 = lambda g: (g[:, None, :], T)
    else:
        gates3 = lambda g: (rows(g[:, None, :])[None], None)
    hb = rows(_normmod(x, p['g_mix'], sc1, sh1))
    zs = {g: _proj(hb, p['w_' + g], name="proj_" + g) for g in ('nsa', 'gla', 'hgrn', 'mlstm')}
    gates = _proj(hb, p['w_gate'], out_dtype=jnp.bfloat16, act="sigmoid", name="proj_gate")
    obs, st = _token_mixers({g: z[:m] for g, z in zs.items()}, B, T, pos0, past, p, rel_bias, lb, tables)
    merged = _merge([rows(o) for o in obs], p['w_branch'], gates)
    g3, rpg = gates3(gt1)
    x1 = _mm_res(merged, p['w_out'], rows(x), g3, rows_per_gate=rpg, name="mm_out")
    hb2 = rows(_normmod(x1[:m].reshape(B, T, D_MODEL), p['g_ffn'], sc2, sh2))
    act = _mm_swiglu(hb2, p['w_ffn_in'], name="mm_ffn_in")
    g3, rpg = gates3(gt2)
    x2 = _mm_res(act, p['w_ffn_out'], x1, g3, rows_per_gate=rpg, bk=p['w_ffn_out'].shape[0] // 4, name="mm_ffn_out")
    return x2[:m].reshape(B, T, D_MODEL), st


def kernel(x_prompt, x_sample, cache_nsa_kv, state_nsa_win, state_gla, state_hgrn, state_mlstm_C,
           state_mlstm_n, state_mlstm_m, state_mlstm_conv, page_table, c_prompt, c_sample, rel_bias,
           w_ada, b_ada, g_mix, g_ffn, w_in, nsa_gq, nsa_gk, cmp_pe, cmp_w1, cmp_w2, gla_wa, gla_ba,
           gla_gn, hgrn_lb, hgrn_gn, m_wconv, m_bconv, m_bi, m_bf, m_gn, w_branch, w_out, w_ffn_in, w_ffn_out):
    lb_cum = jnp.cumsum(jax.nn.softmax(hgrn_lb.astype(jnp.float32), axis=0), axis=0)
    lb_all = lb_cum - lb_cum[:1]
    dec_b, n_pages = page_table.shape
    past_len = n_pages * PAGE_SIZE
    n_prompt = c_prompt.shape[0]
    ffp = _round_up(FF_DIM, 1024)
    x_p, x_s = x_prompt, x_sample
    st_prompt, st_sample = [], []
    c_all = jax.nn.silu(jnp.concatenate([c_prompt, c_sample], axis=0))
    tables = _bias_tables(rel_bias, x_prompt.shape[1])
    n_pool = cache_nsa_kv.shape[1]
    cache_all = cache_nsa_kv.reshape((DEPTH * n_pool,) + cache_nsa_kv.shape[2:])
    for l in range(DEPTH):
        w_ffn_in_l = jnp.concatenate(
            [_prep_w(w_ffn_in[l][:, :FF_DIM]), _prep_w(w_ffn_in[l][:, FF_DIM:])], axis=1)
        p = {'g_mix': g_mix[l], 'g_ffn': g_ffn[l],
             'nsa_gq': nsa_gq[l], 'nsa_gk': nsa_gk[l], 'cmp_pe': cmp_pe[l], 'cmp_w1': cmp_w1[l],
             'cmp_w2': cmp_w2[l], 'gla_wa': gla_wa[l], 'gla_ba': gla_ba[l], 'gla_gn': gla_gn[l],
             'hgrn_gn': hgrn_gn[l], 'm_wconv': m_wconv[l], 'm_bconv': m_bconv[l], 'm_bi': m_bi[l],
             'm_bf': m_bf[l], 'm_gn': m_gn[l],
             'w_branch': w_branch[l].astype(jnp.bfloat16),
             'w_out': _prep_w(w_out[l]),
             'w_ffn_in': w_ffn_in_l, 'w_ffn_out': _prep_w(w_ffn_out[l], k_mult=1024)}
        p.update({'w_' + g: w for g, w in _group_weights(w_in[l]).items()})
        assert p['w_ffn_out'].shape[0] == ffp
        mod = _matmul(c_all, w_ada[l], "mm_ada") + b_ada[l]
        x_p, sp = _layer(x_p, mod[:n_prompt], 0, None, p, rel_bias, lb_all[l], tables)
        past = {'cache': cache_all, 'layer_off': l * n_pool, 'page_table': page_table,
                'win': state_nsa_win[l], 'gla': state_gla[l], 'hgrn': state_hgrn[l],
                'mC': state_mlstm_C[l], 'mn': state_mlstm_n[l], 'mm': state_mlstm_m[l],
                'conv': state_mlstm_conv[l]}
        x_s, ss = _layer(x_s, mod[n_prompt:], past_len, past, p, rel_bias, lb_all[l])
        st_prompt.append(sp)
        st_sample.append(ss)

    def stk(sts, i):
        return jnp.stack([s[i] for s in sts], axis=0)

    outs = [x_p, x_s]
    for i in range(8):
        outs.append(stk(st_prompt, i))
        outs.append(stk(st_sample, i))
    return tuple(outs)
```

```python
import functools
import math

import jax
import jax.numpy as jnp
import numpy as np
from jax import lax
from jax.experimental import pallas as pl
from jax.experimental.pallas import tpu as pltpu

D_MODEL = 4096
DEPTH = 2
PAGE_SIZE = 128
HEAD_DIM = 128
N_BRANCH = 4
BRANCH_WIDTH = D_MODEL // N_BRANCH
NSA_HEADS = BRANCH_WIDTH // HEAD_DIM
NSA_KV_HEADS = 2
NSA_HPG = NSA_HEADS // NSA_KV_HEADS
CMP_BLOCK = 32
CMP_STRIDE = 16
SEL_BLOCK = 64
N_SELECT = 16
WINDOW = 512
Q_BLOCK = 64
BAND = 128
REL_BUCKETS = 32
REL_MAX_DIST = 128
GLA_HEADS = 4
GLA_DK = BRANCH_WIDTH // (2 * GLA_HEADS)
GLA_DV = BRANCH_WIDTH // GLA_HEADS
GLA_GATE_RANK = 16
GLA_TAU = 16.0
HGRN_EXPAND = 128
HGRN_HEADS = BRANCH_WIDTH // HGRN_EXPAND
MLSTM_HEADS = 4
MLSTM_DQK = BRANCH_WIDTH // (2 * MLSTM_HEADS)
MLSTM_DV = BRANCH_WIDTH // MLSTM_HEADS
MLSTM_CONV = 4
CHUNK = 64
FF_DIM = ((8 * D_MODEL + 3 * 256 - 1) // (3 * 256)) * 256
EPS = 1e-6
NEG = -1e30
BIG = 1e6
IN_SIZES = (
    NSA_HEADS * HEAD_DIM, 6 * NSA_KV_HEADS * HEAD_DIM, 3 * NSA_HEADS,
    GLA_HEADS * GLA_DK, GLA_HEADS * GLA_DK, GLA_HEADS * GLA_DV, GLA_GATE_RANK, BRANCH_WIDTH,
    BRANCH_WIDTH, BRANCH_WIDTH, BRANCH_WIDTH, BRANCH_WIDTH,
    2 * MLSTM_HEADS * MLSTM_DQK, MLSTM_HEADS * MLSTM_DV, MLSTM_HEADS, MLSTM_HEADS, BRANCH_WIDTH,
    N_BRANCH * D_MODEL,
)
D_IN = sum(IN_SIZES)

V7X_VMEM_LIMIT_BYTES = 56 * 1024 * 1024
LANE = 128
SUB = 8


def _round_up(n, m):
    return (n + m - 1) // m * m


def _mm_kernel(a_ref, w_ref, o_ref):
    o_ref[...] = jnp.dot(a_ref[...], w_ref[...].astype(jnp.bfloat16),
                         preferred_element_type=jnp.float32).astype(o_ref.dtype)


def _mm_k_kernel(a_ref, w_ref, o_ref, acc_ref):
    k = pl.program_id(2)

    @pl.when(k == 0)
    def _():
        acc_ref[...] = jnp.zeros_like(acc_ref)

    acc_ref[...] += jnp.dot(a_ref[...], w_ref[...], preferred_element_type=jnp.float32)

    @pl.when(k == pl.num_programs(2) - 1)
    def _():
        o_ref[...] = acc_ref[...].astype(o_ref.dtype)


def _mm(a, w, *, bm, bn, bk=None, out_dtype=jnp.float32, name="mm"):
    m, kd = a.shape
    n = w.shape[1]
    assert w.shape[0] == kd and m % bm == 0 and n % bn == 0, (a.shape, w.shape, bm, bn)
    params = dict(vmem_limit_bytes=V7X_VMEM_LIMIT_BYTES)
    if bk is None or bk == kd:
        return pl.pallas_call(
            _mm_kernel,
            grid=(n // bn, m // bm),
            in_specs=[pl.BlockSpec((bm, kd), lambda j, i: (i, 0)),
                      pl.BlockSpec((kd, bn), lambda j, i: (0, j))],
            out_specs=pl.BlockSpec((bm, bn), lambda j, i: (i, j)),
            out_shape=jax.ShapeDtypeStruct((m, n), out_dtype),
            compiler_params=pltpu.CompilerParams(dimension_semantics=("arbitrary", "arbitrary"), **params),
            name=name,
        )(a, w)
    assert kd % bk == 0
    return pl.pallas_call(
        _mm_k_kernel,
        grid=(n // bn, m // bm, kd // bk),
        in_specs=[pl.BlockSpec((bm, bk), lambda j, i, k: (i, k)),
                  pl.BlockSpec((bk, bn), lambda j, i, k: (k, j))],
        out_specs=pl.BlockSpec((bm, bn), lambda j, i, k: (i, j)),
        out_shape=jax.ShapeDtypeStruct((m, n), out_dtype),
        scratch_shapes=[pltpu.VMEM((bm, bn), jnp.float32)],
        compiler_params=pltpu.CompilerParams(
            dimension_semantics=("arbitrary", "arbitrary", "arbitrary"), **params),
        name=name,
    )(a, w)


def _matmul(a, w_bf16, name):
    m, kd = a.shape
    a = a.astype(jnp.bfloat16)
    mp = _round_up(m, 16)
    if mp != m:
        a = jnp.pad(a, ((0, mp - m), (0, 0)))
    bm, bn = _tiles(mp, w_bf16.shape[1])
    out = _mm(a, w_bf16, bm=bm, bn=bn, name=name)
    return out[:m] if mp != m else out


def _tiles(m, n):
    if m >= 1024:
        return 1024, 512
    return m, (1024 if n % 1024 == 0 else 512)


def _cparams(n_axes):
    return pltpu.CompilerParams(dimension_semantics=("arbitrary",) * n_axes,
                                vmem_limit_bytes=V7X_VMEM_LIMIT_BYTES)


def _proj_kernel(a_ref, w_ref, o_ref, *, act):
    y = jnp.dot(a_ref[...], w_ref[...], preferred_element_type=jnp.float32)
    if act == "sigmoid":
        y = jax.nn.sigmoid(y)
    o_ref[...] = y.astype(o_ref.dtype)


def _proj(a, w, *, out_dtype=jnp.float32, act=None, name="proj"):
    m, kd = a.shape
    n = w.shape[1]
    bm, bn = _tiles(m, n)
    return pl.pallas_call(
        functools.partial(_proj_kernel, act=act),
        grid=(n // bn, m // bm),
        in_specs=[pl.BlockSpec((bm, kd), lambda j, i: (i, 0)),
                  pl.BlockSpec((kd, bn), lambda j, i: (0, j))],
        out_specs=pl.BlockSpec((bm, bn), lambda j, i: (i, j)),
        out_shape=jax.ShapeDtypeStruct((m, n), out_dtype),
        compiler_params=_cparams(2),
        name=name,
    )(a, w)


def _mm_res_kernel(a_ref, w_ref, x_ref, gt_ref, o_ref, acc_ref):
    k = pl.program_id(2)

    @pl.when(k == 0)
    def _():
        acc_ref[...] = jnp.zeros_like(acc_ref)

    acc_ref[...] += jnp.dot(a_ref[...], w_ref[...], preferred_element_type=jnp.float32)

    @pl.when(k == pl.num_programs(2) - 1)
    def _():
        o_ref[...] = x_ref[...] + gt_ref[0] * acc_ref[...]


def _mm_res(a, w, x, gt3, *, rows_per_gate, bk=None, name="mm_res"):
    m, kd = a.shape
    n = w.shape[1]
    bm, bn = _tiles(m, n)
    bk = kd if bk is None else bk
    if rows_per_gate is None:
        gt_spec = pl.BlockSpec((1, bm, bn), lambda j, i, k: (0, i, j))
    else:
        tiles_per_gate = rows_per_gate // bm
        gt_spec = pl.BlockSpec((1, 1, bn), lambda j, i, k: (i // tiles_per_gate, 0, j))
    return pl.pallas_call(
        _mm_res_kernel,
        grid=(n // bn, m // bm, kd // bk),
        in_specs=[pl.BlockSpec((bm, bk), lambda j, i, k: (i, k)),
                  pl.BlockSpec((bk, bn), lambda j, i, k: (k, j)),
                  pl.BlockSpec((bm, bn), lambda j, i, k: (i, j)),
                  gt_spec],
        out_specs=pl.BlockSpec((bm, bn), lambda j, i, k: (i, j)),
        out_shape=jax.ShapeDtypeStruct((m, n), jnp.float32),
        scratch_shapes=[pltpu.VMEM((bm, bn), jnp.float32)],
        compiler_params=_cparams(3),
        name=name,
    )(a, w, x, gt3)


def _mm_swiglu_kernel(a_ref, wg_ref, wu_ref, o_ref):
    a = a_ref[...]
    g = jnp.dot(a, wg_ref[...], preferred_element_type=jnp.float32)
    u = jnp.dot(a, wu_ref[...], preferred_element_type=jnp.float32)
    o_ref[...] = (g * jax.nn.sigmoid(g) * u).astype(o_ref.dtype)


def _mm_swiglu(a, w_gu, name="mm_swiglu"):
    m, kd = a.shape
    f = w_gu.shape[1] // 2
    bm, bn = _tiles(m, f)
    nb = f // bn
    return pl.pallas_call(
        _mm_swiglu_kernel,
        grid=(nb, m // bm),
        in_specs=[pl.BlockSpec((bm, kd), lambda j, i: (i, 0)),
                  pl.BlockSpec((kd, bn), lambda j, i: (0, j)),
                  pl.BlockSpec((kd, bn), lambda j, i: (0, j + nb))],
        out_specs=pl.BlockSpec((bm, bn), lambda j, i: (i, j)),
        out_shape=jax.ShapeDtypeStruct((m, f), jnp.bfloat16),
        compiler_params=_cparams(2),
        name=name,
    )(a, w_gu, w_gu)


def _ada_kernel(a_ref, w_ref, b_ref, o_ref):
    o_ref[...] = jnp.dot(a_ref[...], w_ref[0].astype(jnp.bfloat16), preferred_element_type=jnp.float32) + b_ref[0]


def _ada_mod(c_act, w_ada, b_ada, layer):
    m, kd = c_act.shape
    n = w_ada.shape[2]
    bn = 1024
    return pl.pallas_call(
        _ada_kernel,
        grid=(n // bn,),
        in_specs=[pl.BlockSpec((m, kd), lambda j: (0, 0)),
                  pl.BlockSpec((1, kd, bn), lambda j: (layer, 0, j)),
                  pl.BlockSpec((1, 1, bn), lambda j: (layer, 0, j))],
        out_specs=pl.BlockSpec((m, bn), lambda j: (0, j)),
        out_shape=jax.ShapeDtypeStruct((m, n), jnp.float32),
        compiler_params=_cparams(1),
        name="ada_mod",
    )(c_act, w_ada, b_ada[:, None, :])


def _swiglu32_kernel(a_ref, wg_ref, wu_ref, o_ref, wg_s, wu_s):
    @pl.when(pl.program_id(1) == 0)
    def _():
        wg_s[...] = wg_ref[0].astype(jnp.bfloat16)
        wu_s[...] = wu_ref[0].astype(jnp.bfloat16)

    a = a_ref[...]
    g = jnp.dot(a, wg_s[...], preferred_element_type=jnp.float32)
    u = jnp.dot(a, wu_s[...], preferred_element_type=jnp.float32)
    o_ref[...] = (g * jax.nn.sigmoid(g) * u).astype(o_ref.dtype)


FFN_BN = 256


def _swiglu32(a, w_gu, layer, name="mm_ffn_in"):
    m, kd = a.shape
    f = w_gu.shape[2] // 2
    bm = min(m, 1024)
    bn = FFN_BN
    nb = f // bn
    assert f % bn == 0 and m % bm == 0
    return pl.pallas_call(
        _swiglu32_kernel,
        grid=(nb, m // bm),
        in_specs=[pl.BlockSpec((bm, kd), lambda j, i: (i, 0)),
                  pl.BlockSpec((1, kd, bn), lambda j, i: (layer, 0, j)),
                  pl.BlockSpec((1, kd, bn), lambda j, i: (layer, 0, j + nb))],
        out_specs=pl.BlockSpec((bm, bn), lambda j, i: (i, j)),
        out_shape=jax.ShapeDtypeStruct((m, f), jnp.bfloat16),
        scratch_shapes=[pltpu.VMEM((kd, bn), jnp.bfloat16), pltpu.VMEM((kd, bn), jnp.bfloat16)],
        compiler_params=_cparams(2),
        name=name,
    )(a, w_gu, w_gu)


def _res32_kernel(a_ref, w_ref, x_ref, gt_ref, o_ref, acc_ref, w_s, *, bk):
    i, k = pl.program_id(1), pl.program_id(2)
    koff = pl.multiple_of(k * bk, bk)

    @pl.when(i == 0)
    def _():
        w_s[pl.ds(koff, bk), :] = w_ref[0].astype(jnp.bfloat16)

    @pl.when(k == 0)
    def _():
        acc_ref[...] = jnp.zeros_like(acc_ref)

    acc_ref[...] += jnp.dot(a_ref[...], w_s[pl.ds(koff, bk), :], preferred_element_type=jnp.float32)

    @pl.when(k == pl.num_programs(2) - 1)
    def _():
        o_ref[...] = x_ref[...] + gt_ref[0] * acc_ref[...]


def _res32(a, w, layer, x, gt3, *, rows_per_gate, n_k, name):
    m, kd = a.shape
    n = w.shape[2]
    bm = min(m, 512)
    bn = 512
    bk = kd // n_k
    assert kd % n_k == 0 and bk % 16 == 0 and m % bm == 0 and n % bn == 0
    if rows_per_gate is None:
        gt_spec = pl.BlockSpec((1, bm, bn), lambda j, i, k: (0, i, j))
    else:
        tiles_per_gate = rows_per_gate // bm
        gt_spec = pl.BlockSpec((1, 1, bn), lambda j, i, k: (i // tiles_per_gate, 0, j))
    return pl.pallas_call(
        functools.partial(_res32_kernel, bk=bk),
        grid=(n // bn, m // bm, n_k),
        in_specs=[pl.BlockSpec((bm, bk), lambda j, i, k: (i, k)),
                  pl.BlockSpec((1, bk, bn), lambda j, i, k: (layer, k, j)),
                  pl.BlockSpec((bm, bn), lambda j, i, k: (i, j)),
                  gt_spec],
        out_specs=pl.BlockSpec((bm, bn), lambda j, i, k: (i, j)),
        out_shape=jax.ShapeDtypeStruct((m, n), jnp.float32),
        scratch_shapes=[pltpu.VMEM((bm, bn), jnp.float32), pltpu.VMEM((kd, bn), jnp.bfloat16)],
        compiler_params=_cparams(3),
        name=name,
    )(a, w, x, gt3)


def _merge_kernel(*refs):
    obs, wb_ref, gates, o_ref = refs[:N_BRANCH], refs[N_BRANCH], refs[N_BRANCH + 1:2 * N_BRANCH + 1], refs[-1]
    acc = None
    for br in range(N_BRANCH):
        y = gates[br][...].astype(jnp.float32) * jnp.dot(obs[br][...], wb_ref[br],
                                                         preferred_element_type=jnp.float32)
        acc = y if acc is None else acc + y
    o_ref[...] = acc.astype(o_ref.dtype)


def _merge(obs, wb, gates, name="merge"):
    m, wd = obs[0].shape
    n = wb.shape[2]
    bm, bn = _tiles(m, n)
    nb = n // bn
    gate_specs = [pl.BlockSpec((bm, bn), functools.partial(lambda j, i, br: (i, br * nb + j), br=br))
                  for br in range(N_BRANCH)]
    return pl.pallas_call(
        _merge_kernel,
        grid=(nb, m // bm),
        in_specs=([pl.BlockSpec((bm, wd), lambda j, i: (i, 0))] * N_BRANCH
                  + [pl.BlockSpec((N_BRANCH, wd, bn), lambda j, i: (0, 0, j))] + gate_specs),
        out_specs=pl.BlockSpec((bm, bn), lambda j, i: (i, j)),
        out_shape=jax.ShapeDtypeStruct((m, n), jnp.bfloat16),
        compiler_params=_cparams(2),
        name=name,
    )(*obs, wb, *([gates] * N_BRANCH))


def _normmod_kernel(x_ref, g_ref, sc_ref, sh_ref, o_ref):
    x = x_ref[0]
    y = x * lax.rsqrt(jnp.mean(x * x, axis=-1, keepdims=True) + EPS) * g_ref[...]
    o_ref[0] = (y * (1.0 + sc_ref[0]) + sh_ref[0]).astype(o_ref.dtype)


def _normmod(x, g, sc, sh):
    B, T, D = x.shape
    tt = min(T, 256)
    row = pl.BlockSpec((1, 1, D), lambda b, t: (b, 0, 0))
    return pl.pallas_call(
        _normmod_kernel,
        grid=(B, T // tt),
        in_specs=[pl.BlockSpec((1, tt, D), lambda b, t: (b, t, 0)),
                  pl.BlockSpec((1, D), lambda b, t: (0, 0)), row, row],
        out_specs=pl.BlockSpec((1, tt, D), lambda b, t: (b, t, 0)),
        out_shape=jax.ShapeDtypeStruct((B, T, D), jnp.bfloat16),
        compiler_params=_cparams(2),
        name="normmod",
    )(x, g[None], sc[:, None], sh[:, None])


def _prep_w(w, n_mult=1024, k_mult=None):
    kd, n = w.shape
    np_ = _round_up(n, n_mult)
    kp = kd if k_mult is None else _round_up(kd, k_mult)
    w = w.astype(jnp.bfloat16)
    if np_ != n or kp != kd:
        w = jnp.pad(w, ((0, kp - kd), (0, np_ - n)))
    return w


NSA_TQ = 128
NSA_KB = 128
SEL_PER_KB = NSA_KB // SEL_BLOCK
WIN_TILES = WINDOW // NSA_KB + 1


def _split3_bf16(x):
    hi = x.astype(jnp.bfloat16)
    r1 = x - hi.astype(jnp.float32)
    mid = r1.astype(jnp.bfloat16)
    lo = (r1 - mid.astype(jnp.float32)).astype(jnp.bfloat16)
    return hi, mid, lo


def _nsa_prompt_kernel(qT_ref, ksel_ref, vselT_ref, kwin_ref, vwinT_ref, kc_ref, vcT_ref, covT_ref,
                       bcmp_ref, bsel_ref, bwin_ref, gate_ref, o_ref,
                       mt_ref, m_ref, l_ref, acc_ref, *, n_top):
    hpg = qT_ref.shape[2]
    tq = qT_ref.shape[4]
    ns = covT_ref.shape[0]
    qb = pl.program_id(2)
    t0 = qb * tq
    gates = jax.nn.sigmoid(gate_ref[0, 0])

    kc = kc_ref[0, 0]
    vcT = vcT_ref[0, 0]
    p_sum = jnp.zeros((kc.shape[0], tq), jnp.float32)
    for h in range(hpg):
        bias = bcmp_ref[0, h]
        s = jnp.dot(kc, qT_ref[0, 0, h], preferred_element_type=jnp.float32) + bias
        e = jnp.exp(s - jnp.max(s, axis=0, keepdims=True))
        p = jnp.where(bias > 0.5 * NEG, e / jnp.sum(e, axis=0, keepdims=True), 0.0)
        p_sum = p_sum + p
        o_c = jnp.dot(vcT, p.astype(jnp.bfloat16), preferred_element_type=jnp.float32)
        o_ref[0, 0, h] = gates[0, h:h + 1, :] * o_c
    cov = covT_ref[...]
    imp = sum(jnp.dot(cov, part, preferred_element_type=jnp.float32) for part in _split3_bf16(p_sum))

    blk = lax.broadcasted_iota(jnp.int32, (ns, tq), 0)
    cur = (t0 + lax.broadcasted_iota(jnp.int32, (ns, tq), 1)) // SEL_BLOCK
    forced = (blk == 0) | (blk == cur) | (blk == cur - 1)
    score = jnp.where(forced, BIG, jnp.where(blk <= cur, imp, -BIG))
    rank = jnp.zeros((ns, tq), jnp.int32)
    for jp in range(ns):
        row = score[jp:jp + 1, :]
        beats = (row > score) | ((row == score) & (blk > jp))
        rank = rank + beats.astype(jnp.int32)
    mt_ref[...] = (rank < n_top).astype(jnp.float32)

    sub = lax.broadcasted_iota(jnp.int32, (NSA_KB, tq), 0)

    def attend(k_ref, vT_ref, bias_ref, n_tiles, lo, use_sel):
        m_ref[...] = jnp.full(m_ref.shape, NEG, jnp.float32)
        l_ref[...] = jnp.zeros(l_ref.shape, jnp.float32)
        acc_ref[...] = jnp.zeros(acc_ref.shape, jnp.float32)

        def body(kb, carry):
            koff = pl.multiple_of(kb * NSA_KB, NSA_KB)
            k_blk = k_ref[0, 0, pl.ds(koff, NSA_KB), :]
            vT_blk = vT_ref[0, 0, :, pl.ds(koff, NSA_KB)]
            tile = jnp.minimum(qb - kb, n_tiles - 1)
            if use_sel:
                r0 = mt_ref[pl.ds(kb * SEL_PER_KB, 1), :]
                r1 = mt_ref[pl.ds(kb * SEL_PER_KB + 1, 1), :]
                selm = jnp.where(sub < SEL_BLOCK, r0, r1) > 0.5
            for h in range(hpg):
                s = jnp.dot(k_blk, qT_ref[0, 0, h], preferred_element_type=jnp.float32) + bias_ref[0, h, tile]
                if use_sel:
                    s = jnp.where(selm, s, NEG)
                m_old = m_ref[h]
                m_new = jnp.maximum(m_old, jnp.max(s, axis=0, keepdims=True))
                alpha = jnp.exp(m_old - m_new)
                p = jnp.exp(s - m_new)
                l_ref[h] = alpha * l_ref[h] + jnp.sum(p, axis=0, keepdims=True)
                acc_ref[h] = alpha * acc_ref[h] + jnp.dot(vT_blk, p.astype(jnp.bfloat16),
                                                          preferred_element_type=jnp.float32)
                m_ref[h] = m_new
            return carry

        lax.fori_loop(lo, qb + 1, body, 0)

    attend(ksel_ref, vselT_ref, bsel_ref, bsel_ref.shape[2], 0, True)
    for h in range(hpg):
        o_ref[0, 0, h] += gates[1, h:h + 1, :] * (acc_ref[h] / l_ref[h])

    attend(kwin_ref, vwinT_ref, bwin_ref, bwin_ref.shape[2], jnp.maximum(qb - (WIN_TILES - 1), 0), False)
    for h in range(hpg):
        o_ref[0, 0, h] += gates[2, h:h + 1, :] * (acc_ref[h] / l_ref[h])


def _bias_lookup(rel_bias, dist):
    onehot = jax.nn.one_hot(_t5_bucket(dist), REL_BUCKETS, dtype=jnp.float32)
    return jnp.einsum('...k,kh->...h', onehot, rel_bias, precision=lax.Precision.HIGHEST)


def _bias_tables(rel_bias, T):
    G, HPG = NSA_KV_HEADS, NSA_HPG
    nc = (T - CMP_BLOCK) // CMP_STRIDE + 1
    ncp = T // CMP_STRIDE

    def lookup(dist, valid):
        b = _bias_lookup(rel_bias, dist)
        b = jnp.where(valid[..., None], b, NEG)
        return jnp.moveaxis(b, -1, 0).reshape((G, HPG) + dist.shape)

    c = jnp.arange(NSA_KB)[:, None]
    i = jnp.arange(NSA_TQ)[None, :]
    sel_d = jnp.stack([dlt + i - c for dlt in (0, NSA_KB, 2 * NSA_KB)])
    bsel = lookup(sel_d, sel_d >= 0)
    win_d = jnp.stack([dlt * NSA_KB + i - c for dlt in range(WIN_TILES)])
    bwin = lookup(win_d, (win_d >= 0) & (win_d <= WINDOW))
    n = jnp.arange(ncp)[:, None]
    t = jnp.arange(T)[None, :]
    cmp_d = t - (n * CMP_STRIDE + CMP_BLOCK - 1)
    bcmp = lookup(cmp_d, (cmp_d >= 0) & (n < nc))
    ns = -(-T // SEL_BLOCK)
    covT = _block_cover(nc, ns).T
    covT = jnp.pad(covT, ((0, 0), (0, ncp - nc))).astype(jnp.bfloat16)
    return bsel, bwin, bcmp, covT


def _nsa_prompt(q, k_sel, v_sel, k_win, v_win, kc, vc, ngt, tables):
    B, T, G, HPG, HD = q.shape
    bsel, bwin, bcmp, covT = tables
    ns, ncp = covT.shape
    assert kc.shape == (B, G, ncp, HD), kc.shape
    bf = jnp.bfloat16
    qT = q.astype(bf).transpose(0, 2, 3, 4, 1)
    ksel = k_sel.astype(bf).transpose(0, 2, 1, 3)
    vselT = v_sel.astype(bf).transpose(0, 2, 3, 1)
    kwin = k_win.astype(bf).transpose(0, 2, 1, 3)
    vwinT = v_win.astype(bf).transpose(0, 2, 3, 1)
    kcp = kc.astype(bf)
    vcT = vc.astype(bf).transpose(0, 1, 3, 2)
    gT = ngt.reshape(B, T, 3, G, HPG).transpose(0, 3, 2, 4, 1)
    tq = NSA_TQ
    full = lambda b, g, i: (b, g, 0, 0)
    oT = pl.pallas_call(
        functools.partial(_nsa_prompt_kernel, n_top=min(N_SELECT, ns)),
        grid=(B, G, T // tq),
        in_specs=[
            pl.BlockSpec((1, 1, HPG, HD, tq), lambda b, g, i: (b, g, 0, 0, i)),
            pl.BlockSpec((1, 1, T, HD), full),
            pl.BlockSpec((1, 1, HD, T), full),
            pl.BlockSpec((1, 1, T, HD), full),
            pl.BlockSpec((1, 1, HD, T), full),
            pl.BlockSpec((1, 1, ncp, HD), full),
            pl.BlockSpec((1, 1, HD, ncp), full),
            pl.BlockSpec((ns, ncp), lambda b, g, i: (0, 0)),
            pl.BlockSpec((1, HPG, ncp, tq), lambda b, g, i: (g, 0, 0, i)),
            pl.BlockSpec((1, HPG) + bsel.shape[2:], lambda b, g, i: (g, 0, 0, 0, 0)),
            pl.BlockSpec((1, HPG) + bwin.shape[2:], lambda b, g, i: (g, 0, 0, 0, 0)),
            pl.BlockSpec((1, 1, 3, HPG, tq), lambda b, g, i: (b, g, 0, 0, i)),
        ],
        out_specs=pl.BlockSpec((1, 1, HPG, HD, tq), lambda b, g, i: (b, g, 0, 0, i)),
        out_shape=jax.ShapeDtypeStruct((B, G, HPG, HD, T), jnp.float32),
        scratch_shapes=[pltpu.VMEM((ns, tq), jnp.float32),
                        pltpu.VMEM((HPG, 1, tq), jnp.float32),
                        pltpu.VMEM((HPG, 1, tq), jnp.float32),
                        pltpu.VMEM((HPG, HD, tq), jnp.float32)],
        compiler_params=pltpu.CompilerParams(
            dimension_semantics=("arbitrary", "arbitrary", "arbitrary"),
            vmem_limit_bytes=V7X_VMEM_LIMIT_BYTES),
        name="nsa_prompt",
    )(qT, ksel, vselT, kwin, vwinT, kcp, vcT, covT, bcmp, bsel, bwin, gT)
    return oT.transpose(0, 4, 1, 2, 3).reshape(B, T, G * HPG * HD)


LIN_TB = 256


def _logsigmoid(x):
    return jnp.minimum(x, 0.0) - jnp.log(1.0 + jnp.exp(-jnp.abs(x)))


def _lin_attn_kernel(*refs, mode, chunk):
    if mode == "gla":
        (q_ref, k_ref, v_ref, r_ref, a_ref, wa_ref, ba_ref, gn_ref, o_ref, st_ref,
         q_s, k_s, g_s, sT_ref) = refs
    else:
        (q_ref, k_ref, v_ref, r_ref, llb_ref, l1m_ref, gn_ref, o_ref, st_ref,
         q_s, k_s, g_s, sT_ref) = refs
    tb, dk = q_s.shape
    C = chunk
    t = pl.program_id(2)

    if mode == "gla":
        q_s[...] = q_ref[0] * (dk ** -0.5)
        k_s[...] = k_ref[0]
        pre = jnp.dot(a_ref[0].astype(jnp.bfloat16), wa_ref[...], preferred_element_type=jnp.float32) + ba_ref[...]
        g_s[...] = _logsigmoid(pre) / GLA_TAU
    else:
        x = q_ref[0]
        q_s[...] = x * jax.nn.sigmoid(x)
        u = llb_ref[...]
        w = l1m_ref[...] + _logsigmoid(k_ref[0])
        lf = jnp.maximum(u, w) + jnp.log(1.0 + jnp.exp(-jnp.abs(u - w)))
        g_s[...] = lf
        k_s[...] = 1.0 - jnp.exp(lf)

    @pl.when(t == 0)
    def _():
        sT_ref[...] = jnp.zeros(sT_ref.shape, jnp.float32)

    rr = lax.broadcasted_iota(jnp.int32, (C, C), 0)
    cc = lax.broadcasted_iota(jnp.int32, (C, C), 1)
    tril = (rr >= cc).astype(jnp.bfloat16)
    row8 = lax.broadcasted_iota(jnp.int32, (SUB, dk), 0)
    cc8 = lax.broadcasted_iota(jnp.int32, (SUB, C), 1)
    gn = gn_ref[...]
    bf = jnp.bfloat16

    def chunk_body(c, carry):
        r0 = pl.multiple_of(c * C, C)
        qc = q_s[pl.ds(r0, C), :]
        kc = k_s[pl.ds(r0, C), :]
        vc = v_ref[0, pl.ds(r0, C), :].astype(bf)
        b = sum(jnp.dot(tril, part, preferred_element_type=jnp.float32) for part in _split3_bf16(g_s[pl.ds(r0, C), :]))
        bl = b[C - 1:C, :]

        nt = C // SUB
        b_t = [b[i * SUB:(i + 1) * SUB] for i in range(nt)]
        q_t = [qc[i * SUB:(i + 1) * SUB] for i in range(nt)]
        att_t = [jnp.zeros((SUB, C), jnp.float32) for _ in range(nt)]
        for s in range(C):
            bs = b[s:s + 1, :]
            ks = kc[s:s + 1, :]
            for i in range(s // SUB, nt):
                d = b_t[i] - bs
                if i == s // SUB and s % SUB:
                    d = jnp.where(row8 >= s % SUB, d, NEG)
                col = jnp.sum(q_t[i] * ks * jnp.exp(d), axis=-1, keepdims=True)
                att_t[i] = jnp.where(cc8 == s, col, att_t[i])
        att = jnp.concatenate(att_t, axis=0)
        sT = sT_ref[...]
        o = (lax.dot_general((qc * jnp.exp(b)).astype(bf), sT.astype(bf), (((1,), (1,)), ((), ())),
                             preferred_element_type=jnp.float32)
             + jnp.dot(att.astype(bf), vc, preferred_element_type=jnp.float32))
        sT_ref[...] = sT * jnp.exp(bl) + lax.dot_general(
            vc, (kc * jnp.exp(bl - b)).astype(bf), (((0,), (0,)), ((), ())), preferred_element_type=jnp.float32)
        y = o * lax.rsqrt(jnp.mean(o * o, axis=-1, keepdims=True) + EPS) * gn
        r = r_ref[0, pl.ds(r0, C), :]
        gate = jax.nn.sigmoid(r)
        if mode == "gla":
            gate = r * gate
        o_ref[0, pl.ds(r0, C), :] = (y * gate).astype(o_ref.dtype)
        return carry

    lax.fori_loop(0, tb // C, chunk_body, 0)

    @pl.when(t == pl.num_programs(2) - 1)
    def _():
        st_ref[0, 0] = sT_ref[...]


def _lin_attn_prompt(z, B, T, mode, params):
    tb = LIN_TB
    z3 = z.reshape(B, T, z.shape[-1])
    if mode == "gla":
        H, dk, dv = GLA_HEADS, GLA_DK, GLA_DV
        wa, ba, gn = params
        in_specs = [
            pl.BlockSpec((1, tb, dk), lambda b, h, t: (b, t, h)),
            pl.BlockSpec((1, tb, dk), lambda b, h, t: (b, t, H + h)),
            pl.BlockSpec((1, tb, dv), lambda b, h, t: (b, t, (2 * H * dk) // dv + h)),
            pl.BlockSpec((1, tb, dv), lambda b, h, t: (b, t, (2 * H * dk + H * dv) // dv + h)),
            pl.BlockSpec((1, tb, LANE), lambda b, h, t: (b, t, (2 * H * dk + 2 * H * dv) // LANE)),
            pl.BlockSpec((LANE, dk), lambda b, h, t: (0, h)),
            pl.BlockSpec((1, dk), lambda b, h, t: (0, h)),
            pl.BlockSpec((1, dv), lambda b, h, t: (0, 0)),
        ]
        args = [z3, z3, z3, z3, z3, wa, ba, gn]
    else:
        H, dk, dv = HGRN_HEADS, HGRN_EXPAND, HGRN_EXPAND
        llb, l1m, gn = params
        in_specs = [
            pl.BlockSpec((1, tb, dk), lambda b, h, t: (b, t, h)),
            pl.BlockSpec((1, tb, dk), lambda b, h, t: (b, t, H + h)),
            pl.BlockSpec((1, tb, dv), lambda b, h, t: (b, t, 2 * H + h)),
            pl.BlockSpec((1, tb, dv), lambda b, h, t: (b, t, 3 * H + h)),
            pl.BlockSpec((1, dk), lambda b, h, t: (0, h)),
            pl.BlockSpec((1, dk), lambda b, h, t: (0, h)),
            pl.BlockSpec((1, dv), lambda b, h, t: (0, 0)),
        ]
        args = [z3, z3, z3, z3, llb, l1m, gn]
    o, sT = pl.pallas_call(
        functools.partial(_lin_attn_kernel, mode=mode, chunk=CHUNK),
        grid=(B, H, T // tb),
        in_specs=in_specs,
        out_specs=[pl.BlockSpec((1, tb, dv), lambda b, h, t: (b, t, h)),
                   pl.BlockSpec((1, 1, dv, dk), lambda b, h, t: (b, h, 0, 0))],
        out_shape=[jax.ShapeDtypeStruct((B, T, H * dv), jnp.bfloat16),
                   jax.ShapeDtypeStruct((B, H, dv, dk), jnp.float32)],
        scratch_shapes=[pltpu.VMEM((tb, dk), jnp.float32), pltpu.VMEM((tb, dk), jnp.float32),
                        pltpu.VMEM((tb, dk), jnp.float32), pltpu.VMEM((dv, dk), jnp.float32)],
        compiler_params=pltpu.CompilerParams(
            dimension_semantics=("arbitrary", "arbitrary", "arbitrary"),
            vmem_limit_bytes=V7X_VMEM_LIMIT_BYTES),
        name="lin_attn_" + mode,
    )(*args)
    return o.reshape(B * T, H * dv), sT.transpose(0, 1, 3, 2)


def _mlstm_kernel(xq_ref, xk_ref, v_ref, og_ref, gi_ref, wq_ref, wk_ref, bq_ref, bk_ref, bibf_ref, gn_ref,
                  o_ref, c_out, n_out, m_out, xe_q, xe_k, q_s, k_s, c_s, n_s, m_s, *, chunk):
    tb, dk = q_s.shape
    C = chunk
    t = pl.program_id(2)
    h = pl.program_id(1)
    W = MLSTM_CONV

    @pl.when(t == 0)
    def _():
        xe_q[0:SUB, :] = jnp.zeros((SUB, dk), jnp.float32)
        xe_k[0:SUB, :] = jnp.zeros((SUB, dk), jnp.float32)
        c_s[...] = jnp.zeros(c_s.shape, jnp.float32)
        n_s[...] = jnp.zeros(n_s.shape, jnp.float32)
        m_s[...] = jnp.zeros(m_s.shape, jnp.float32)

    for xe, x_ref, w_ref, b_ref, dst, scale in ((xe_q, xq_ref, wq_ref, bq_ref, q_s, 1.0),
                                                (xe_k, xk_ref, wk_ref, bk_ref, k_s, dk ** -0.5)):
        xe[SUB:SUB + tb, :] = x_ref[0]
        y = b_ref[...]
        for j in range(W):
            y = y + xe[SUB - (W - 1) + j:SUB - (W - 1) + j + tb, :] * w_ref[j:j + 1, :]
        dst[...] = (y * jax.nn.sigmoid(y)) * scale
        xe[SUB - (W - 1):SUB, :] = xe[SUB + tb - (W - 1):SUB + tb, :]

    lane = lax.broadcasted_iota(jnp.int32, (tb, LANE), 1)
    g = gi_ref[0] + bibf_ref[...]
    i_all = jnp.sum(jnp.where(lane == h, g, 0.0), axis=1, keepdims=True)
    f_all = _logsigmoid(jnp.sum(jnp.where(lane == MLSTM_HEADS + h, g, 0.0), axis=1, keepdims=True))
    rr = lax.broadcasted_iota(jnp.int32, (C, C), 0)
    cc = lax.broadcasted_iota(jnp.int32, (C, C), 1)
    causal = rr >= cc
    eye = rr == cc
    gn = gn_ref[...]
    bf = jnp.bfloat16

    def to_row(col):
        return jnp.sum(jnp.where(eye, col, 0.0), axis=0, keepdims=True)

    m = m_s[0:1, 0:1]
    for c in range(tb // C):
        r0 = c * C
        qc = q_s[r0:r0 + C, :]
        kc = k_s[r0:r0 + C, :]
        vc = v_ref[0, r0:r0 + C, :]
        i_col = i_all[r0:r0 + C]
        f_col = f_all[r0:r0 + C]
        f_row = to_row(f_col)
        F_col = jnp.sum(jnp.where(causal, f_row, 0.0), axis=1, keepdims=True)
        F_row = to_row(F_col)
        i_row = to_row(i_col)
        logw = jnp.where(causal, F_col - F_row + i_row, NEG)
        from_state = F_col + m
        m_hat = jnp.maximum(from_state, jnp.max(logw, axis=1, keepdims=True))
        qk = lax.dot_general(qc.astype(bf), kc.astype(bf), (((1,), (1,)), ((), ())),
                             preferred_element_type=jnp.float32)
        w = jnp.exp(logw - m_hat) * qk
        ws = jnp.exp(from_state - m_hat)
        cs = c_s[...]
        ns = n_s[0:1, :]
        num = (ws * lax.dot_general(qc.astype(bf), cs.astype(bf), (((1,), (1,)), ((), ())),
                                    preferred_element_type=jnp.float32)
               + jnp.dot(w.astype(bf), vc.astype(bf), preferred_element_type=jnp.float32))
        den = ws * jnp.sum(qc * ns, axis=1, keepdims=True) + jnp.sum(w, axis=1, keepdims=True)
        hh = num / jnp.maximum(jnp.abs(den), jnp.exp(-m_hat))
        m_new = m_hat[C - 1:C, :]
        F_last = F_col[C - 1:C, :]
        ds = jnp.exp(F_last - F_col + i_col - m_new)
        dst = jnp.exp(F_last + m - m_new)
        c_s[...] = dst * cs + lax.dot_general((vc * ds).astype(bf), kc.astype(bf), (((0,), (0,)), ((), ())),
                                              preferred_element_type=jnp.float32)
        n_s[...] = jnp.broadcast_to(dst * ns + jnp.sum(ds * kc, axis=0, keepdims=True), n_s.shape)
        m = m_new
        y = hh * lax.rsqrt(jnp.mean(hh * hh, axis=-1, keepdims=True) + EPS) * gn
        o_ref[0, r0:r0 + C, :] = (jax.nn.sigmoid(og_ref[0, r0:r0 + C, :]) * y).astype(o_ref.dtype)
    m_s[...] = jnp.broadcast_to(m, m_s.shape)

    @pl.when(t == pl.num_programs(2) - 1)
    def _():
        c_out[0, 0] = c_s[...]
        n_out[0, 0] = n_s[...]
        m_out[0, 0] = m_s[...]


def _mlstm_prompt(z, B, T, p):
    H, dk, dv = MLSTM_HEADS, MLSTM_DQK, MLSTM_DV
    tb = LIN_TB
    z3 = z.reshape(B, T, z.shape[-1])
    nqk = H * dk
    bibf = jnp.pad(jnp.concatenate([p['m_bi'], p['m_bf']]), (0, LANE - 2 * H))[None]
    wconv, bconv = p['m_wconv'], p['m_bconv'][None]
    o, c_f, n_f, m_f = pl.pallas_call(
        functools.partial(_mlstm_kernel, chunk=CHUNK),
        grid=(B, H, T // tb),
        in_specs=[
            pl.BlockSpec((1, tb, dk), lambda b, h, t: (b, t, h)),
            pl.BlockSpec((1, tb, dk), lambda b, h, t: (b, t, H + h)),
            pl.BlockSpec((1, tb, dv), lambda b, h, t: (b, t, (2 * nqk) // dv + h)),
            pl.BlockSpec((1, tb, dv), lambda b, h, t: (b, t, (2 * nqk + H * dv) // dv + h)),
            pl.BlockSpec((1, tb, LANE), lambda b, h, t: (b, t, (2 * nqk + 2 * H * dv) // LANE)),
            pl.BlockSpec((MLSTM_CONV, dk), lambda b, h, t: (0, h)),
            pl.BlockSpec((MLSTM_CONV, dk), lambda b, h, t: (0, H + h)),
            pl.BlockSpec((1, dk), lambda b, h, t: (0, h)),
            pl.BlockSpec((1, dk), lambda b, h, t: (0, H + h)),
            pl.BlockSpec((1, LANE), lambda b, h, t: (0, 0)),
            pl.BlockSpec((1, dv), lambda b, h, t: (0, 0)),
        ],
        out_specs=[pl.BlockSpec((1, tb, dv), lambda b, h, t: (b, t, h)),
                   pl.BlockSpec((1, 1, dv, dk), lambda b, h, t: (b, h, 0, 0)),
                   pl.BlockSpec((1, 1, SUB, dk), lambda b, h, t: (b, h, 0, 0)),
                   pl.BlockSpec((1, 1, SUB, LANE), lambda b, h, t: (b, h, 0, 0))],
        out_shape=[jax.ShapeDtypeStruct((B, T, H * dv), jnp.bfloat16),
                   jax.ShapeDtypeStruct((B, H, dv, dk), jnp.float32),
                   jax.ShapeDtypeStruct((B, H, SUB, dk), jnp.float32),
                   jax.ShapeDtypeStruct((B, H, SUB, LANE), jnp.float32)],
        scratch_shapes=[pltpu.VMEM((SUB + tb, dk), jnp.float32), pltpu.VMEM((SUB + tb, dk), jnp.float32),
                        pltpu.VMEM((tb, dk), jnp.float32), pltpu.VMEM((tb, dk), jnp.float32),
                        pltpu.VMEM((dv, dk), jnp.float32), pltpu.VMEM((SUB, dk), jnp.float32),
                        pltpu.VMEM((SUB, LANE), jnp.float32)],
        compiler_params=_cparams(3),
        name="mlstm_prompt",
    )(z3, z3, z3, z3, z3, wconv, wconv, bconv, bconv, bibf, p['m_gn'][None])
    return o.reshape(B * T, H * dv), c_f, n_f[:, :, 0, :], m_f[:, :, 0, 0]


CMP_PAGES = 16
CMP_PER_PAGE = PAGE_SIZE // CMP_STRIDE
HEAD_ROWS = SUB


KV_ROWS = 4 * NSA_KV_HEADS


def _compress_blocks(load, kind, w1_ref, w2_ref, pe_ref, gk_ref, nblk):
    acc = jnp.zeros((nblk, HEAD_DIM), jnp.float32)
    for j in range(0, CMP_BLOCK, 2):
        xa = load(j) + pe_ref[kind, j:j + 1, :]
        xb = load(j + 1) + pe_ref[kind, j + 1:j + 2, :]
        x2 = jnp.concatenate([xa, xb], axis=1).astype(jnp.bfloat16)
        acc = acc + jnp.dot(x2, w1_ref[kind, j * HEAD_DIM:(j + 2) * HEAD_DIM, :], preferred_element_type=jnp.float32)
    y = jnp.dot((acc * jax.nn.sigmoid(acc)).astype(jnp.bfloat16), w2_ref[kind], preferred_element_type=jnp.float32)
    if kind == 0:
        y = y * lax.rsqrt(jnp.mean(y * y, axis=-1, keepdims=True) + EPS) * gk_ref[...]
    return y


def _cmp_sample_kernel(pt_ref, *refs):
    del pt_ref
    pages = refs[:CMP_PAGES + 1]
    w1_ref, w2_ref, pe_ref, gk_ref, o_ref, xs_ref = refs[CMP_PAGES + 1:]
    rows = PAGE_SIZE * KV_ROWS
    for u in range(CMP_PAGES + 1):
        xs_ref[u * rows:(u + 1) * rows, :] = pages[u][...]
    nblk = CMP_PAGES * CMP_PER_PAGE
    for c in range(2 * NSA_KV_HEADS):
        load = functools.partial(lambda j, c: xs_ref[pl.ds(KV_ROWS * j + c, nblk, stride=KV_ROWS * CMP_STRIDE), :], c=c)
        o_ref[0, c] = _compress_blocks(load, c // NSA_KV_HEADS, w1_ref, w2_ref, pe_ref, gk_ref, nblk)


def _cmp_prompt_kernel(*refs):
    srcs = refs[:2 * NSA_KV_HEADS]
    w1_ref, w2_ref, pe_ref, gk_ref, o_ref, xs_ref = refs[2 * NSA_KV_HEADS:]
    T = srcs[0].shape[1]
    nblk = T // CMP_STRIDE
    xs_ref[T:, :] = jnp.zeros((xs_ref.shape[0] - T, HEAD_DIM), jnp.float32)
    load = lambda j: xs_ref[pl.ds(j, nblk, stride=CMP_STRIDE), :]
    for c in range(2 * NSA_KV_HEADS):
        xs_ref[0:T, :] = srcs[c][0]
        o_ref[0, c] = _compress_blocks(load, c // NSA_KV_HEADS, w1_ref, w2_ref, pe_ref, gk_ref, nblk)


def _cmp_prompt(z3, w1, w2, pe, gk):
    B, T, _ = z3.shape
    G = NSA_KV_HEADS
    first = NSA_HEADS
    full = lambda *shape: pl.BlockSpec(shape, lambda b: (0,) * len(shape))
    return pl.pallas_call(
        _cmp_prompt_kernel,
        grid=(B,),
        in_specs=[pl.BlockSpec((1, T, HEAD_DIM), functools.partial(lambda b, c: (b, 0, first + c), c=c))
                  for c in range(2 * G)] + [
            full(2, CMP_BLOCK * HEAD_DIM, HEAD_DIM), full(2, HEAD_DIM, HEAD_DIM), full(2, CMP_BLOCK, HEAD_DIM),
            full(1, HEAD_DIM)],
        out_specs=pl.BlockSpec((1, 2 * G, T // CMP_STRIDE, HEAD_DIM), lambda b: (b, 0, 0, 0)),
        out_shape=jax.ShapeDtypeStruct((B, 2 * G, T // CMP_STRIDE, HEAD_DIM), jnp.float32),
        scratch_shapes=[pltpu.VMEM((T + PAGE_SIZE, HEAD_DIM), jnp.float32)],
        compiler_params=_cparams(1),
        name="nsa_cmp_prompt",
    )(*([z3] * (2 * G)), w1.astype(jnp.bfloat16), w2.astype(jnp.bfloat16), pe, gk[None])


def _cmp_sample(cache2d, page_table, layer_off, w1, w2, pe, gk):
    B, n_pages = page_table.shape
    G = NSA_KV_HEADS
    assert n_pages % CMP_PAGES == 0
    nblk = CMP_PAGES * CMP_PER_PAGE
    rows = PAGE_SIZE * KV_ROWS

    def page_spec(u):
        def imap(b, grp, pt):
            page = jnp.minimum(grp * CMP_PAGES + u, n_pages - 1)
            return (layer_off + pt[b * n_pages + page], 0)
        return pl.BlockSpec((rows, HEAD_DIM), imap)

    full = lambda *shape: pl.BlockSpec(shape, lambda b, grp, pt: (0,) * len(shape))
    return pl.pallas_call(
        _cmp_sample_kernel,
        grid_spec=pltpu.PrefetchScalarGridSpec(
            num_scalar_prefetch=1,
            grid=(B, n_pages // CMP_PAGES),
            in_specs=[page_spec(u) for u in range(CMP_PAGES + 1)] + [
                full(2, CMP_BLOCK * HEAD_DIM, HEAD_DIM), full(2, HEAD_DIM, HEAD_DIM),
                full(2, CMP_BLOCK, HEAD_DIM), full(1, HEAD_DIM)],
            out_specs=pl.BlockSpec((1, 2 * G, nblk, HEAD_DIM), lambda b, grp, pt: (b, 0, grp, 0)),
            scratch_shapes=[pltpu.VMEM(((CMP_PAGES + 1) * rows, HEAD_DIM), jnp.float32)],
        ),
        out_shape=jax.ShapeDtypeStruct((B, 2 * G, n_pages * CMP_PER_PAGE, HEAD_DIM), jnp.float32),
        compiler_params=_cparams(2),
        name="nsa_cmp_sample",
    )(page_table.reshape(-1), *([cache2d] * (CMP_PAGES + 1)),
      w1.astype(jnp.bfloat16), w2.astype(jnp.bfloat16), pe, gk[None])


def _topk_sample_kernel(q_ref, kc_ref, vc_ref, bias_ref, cov_ref, ocmp_ref, top_ref, *, cur, n_top):
    nsp = cov_ref.shape[1]
    q = q_ref[0, 0]
    bias = bias_ref[0]
    s = lax.dot_general(q, kc_ref[0, 0].astype(jnp.bfloat16), (((1,), (1,)), ((), ())),
                        preferred_element_type=jnp.float32) + bias
    e = jnp.exp(s - jnp.max(s, axis=-1, keepdims=True))
    p = jnp.where(bias > 0.5 * NEG, e / jnp.sum(e, axis=-1, keepdims=True), 0.0)
    ocmp_ref[0, 0] = jnp.dot(p.astype(jnp.bfloat16), vc_ref[0, 0].astype(jnp.bfloat16),
                             preferred_element_type=jnp.float32)
    head = lax.broadcasted_iota(jnp.int32, p.shape, 0)
    p_sum = jnp.sum(jnp.where(head < NSA_HPG, p, 0.0), axis=0, keepdims=True)
    p_sum = jnp.broadcast_to(p_sum, (SUB, p_sum.shape[1]))
    cov = cov_ref[...]
    imp = sum(jnp.dot(part, cov, preferred_element_type=jnp.float32) for part in _split3_bf16(p_sum))[0:1]
    blk = lax.broadcasted_iota(jnp.int32, (1, nsp), 1)
    forced = (blk == 0) | (blk == cur) | (blk == cur - 1)
    score = jnp.where(forced, BIG, jnp.where(blk <= cur, imp, -BIG))
    score = jnp.where(blk <= cur, score, -2.0 * BIG)
    ii = lax.broadcasted_iota(jnp.int32, (nsp, nsp), 0)
    jj = lax.broadcasted_iota(jnp.int32, (nsp, nsp), 1)
    eye = ii == jj

    def to_col(row):
        return jnp.sum(jnp.where(eye, row, 0.0), axis=1, keepdims=True)

    col = to_col(score)
    beats = (col > score) | ((col == score) & (ii < jj))
    rank = jnp.sum(beats.astype(jnp.float32), axis=0, keepdims=True)
    sel = jnp.where((rank < n_top) & (blk < cur), 1.0, 0.0)
    pos = jnp.sum(jnp.where(ii < jj, to_col(sel), 0.0), axis=0, keepdims=True)
    lane = lax.broadcasted_iota(jnp.int32, (1, LANE), 1)
    out = jnp.zeros((1, LANE), jnp.float32)
    blk_f = blk.astype(jnp.float32)
    for k in range(n_top - 1):
        idx = jnp.sum(jnp.where((sel > 0.5) & (pos == k), blk_f, 0.0), axis=1, keepdims=True)
        out = jnp.where(lane == k, idx, out)
    top_ref[0, 0] = jnp.broadcast_to(out, (SUB, LANE)).astype(jnp.int32)


def _topk_sample(q8, kcvc, bias_c, cov, cur):
    B, G = q8.shape[:2]
    nc = kcvc.shape[2]
    nsp = cov.shape[1]
    return pl.pallas_call(
        functools.partial(_topk_sample_kernel, cur=cur, n_top=N_SELECT),
        grid=(B, G),
        in_specs=[pl.BlockSpec((1, 1, HEAD_ROWS, HEAD_DIM), lambda b, g: (b, g, 0, 0)),
                  pl.BlockSpec((1, 1, nc, HEAD_DIM), lambda b, g: (b, g, 0, 0)),
                  pl.BlockSpec((1, 1, nc, HEAD_DIM), lambda b, g: (b, G + g, 0, 0)),
                  pl.BlockSpec((1, HEAD_ROWS, nc), lambda b, g: (g, 0, 0)),
                  pl.BlockSpec((nc, nsp), lambda b, g: (0, 0))],
        out_specs=[pl.BlockSpec((1, 1, HEAD_ROWS, HEAD_DIM), lambda b, g: (b, g, 0, 0)),
                   pl.BlockSpec((1, 1, SUB, LANE), lambda b, g: (b, g, 0, 0))],
        out_shape=[jax.ShapeDtypeStruct((B, G, HEAD_ROWS, HEAD_DIM), jnp.float32),
                   jax.ShapeDtypeStruct((B, G, SUB, LANE), jnp.int32)],
        compiler_params=_cparams(2),
        name="nsa_topk_sample",
    )(q8, kcvc, kcvc, bias_c, cov)


def _sel_sample_kernel(pt_ref, top_ref, q_ref, blk_ref, new_ref, kw_ref, vw_ref, ocmp_ref,
                       bsel_ref, bwin_ref, b0_ref, gate_ref, o_ref, m_ref, l_ref, acc_ref, *, cur):
    del pt_ref
    b, g, k = pl.program_id(0), pl.program_id(1), pl.program_id(2)
    nk = pl.num_programs(2)
    q = q_ref[0, 0]
    qf = q.astype(jnp.float32)
    b0 = b0_ref[0]
    bf = jnp.bfloat16

    @pl.when(k == 0)
    def _():
        m_ref[...] = jnp.sum(qf * new_ref[0, 0, 0:1, :], axis=-1, keepdims=True) + b0
        l_ref[...] = jnp.ones(l_ref.shape, jnp.float32)
        acc_ref[...] = jnp.broadcast_to(new_ref[0, 0, 1:2, :], acc_ref.shape)

    j = top_ref[(b * NSA_KV_HEADS + g) * LANE + k]
    tile = jnp.minimum(cur - j, bsel_ref.shape[1]) - 1
    def group_rows(kind):
        out = None
        for gi in range(NSA_KV_HEADS):
            r = blk_ref[pl.ds(kind * NSA_KV_HEADS + gi, SEL_BLOCK, stride=KV_ROWS), :]
            out = r if out is None else jnp.where(g == gi, r, out)
        return out.astype(bf)

    s = lax.dot_general(q, group_rows(2), (((1,), (1,)), ((), ())),
                        preferred_element_type=jnp.float32) + bsel_ref[0, tile]
    m_old = m_ref[...]
    m_new = jnp.maximum(m_old, jnp.max(s, axis=-1, keepdims=True))
    alpha = jnp.exp(m_old - m_new)
    p = jnp.exp(s - m_new)
    l_ref[...] = alpha * l_ref[...] + jnp.sum(p, axis=-1, keepdims=True)
    acc_ref[...] = alpha * acc_ref[...] + jnp.dot(p.astype(bf), group_rows(3),
                                                  preferred_element_type=jnp.float32)
    m_ref[...] = m_new

    @pl.when(k == nk - 1)
    def _():
        gates = jax.nn.sigmoid(gate_ref[0, 0])
        o_sel = acc_ref[...] / l_ref[...]
        sw = lax.dot_general(q, kw_ref[0].astype(bf), (((1,), (1,)), ((), ())),
                             preferred_element_type=jnp.float32) + bwin_ref[0]
        sn = jnp.sum(qf * new_ref[0, 0, 2:3, :], axis=-1, keepdims=True) + b0
        mw = jnp.maximum(jnp.max(sw, axis=-1, keepdims=True), sn)
        pw = jnp.exp(sw - mw)
        pn = jnp.exp(sn - mw)
        o_w = (jnp.dot(pw.astype(bf), vw_ref[0].astype(bf), preferred_element_type=jnp.float32)
               + pn * new_ref[0, 0, 3:4, :]) / (jnp.sum(pw, axis=-1, keepdims=True) + pn)
        o_ref[0, 0] = gates[0] * ocmp_ref[0, 0] + gates[1] * o_sel + gates[2] * o_w


def _sel_sample(cache2d, page_table, layer_off, top, q8, new_rows, win_rows, ocmp, bsel, bwin, b0, gate8, cur):
    B, n_pages = page_table.shape
    G = NSA_KV_HEADS
    per_page = PAGE_SIZE // SEL_BLOCK
    wb = win_rows.shape[1]

    def sel_map(b, g, k, pt, tp):
        j = tp[(b * G + g) * LANE + k]
        page = layer_off + pt[b * n_pages + j // per_page]
        return (page * per_page + j % per_page, 0)

    sel_spec = pl.BlockSpec((SEL_BLOCK * KV_ROWS, HEAD_DIM), sel_map)

    bg = lambda b, g, k, pt, tp: (b, g, 0, 0)
    return pl.pallas_call(
        functools.partial(_sel_sample_kernel, cur=cur),
        grid_spec=pltpu.PrefetchScalarGridSpec(
            num_scalar_prefetch=2,
            grid=(B, G, N_SELECT - 1),
            in_specs=[
                pl.BlockSpec((1, 1, HEAD_ROWS, HEAD_DIM), bg),
                sel_spec,
                pl.BlockSpec((1, 1, SUB, HEAD_DIM), bg),
                pl.BlockSpec((1, wb, HEAD_DIM), lambda b, g, k, pt, tp: (b, 0, g)),
                pl.BlockSpec((1, wb, HEAD_DIM), lambda b, g, k, pt, tp: (b, 0, G + g)),
                pl.BlockSpec((1, 1, HEAD_ROWS, HEAD_DIM), bg),
                pl.BlockSpec((1,) + bsel.shape[1:], lambda b, g, k, pt, tp: (g, 0, 0, 0)),
                pl.BlockSpec((1,) + bwin.shape[1:], lambda b, g, k, pt, tp: (g, 0, 0)),
                pl.BlockSpec((1,) + b0.shape[1:], lambda b, g, k, pt, tp: (g, 0, 0)),
                pl.BlockSpec((1, 1, 3, HEAD_ROWS, HEAD_DIM), lambda b, g, k, pt, tp: (b, g, 0, 0, 0)),
            ],
            out_specs=pl.BlockSpec((1, 1, HEAD_ROWS, HEAD_DIM), bg),
            scratch_shapes=[pltpu.VMEM((HEAD_ROWS, 1), jnp.float32), pltpu.VMEM((HEAD_ROWS, 1), jnp.float32),
                            pltpu.VMEM((HEAD_ROWS, HEAD_DIM), jnp.float32)],
        ),
        out_shape=jax.ShapeDtypeStruct((B, G, HEAD_ROWS, HEAD_DIM), jnp.float32),
        compiler_params=_cparams(3),
        name="nsa_sel_sample",
    )(page_table.reshape(-1), top.reshape(-1), q8, cache2d, new_rows, win_rows, win_rows,
      ocmp, bsel, bwin, b0, gate8)


def _nsa_sample(z_nsa, cache_l, layer_off, page_table, win_l, p, rel_bias):
    B, n_pages = page_table.shape
    G, HPG, HD = NSA_KV_HEADS, NSA_HPG, HEAD_DIM
    past_len = n_pages * PAGE_SIZE
    cur = past_len // SEL_BLOCK
    nq = z_nsa[:, :G * HPG * HD].reshape(B, G, HPG, HD)
    kv = z_nsa[:, G * HPG * HD:G * HPG * HD + 6 * G * HD].reshape(B, 6, G, HD)
    ngt = z_nsa[:, G * HPG * HD + 6 * G * HD:][:, :3 * G * HPG].reshape(B, 3, G, HPG)
    q = _rmsnorm(nq, p['nsa_gq']) * (HD ** -0.5)
    rows_new = jnp.stack([kv[:, 0], kv[:, 1], _rmsnorm(kv[:, 2], p['nsa_gk'][1]), kv[:, 3]], axis=1)
    win_new = jnp.stack([_rmsnorm(kv[:, 4], p['nsa_gk'][2]), kv[:, 5]], axis=1)
    pad_heads = lambda a: jnp.pad(a, ((0, 0), (0, 0), (0, HEAD_ROWS - HPG), (0, 0)))
    q8 = pad_heads(q).astype(jnp.bfloat16)
    new_rows = jnp.stack([rows_new[:, 2], rows_new[:, 3], win_new[:, 0], win_new[:, 1]], axis=2)
    new_rows = jnp.pad(new_rows, ((0, 0), (0, 0), (0, SUB - 4), (0, 0)))
    gate8 = jnp.pad(ngt.transpose(0, 2, 1, 3)[..., None], ((0, 0), (0, 0), (0, 0), (0, HEAD_ROWS - HPG), (0, 0)))
    gate8 = jnp.broadcast_to(gate8, (B, G, 3, HEAD_ROWS, HD))

    def lookup(dist, valid):
        t = jnp.where(valid[..., None], _bias_lookup(rel_bias, dist), NEG)
        t = jnp.moveaxis(t, -1, 0).reshape((G, HPG) + dist.shape)
        return jnp.pad(t, ((0, 0), (0, HEAD_ROWS - HPG)) + ((0, 0),) * dist.ndim)

    nc = n_pages * CMP_PER_PAGE
    n = jnp.arange(nc)
    bias_c = lookup(past_len - (n * CMP_STRIDE + CMP_BLOCK - 1), n < nc - 1)
    ns = cur + 1
    nsp = _round_up(ns, LANE)
    cov = jnp.pad(_block_cover(nc - 1, ns), ((0, 1), (0, nsp - ns))).astype(jnp.bfloat16)
    jj = jnp.arange(SEL_BLOCK)
    sel_d = jnp.stack([SEL_BLOCK * d - jj for d in (1, 2, 3)])
    bsel = lookup(sel_d, sel_d >= 0).transpose(0, 2, 1, 3)
    wb = win_l.shape[1]
    bwin = lookup(wb - jnp.arange(wb), jnp.ones((wb,), bool))
    b0 = lookup(jnp.zeros((1,), jnp.int32), jnp.ones((1,), bool))

    cache2d = cache_l.reshape(-1, HD)
    kcvc = _cmp_sample(cache2d, page_table, layer_off, p['cmp_w1'], p['cmp_w2'], p['cmp_pe'], p['nsa_gk'][0])
    ocmp, top = _topk_sample(q8, kcvc, bias_c, cov, cur)
    win_rows = win_l.reshape(B, wb, 2 * G * HD)
    o = _sel_sample(cache2d, page_table, layer_off, top[:, :, 0, :], q8, new_rows, win_rows, ocmp,
                    bsel, bwin, b0, gate8, cur)
    o_nsa = o[:, :, :HPG, :].reshape(B, G * HPG * HD)
    win_state = jnp.concatenate([win_l[:, 1:], win_new[:, None]], axis=1)
    return o_nsa, rows_new[:, None], win_state


def _split(z, sizes):
    cuts = [int(c) for c in np.cumsum(sizes)[:-1]]
    return jnp.split(z, cuts, axis=-1)


def _rmsnorm(x, g):
    xf = x.astype(jnp.float32)
    y = xf * lax.rsqrt(jnp.mean(xf * xf, axis=-1, keepdims=True) + EPS)
    return (y * g.astype(jnp.float32)).astype(x.dtype)


def _masked_softmax(s, mask):
    s = jnp.where(mask, s.astype(jnp.float32), NEG)
    return jnp.where(mask, jax.nn.softmax(s, axis=-1), 0.0)


def _t5_bucket(dist):
    d = jnp.maximum(dist, 0)
    exact = REL_BUCKETS // 2
    far = exact + (jnp.log(jnp.maximum(d, 1).astype(jnp.float32) / exact)
                   / math.log(REL_MAX_DIST / exact) * (REL_BUCKETS - exact)).astype(jnp.int32)
    return jnp.where(d < exact, d, jnp.minimum(far, REL_BUCKETS - 1))


def _nsa_compress(k_rows, v_rows, pe, w1, w2, g_kc):
    B, L, G, _ = k_rows.shape
    nc = (L - CMP_BLOCK) // CMP_STRIDE + 1
    idx = jnp.arange(nc)[:, None] * CMP_STRIDE + jnp.arange(CMP_BLOCK)[None, :]

    def phi(rows, pe_, w1_, w2_):
        blk = rows[:, idx] + pe_[None, None, :, None, :]
        flat = blk.transpose(0, 1, 3, 2, 4).reshape(B, nc, G, CMP_BLOCK * HEAD_DIM)
        return jax.nn.silu(flat @ w1_) @ w2_

    kc = _rmsnorm(phi(k_rows, pe[0], w1[0], w2[0]), g_kc)
    vc = phi(v_rows, pe[1], w1[1], w2[1])
    c_end = jnp.arange(nc) * CMP_STRIDE + CMP_BLOCK - 1
    return kc, vc, c_end


def _block_cover(nc, ns):
    c0 = jnp.arange(nc)[:, None] * CMP_STRIDE
    s0 = jnp.arange(ns)[None, :] * SEL_BLOCK
    return ((c0 <= s0 + SEL_BLOCK - 1) & (c0 + CMP_BLOCK - 1 >= s0)).astype(jnp.float32)


def _nsa_cmp_sel(q, q_pos, kc, vc, c_end, ks_t, vs_t, rel_bias, cover):
    B, Tq, G, HPG, _ = q.shape
    L = ks_t.shape[2]
    ns = cover.shape[1]
    s_c = jnp.einsum('btghd,bngd->btghn', q, kc).astype(jnp.float32)
    dist_c = q_pos[:, None] - c_end[None, :]
    bias_c = rel_bias[_t5_bucket(dist_c)].reshape(Tq, -1, G, HPG).transpose(0, 2, 3, 1)
    p_c = _masked_softmax(s_c + bias_c, (dist_c >= 0)[:, None, None, :])
    o_cmp = jnp.einsum('btghn,bngd->btghd', p_c.astype(vc.dtype), vc)
    imp = jnp.einsum('btgn,ns->btgs', p_c.sum(axis=3), cover)
    blk = jnp.arange(ns)[None, :]
    cur = (q_pos // SEL_BLOCK)[:, None]
    forced = ((blk == 0) | (blk == cur) | (blk == cur - 1))[:, None, :]
    valid = (blk <= cur)[:, None, :]
    score = jnp.where(forced, BIG, jnp.where(valid, imp, -BIG))
    n_top = min(N_SELECT, ns)
    _, top = lax.top_k(score, n_top)
    tok = (top[..., None] * SEL_BLOCK + jnp.arange(SEL_BLOCK)).reshape(B, Tq, G, n_top * SEL_BLOCK)
    tok_c = jnp.minimum(tok, L - 1)
    bi = jnp.arange(B)[:, None, None, None]
    gi = jnp.arange(G)[None, None, :, None]
    kg = ks_t[bi, gi, tok_c]
    vg = vs_t[bi, gi, tok_c]
    s_s = jnp.einsum('btghd,btgsd->btghs', q, kg).astype(jnp.float32)
    dist_s = q_pos[None, :, None, None] - tok
    bias_s = rel_bias.reshape(REL_BUCKETS, G, HPG)[_t5_bucket(dist_s), gi]
    p_s = _masked_softmax(s_s + bias_s.transpose(0, 1, 2, 4, 3), (dist_s >= 0)[:, :, :, None, :])
    o_sel = jnp.einsum('btghs,btgsd->btghd', p_s.astype(vg.dtype), vg)
    return o_cmp, o_sel


def _window_attend(q, q_pos, k, v, k_pos, rel_bias):
    N, Qb = q_pos.shape
    Kb = k_pos.shape[1]
    G, HPG = q.shape[3], q.shape[4]
    s = jnp.einsum('bnqghd,bnkgd->bnqghk', q, k).astype(jnp.float32)
    dist = q_pos[:, :, None] - k_pos[:, None, :]
    bias = rel_bias[_t5_bucket(dist)].reshape(N, Qb, Kb, G, HPG).transpose(0, 1, 3, 4, 2)
    mask = ((dist >= 0) & (dist <= WINDOW) & (k_pos[:, None, :] >= 0))[:, :, None, None, :]
    p = _masked_softmax(s + bias, mask)
    return jnp.einsum('bnqghk,bnkgd->bnqghd', p.astype(v.dtype), v)


def _gated_linear_attn(q, k, v, log_a, s0):
    B, T, H, _ = q.shape
    dv = v.shape[-1]
    C = CHUNK if T % CHUNK == 0 else T
    n = T // C

    def chunks(a):
        return a.astype(jnp.float32).reshape(B, n, C, H, a.shape[-1]).transpose(1, 0, 3, 2, 4)

    causal = jnp.tril(jnp.ones((C, C), bool))

    def step(S, inp):
        qc, kc, vc, gc = inp
        b = jnp.cumsum(gc, axis=2)
        diff = jnp.where(causal[:, :, None], b[:, :, :, None, :] - b[:, :, None, :, :], -jnp.inf)
        att = jnp.einsum('bhtk,bhsk,bhtsk->bhts', qc, kc, jnp.exp(diff))
        o = (jnp.einsum('bhtk,bhkv->bhtv', qc * jnp.exp(b), S)
             + jnp.einsum('bhts,bhsv->bhtv', att, vc))
        b_last = b[:, :, -1:, :]
        S = (jnp.exp(b_last[:, :, 0, :])[..., None] * S
             + jnp.einsum('bhsk,bhsv->bhkv', kc * jnp.exp(b_last - b), vc))
        return S, o

    S, o = lax.scan(step, s0.astype(jnp.float32), (chunks(q), chunks(k), chunks(v), chunks(log_a)))
    o = o.transpose(1, 0, 3, 2, 4).reshape(B, T, H, dv)
    return o.astype(v.dtype), S.astype(s0.dtype)


def _mlstm(q, k, v, ig, lf, C0, n0, m0):
    B, T, H, _ = q.shape
    dv = v.shape[-1]
    C = CHUNK if T % CHUNK == 0 else T
    n = T // C

    def chunks(a):
        return a.astype(jnp.float32).reshape(B, n, C, H, a.shape[-1]).transpose(1, 0, 3, 2, 4)

    def gchunks(a):
        return a.astype(jnp.float32).reshape(B, n, C, H).transpose(1, 0, 3, 2)

    causal = jnp.tril(jnp.ones((C, C), bool))

    def step(carry, inp):
        Cs, ns_, m = carry
        qc, kc, vc, ic, fc = inp
        F = jnp.cumsum(fc, axis=-1)
        logw = jnp.where(causal, F[..., :, None] - F[..., None, :] + ic[..., None, :], -jnp.inf)
        from_state = F + m[..., None]
        m_hat = jnp.maximum(from_state, logw.max(-1))
        w = jnp.exp(logw - m_hat[..., None]) * jnp.einsum('bhtk,bhsk->bhts', qc, kc)
        ws = jnp.exp(from_state - m_hat)
        num = ws[..., None] * jnp.einsum('bhtk,bhvk->bhtv', qc, Cs) + jnp.einsum('bhts,bhsv->bhtv', w, vc)
        den = ws * jnp.einsum('bhtk,bhk->bht', qc, ns_) + w.sum(-1)
        h = num / jnp.maximum(jnp.abs(den), jnp.exp(-m_hat))[..., None]
        m_new = m_hat[..., -1]
        ds = jnp.exp(F[..., -1:] - F + ic - m_new[..., None])
        dst = jnp.exp(F[..., -1] + m - m_new)
        Cs = dst[..., None, None] * Cs + jnp.einsum('bhs,bhsv,bhsk->bhvk', ds, vc, kc)
        ns_ = dst[..., None] * ns_ + jnp.einsum('bhs,bhsk->bhk', ds, kc)
        return (Cs, ns_, m_new), h

    carry0 = (C0.astype(jnp.float32), n0.astype(jnp.float32), m0.astype(jnp.float32))
    (Cf, nf, mf), h = lax.scan(step, carry0, (chunks(q), chunks(k), chunks(v), gchunks(ig), gchunks(lf)))
    h = h.transpose(1, 0, 3, 2, 4).reshape(B, T, H, dv)
    return h.astype(v.dtype), Cf.astype(C0.dtype), nf.astype(n0.dtype), mf.astype(m0.dtype)


def _causal_conv(u, buf, w, b):
    T = u.shape[1]
    up = jnp.concatenate([buf, u], axis=1)
    y = b
    for j in range(MLSTM_CONV):
        y = y + up[:, j:j + T] * w[j]
    return jax.nn.silu(y), up[:, -(MLSTM_CONV - 1):]


GROUPS = {
    'nsa': (('nq', 'nkv', 'ngt'), 3072),
    'gla': (('gq', 'gk', 'gv', 'gr', 'ga'), 3584),
    'hgrn': (('hq', 'hf', 'hi', 'hg'), 4096),
    'mlstm': (('mqk', 'mv', 'mo', 'mi', 'mf'), 3584),
    'gate': (('mg',), N_BRANCH * D_MODEL),
}
IN_NAMES = ('nq', 'nkv', 'ngt', 'gq', 'gk', 'gv', 'ga', 'gr', 'hq', 'hf', 'hi', 'hg',
            'mqk', 'mv', 'mi', 'mf', 'mo', 'mg')
IN_WIDTH = dict(zip(IN_NAMES, IN_SIZES))


def _group_weights(w_in_l):
    cuts = dict(zip(IN_NAMES, np.cumsum((0,) + IN_SIZES[:-1])))
    out = {}
    for gname, (members, width) in GROUPS.items():
        w = jnp.concatenate([w_in_l[:, int(cuts[m]):int(cuts[m]) + IN_WIDTH[m]] for m in members], axis=1)
        w = w.astype(jnp.bfloat16)
        out[gname] = jnp.pad(w, ((0, 0), (0, width - w.shape[1])))
    return out


def _group_fields(zs, gname, B, T):
    out, off = {}, 0
    for m in GROUPS[gname][0]:
        out[m] = zs[gname][:, off:off + IN_WIDTH[m]].reshape(B, T, IN_WIDTH[m])
        off += IN_WIDTH[m]
    return out


def _token_mixers(zs, B, T, pos0, past, p, rel_bias, lb, tables):
    G, HPG, HD = NSA_KV_HEADS, NSA_HPG, HEAD_DIM
    f = _group_fields(zs, 'nsa', B, T)
    nq, nkv, ngt = f['nq'], f['nkv'], f['ngt']
    f = _group_fields(zs, 'mlstm', B, T)
    mqk, mv, mo, mi, mf = f['mqk'], f['mv'], f['mo'], f['mi'], f['mf']
    q_pos = pos0 + jnp.arange(T)

    if past is None:
        q = _rmsnorm(nq.reshape(B, T, G, HPG, HD), p['nsa_gq']) * (HD ** -0.5)
        kv = nkv.reshape(B, T, 6, G, HD)
        rows_new = jnp.stack([kv[:, :, 0], kv[:, :, 1], _rmsnorm(kv[:, :, 2], p['nsa_gk'][1]), kv[:, :, 3]], axis=2)
        win_new = jnp.stack([_rmsnorm(kv[:, :, 4], p['nsa_gk'][2]), kv[:, :, 5]], axis=2)
        kcvc = _cmp_prompt(zs['nsa'].reshape(B, T, -1), p['cmp_w1'], p['cmp_w2'], p['cmp_pe'], p['nsa_gk'][0])
        o_nsa = _nsa_prompt(q, rows_new[:, :, 2], rows_new[:, :, 3], win_new[:, :, 0], win_new[:, :, 1],
                            kcvc[:, :G], kcvc[:, G:], ngt, tables)
        win_state = win_new[:, -min(WINDOW, T):]
    else:
        o_nsa, rows_new, win_state = _nsa_sample(zs['nsa'], past['cache'], past['layer_off'], past['page_table'],
                                                 past['win'], p, rel_bias)

    if past is None:
        wa_p = jnp.pad(p['gla_wa'], ((0, LANE - GLA_GATE_RANK), (0, 0))).astype(jnp.bfloat16)
        o_gla, gla_state = _lin_attn_prompt(zs['gla'], B, T, "gla", (wa_p, p['gla_ba'][None], p['gla_gn'][None]))
        o_hgrn, hgrn_state = _lin_attn_prompt(zs['hgrn'], B, T, "hgrn",
                                              (jnp.log(lb)[None], jnp.log1p(-lb)[None], p['hgrn_gn'][None]))
    else:
        f = _group_fields(zs, 'gla', B, T)
        gq, gk, gv, gr, ga = f['gq'], f['gk'], f['gv'], f['gr'], f['ga']
        g_q = gq.reshape(B, T, GLA_HEADS, GLA_DK) * (GLA_DK ** -0.5)
        g_k = gk.reshape(B, T, GLA_HEADS, GLA_DK)
        g_v = gv.reshape(B, T, GLA_HEADS, GLA_DV)
        log_a = (jax.nn.log_sigmoid((ga @ p['gla_wa'] + p['gla_ba']).astype(jnp.float32)) / GLA_TAU).reshape(B, T, GLA_HEADS, GLA_DK)
        o, gla_state = _gated_linear_attn(g_q, g_k, g_v, log_a, past['gla'])
        o_gla = (_rmsnorm(o, p['gla_gn']) * jax.nn.silu(gr.reshape(B, T, GLA_HEADS, GLA_DV))).reshape(B * T, -1)

        f = _group_fields(zs, 'hgrn', B, T)
        hq, hf, hi, hg = f['hq'], f['hf'], f['hi'], f['hg']
        log_f = jnp.logaddexp(jnp.log(lb), jnp.log1p(-lb) + jax.nn.log_sigmoid(hf.astype(jnp.float32)))
        log_f = log_f.reshape(B, T, HGRN_HEADS, HGRN_EXPAND)
        h_q = jax.nn.silu(hq).reshape(B, T, HGRN_HEADS, HGRN_EXPAND)
        h_v = hi.reshape(B, T, HGRN_HEADS, HGRN_EXPAND)
        o, hgrn_state = _gated_linear_attn(h_q, -jnp.expm1(log_f), h_v, log_f, past['hgrn'])
        o_hgrn = (_rmsnorm(o, p['hgrn_gn']) * jax.nn.sigmoid(hg.reshape(B, T, HGRN_HEADS, HGRN_EXPAND))).reshape(B * T, -1)

    nqk = MLSTM_HEADS * MLSTM_DQK
    if past is None:
        o_mlstm, mC, mn, mm = _mlstm_prompt(zs['mlstm'], B, T, p)
        conv_state = mqk[:, -(MLSTM_CONV - 1):]
    else:
        u, conv_state = _causal_conv(mqk, past['conv'], p['m_wconv'], p['m_bconv'])
        m_q = u[..., :nqk].reshape(B, T, MLSTM_HEADS, MLSTM_DQK)
        m_k = u[..., nqk:].reshape(B, T, MLSTM_HEADS, MLSTM_DQK) * (MLSTM_DQK ** -0.5)
        m_v = mv.reshape(B, T, MLSTM_HEADS, MLSTM_DV)
        ig = (mi + p['m_bi']).astype(jnp.float32)
        lf = jax.nn.log_sigmoid((mf + p['m_bf']).astype(jnp.float32))
        hm, mC, mn, mm = _mlstm(m_q, m_k, m_v, ig, lf, past['mC'], past['mn'], past['mm'])
        o_mlstm = (jax.nn.sigmoid(mo).reshape(B, T, MLSTM_HEADS, MLSTM_DV) * _rmsnorm(hm, p['m_gn'])).reshape(B, T, -1)

    obs = [o.reshape(B * T, BRANCH_WIDTH).astype(jnp.bfloat16) for o in (o_nsa, o_gla, o_hgrn, o_mlstm)]
    return obs, (rows_new, win_state, gla_state, hgrn_state, mC, mn, mm, conv_state)


def _layer(x, mod, pos0, past, p, rel_bias, lb, tables=None):
    B, T, _ = x.shape
    m = B * T
    mp = _round_up(m, 16)

    def rows(a):
        a = a.reshape(m, a.shape[-1])
        return a if mp == m else jnp.pad(a, ((0, mp - m), (0, 0)))

    sh1, sc1, gt1, sh2, sc2, gt2 = jnp.split(mod, 6, axis=-1)
    if T > 1:
        gates3 = lambda g: (g[:, None, :], T)
    else:
        gates3 = lambda g: (rows(g[:, None, :])[None], None)
    hb = rows(_normmod(x, p['g_mix'], sc1, sh1))
    zs = {g: _proj(hb, p['w_' + g], name="proj_" + g) for g in ('nsa', 'gla', 'hgrn', 'mlstm')}
    gates = _proj(hb, p['w_gate'], out_dtype=jnp.bfloat16, act="sigmoid", name="proj_gate")
    obs, st = _token_mixers({g: z[:m] for g, z in zs.items()}, B, T, pos0, past, p, rel_bias, lb, tables)
    merged = _merge([rows(o) for o in obs], p['w_branch'], gates)
    g3, rpg = gates3(gt1)
    x1 = _res32(merged, p['w_out'], p['layer'], rows(x), g3, rows_per_gate=rpg, n_k=1, name="mm_out")
    hb2 = rows(_normmod(x1[:m].reshape(B, T, D_MODEL), p['g_ffn'], sc2, sh2))
    act = _swiglu32(hb2, p['w_ffn_in'], p['layer'])
    g3, rpg = gates3(gt2)
    x2 = _res32(act, p['w_ffn_out'], p['layer'], x1, g3, rows_per_gate=rpg, n_k=2, name="mm_ffn_out")
    return x2[:m].reshape(B, T, D_MODEL), st


def kernel(x_prompt, x_sample, cache_nsa_kv, state_nsa_win, state_gla, state_hgrn, state_mlstm_C,
           state_mlstm_n, state_mlstm_m, state_mlstm_conv, page_table, c_prompt, c_sample, rel_bias,
           w_ada, b_ada, g_mix, g_ffn, w_in, nsa_gq, nsa_gk, cmp_pe, cmp_w1, cmp_w2, gla_wa, gla_ba,
           gla_gn, hgrn_lb, hgrn_gn, m_wconv, m_bconv, m_bi, m_bf, m_gn, w_branch, w_out, w_ffn_in, w_ffn_out):
    lb_cum = jnp.cumsum(jax.nn.softmax(hgrn_lb.astype(jnp.float32), axis=0), axis=0)
    lb_all = lb_cum - lb_cum[:1]
    dec_b, n_pages = page_table.shape
    past_len = n_pages * PAGE_SIZE
    n_prompt = c_prompt.shape[0]
    x_p, x_s = x_prompt, x_sample
    st_prompt, st_sample = [], []
    c_all = jax.nn.silu(jnp.concatenate([c_prompt, c_sample], axis=0))
    n_c = c_all.shape[0]
    c_act = jnp.pad(c_all, ((0, _round_up(n_c, 16) - n_c), (0, 0))).astype(jnp.bfloat16)
    tables = _bias_tables(rel_bias, x_prompt.shape[1])
    n_pool = cache_nsa_kv.shape[1]
    cache_all = cache_nsa_kv.reshape((DEPTH * n_pool,) + cache_nsa_kv.shape[2:])
    for l in range(DEPTH):
        p = {'g_mix': g_mix[l], 'g_ffn': g_ffn[l], 'layer': l,
             'nsa_gq': nsa_gq[l], 'nsa_gk': nsa_gk[l], 'cmp_pe': cmp_pe[l], 'cmp_w1': cmp_w1[l],
             'cmp_w2': cmp_w2[l], 'gla_wa': gla_wa[l], 'gla_ba': gla_ba[l], 'gla_gn': gla_gn[l],
             'hgrn_gn': hgrn_gn[l], 'm_wconv': m_wconv[l], 'm_bconv': m_bconv[l], 'm_bi': m_bi[l],
             'm_bf': m_bf[l], 'm_gn': m_gn[l],
             'w_branch': w_branch[l].astype(jnp.bfloat16),
             'w_out': w_out, 'w_ffn_in': w_ffn_in, 'w_ffn_out': w_ffn_out}
        p.update({'w_' + g: w for g, w in _group_weights(w_in[l]).items()})
        mod = _ada_mod(c_act, w_ada, b_ada, l)[:n_c]
        x_p, sp = _layer(x_p, mod[:n_prompt], 0, None, p, rel_bias, lb_all[l], tables)
        past = {'cache': cache_all, 'layer_off': l * n_pool, 'page_table': page_table,
                'win': state_nsa_win[l], 'gla': state_gla[l], 'hgrn': state_hgrn[l],
                'mC': state_mlstm_C[l], 'mn': state_mlstm_n[l], 'mm': state_mlstm_m[l],
                'conv': state_mlstm_conv[l]}
        x_s, ss = _layer(x_s, mod[n_prompt:], past_len, past, p, rel_bias, lb_all[l])
        st_prompt.append(sp)
        st_sample.append(ss)

    def stk(sts, i):
        return jnp.stack([s[i] for s in sts], axis=0)

    outs = [x_p, x_s]
    for i in range(8):
        outs.append(stk(st_prompt, i))
        outs.append(stk(st_sample, i))
    return tuple(outs)
```

```python
import functools
import math

import jax
import jax.numpy as jnp
import numpy as np
from jax import lax
from jax.experimental import pallas as pl
from jax.experimental.pallas import tpu as pltpu

D_MODEL = 4096
DEPTH = 2
PAGE_SIZE = 128
HEAD_DIM = 128
N_BRANCH = 4
BRANCH_WIDTH = D_MODEL // N_BRANCH
NSA_HEADS = BRANCH_WIDTH // HEAD_DIM
NSA_KV_HEADS = 2
NSA_HPG = NSA_HEADS // NSA_KV_HEADS
CMP_BLOCK = 32
CMP_STRIDE = 16
SEL_BLOCK = 64
N_SELECT = 16
WINDOW = 512
Q_BLOCK = 64
BAND = 128
REL_BUCKETS = 32
REL_MAX_DIST = 128
GLA_HEADS = 4
GLA_DK = BRANCH_WIDTH // (2 * GLA_HEADS)
GLA_DV = BRANCH_WIDTH // GLA_HEADS
GLA_GATE_RANK = 16
GLA_TAU = 16.0
HGRN_EXPAND = 128
HGRN_HEADS = BRANCH_WIDTH // HGRN_EXPAND
MLSTM_HEADS = 4
MLSTM_DQK = BRANCH_WIDTH // (2 * MLSTM_HEADS)
MLSTM_DV = BRANCH_WIDTH // MLSTM_HEADS
MLSTM_CONV = 4
CHUNK = 64
FF_DIM = ((8 * D_MODEL + 3 * 256 - 1) // (3 * 256)) * 256
EPS = 1e-6
NEG = -1e30
BIG = 1e6
IN_SIZES = (
    NSA_HEADS * HEAD_DIM, 6 * NSA_KV_HEADS * HEAD_DIM, 3 * NSA_HEADS,
    GLA_HEADS * GLA_DK, GLA_HEADS * GLA_DK, GLA_HEADS * GLA_DV, GLA_GATE_RANK, BRANCH_WIDTH,
    BRANCH_WIDTH, BRANCH_WIDTH, BRANCH_WIDTH, BRANCH_WIDTH,
    2 * MLSTM_HEADS * MLSTM_DQK, MLSTM_HEADS * MLSTM_DV, MLSTM_HEADS, MLSTM_HEADS, BRANCH_WIDTH,
    N_BRANCH * D_MODEL,
)
D_IN = sum(IN_SIZES)

V7X_VMEM_LIMIT_BYTES = 56 * 1024 * 1024
LANE = 128
SUB = 8


def _round_up(n, m):
    return (n + m - 1) // m * m


def _mm_kernel(a_ref, w_ref, o_ref):
    o_ref[...] = jnp.dot(a_ref[...], w_ref[...].astype(jnp.bfloat16),
                         preferred_element_type=jnp.float32).astype(o_ref.dtype)


def _mm_k_kernel(a_ref, w_ref, o_ref, acc_ref):
    k = pl.program_id(2)

    @pl.when(k == 0)
    def _():
        acc_ref[...] = jnp.zeros_like(acc_ref)

    acc_ref[...] += jnp.dot(a_ref[...], w_ref[...], preferred_element_type=jnp.float32)

    @pl.when(k == pl.num_programs(2) - 1)
    def _():
        o_ref[...] = acc_ref[...].astype(o_ref.dtype)


def _mm(a, w, *, bm, bn, bk=None, out_dtype=jnp.float32, name="mm"):
    m, kd = a.shape
    n = w.shape[1]
    assert w.shape[0] == kd and m % bm == 0 and n % bn == 0, (a.shape, w.shape, bm, bn)
    params = dict(vmem_limit_bytes=V7X_VMEM_LIMIT_BYTES)
    if bk is None or bk == kd:
        return pl.pallas_call(
            _mm_kernel,
            grid=(n // bn, m // bm),
            in_specs=[pl.BlockSpec((bm, kd), lambda j, i: (i, 0)),
                      pl.BlockSpec((kd, bn), lambda j, i: (0, j))],
            out_specs=pl.BlockSpec((bm, bn), lambda j, i: (i, j)),
            out_shape=jax.ShapeDtypeStruct((m, n), out_dtype),
            compiler_params=pltpu.CompilerParams(dimension_semantics=("arbitrary", "arbitrary"), **params),
            name=name,
        )(a, w)
    assert kd % bk == 0
    return pl.pallas_call(
        _mm_k_kernel,
        grid=(n // bn, m // bm, kd // bk),
        in_specs=[pl.BlockSpec((bm, bk), lambda j, i, k: (i, k)),
                  pl.BlockSpec((bk, bn), lambda j, i, k: (k, j))],
        out_specs=pl.BlockSpec((bm, bn), lambda j, i, k: (i, j)),
        out_shape=jax.ShapeDtypeStruct((m, n), out_dtype),
        scratch_shapes=[pltpu.VMEM((bm, bn), jnp.float32)],
        compiler_params=pltpu.CompilerParams(
            dimension_semantics=("arbitrary", "arbitrary", "arbitrary"), **params),
        name=name,
    )(a, w)


def _matmul(a, w_bf16, name):
    m, kd = a.shape
    a = a.astype(jnp.bfloat16)
    mp = _round_up(m, 16)
    if mp != m:
        a = jnp.pad(a, ((0, mp - m), (0, 0)))
    bm, bn = _tiles(mp, w_bf16.shape[1])
    out = _mm(a, w_bf16, bm=bm, bn=bn, name=name)
    return out[:m] if mp != m else out


def _tiles(m, n):
    if m >= 1024:
        return 1024, 512
    return m, (1024 if n % 1024 == 0 else 512)


def _cparams(n_axes):
    return pltpu.CompilerParams(dimension_semantics=("arbitrary",) * n_axes,
                                vmem_limit_bytes=V7X_VMEM_LIMIT_BYTES)


def _proj_kernel(a_ref, w_ref, o_ref, *, act):
    y = jnp.dot(a_ref[...], w_ref[...], preferred_element_type=jnp.float32)
    if act == "sigmoid":
        y = jax.nn.sigmoid(y)
    o_ref[...] = y.astype(o_ref.dtype)


def _proj(a, w, *, out_dtype=jnp.float32, act=None, name="proj"):
    m, kd = a.shape
    n = w.shape[1]
    bm, bn = _tiles(m, n)
    return pl.pallas_call(
        functools.partial(_proj_kernel, act=act),
        grid=(n // bn, m // bm),
        in_specs=[pl.BlockSpec((bm, kd), lambda j, i: (i, 0)),
                  pl.BlockSpec((kd, bn), lambda j, i: (0, j))],
        out_specs=pl.BlockSpec((bm, bn), lambda j, i: (i, j)),
        out_shape=jax.ShapeDtypeStruct((m, n), out_dtype),
        compiler_params=_cparams(2),
        name=name,
    )(a, w)


def _mm_res_kernel(a_ref, w_ref, x_ref, gt_ref, o_ref, acc_ref):
    k = pl.program_id(2)

    @pl.when(k == 0)
    def _():
        acc_ref[...] = jnp.zeros_like(acc_ref)

    acc_ref[...] += jnp.dot(a_ref[...], w_ref[...], preferred_element_type=jnp.float32)

    @pl.when(k == pl.num_programs(2) - 1)
    def _():
        o_ref[...] = x_ref[...] + gt_ref[0] * acc_ref[...]


def _mm_res(a, w, x, gt3, *, rows_per_gate, bk=None, name="mm_res"):
    m, kd = a.shape
    n = w.shape[1]
    bm, bn = _tiles(m, n)
    bk = kd if bk is None else bk
    if rows_per_gate is None:
        gt_spec = pl.BlockSpec((1, bm, bn), lambda j, i, k: (0, i, j))
    else:
        tiles_per_gate = rows_per_gate // bm
        gt_spec = pl.BlockSpec((1, 1, bn), lambda j, i, k: (i // tiles_per_gate, 0, j))
    return pl.pallas_call(
        _mm_res_kernel,
        grid=(n // bn, m // bm, kd // bk),
        in_specs=[pl.BlockSpec((bm, bk), lambda j, i, k: (i, k)),
                  pl.BlockSpec((bk, bn), lambda j, i, k: (k, j)),
                  pl.BlockSpec((bm, bn), lambda j, i, k: (i, j)),
                  gt_spec],
        out_specs=pl.BlockSpec((bm, bn), lambda j, i, k: (i, j)),
        out_shape=jax.ShapeDtypeStruct((m, n), jnp.float32),
        scratch_shapes=[pltpu.VMEM((bm, bn), jnp.float32)],
        compiler_params=_cparams(3),
        name=name,
    )(a, w, x, gt3)


def _mm_swiglu_kernel(a_ref, wg_ref, wu_ref, o_ref):
    a = a_ref[...]
    g = jnp.dot(a, wg_ref[...], preferred_element_type=jnp.float32)
    u = jnp.dot(a, wu_ref[...], preferred_element_type=jnp.float32)
    o_ref[...] = (g * jax.nn.sigmoid(g) * u).astype(o_ref.dtype)


def _mm_swiglu(a, w_gu, name="mm_swiglu"):
    m, kd = a.shape
    f = w_gu.shape[1] // 2
    bm, bn = _tiles(m, f)
    nb = f // bn
    return pl.pallas_call(
        _mm_swiglu_kernel,
        grid=(nb, m // bm),
        in_specs=[pl.BlockSpec((bm, kd), lambda j, i: (i, 0)),
                  pl.BlockSpec((kd, bn), lambda j, i: (0, j)),
                  pl.BlockSpec((kd, bn), lambda j, i: (0, j + nb))],
        out_specs=pl.BlockSpec((bm, bn), lambda j, i: (i, j)),
        out_shape=jax.ShapeDtypeStruct((m, f), jnp.bfloat16),
        compiler_params=_cparams(2),
        name=name,
    )(a, w_gu, w_gu)


def _ada_kernel(a_ref, w_ref, b_ref, o_ref):
    o_ref[...] = jnp.dot(a_ref[...], w_ref[0].astype(jnp.bfloat16), preferred_element_type=jnp.float32) + b_ref[0]


def _ada_mod(c_act, w_ada, b_ada, layer):
    m, kd = c_act.shape
    n = w_ada.shape[2]
    bn = 1024
    return pl.pallas_call(
        _ada_kernel,
        grid=(n // bn,),
        in_specs=[pl.BlockSpec((m, kd), lambda j: (0, 0)),
                  pl.BlockSpec((1, kd, bn), lambda j: (layer, 0, j)),
                  pl.BlockSpec((1, 1, bn), lambda j: (layer, 0, j))],
        out_specs=pl.BlockSpec((m, bn), lambda j: (0, j)),
        out_shape=jax.ShapeDtypeStruct((m, n), jnp.float32),
        compiler_params=_cparams(1),
        name="ada_mod",
    )(c_act, w_ada, b_ada[:, None, :])


def _swiglu32_kernel(a_ref, wg_ref, wu_ref, o_ref, wg_s, wu_s):
    @pl.when(pl.program_id(1) == 0)
    def _():
        wg_s[...] = wg_ref[0].astype(jnp.bfloat16)
        wu_s[...] = wu_ref[0].astype(jnp.bfloat16)

    a = a_ref[...]
    g = jnp.dot(a, wg_s[...], preferred_element_type=jnp.float32)
    u = jnp.dot(a, wu_s[...], preferred_element_type=jnp.float32)
    o_ref[...] = (g * jax.nn.sigmoid(g) * u).astype(o_ref.dtype)


FFN_BN = 256


def _swiglu32(a, w_gu, layer, name="mm_ffn_in"):
    m, kd = a.shape
    f = w_gu.shape[2] // 2
    bm = min(m, 1024)
    bn = FFN_BN
    nb = f // bn
    assert f % bn == 0 and m % bm == 0
    return pl.pallas_call(
        _swiglu32_kernel,
        grid=(nb, m // bm),
        in_specs=[pl.BlockSpec((bm, kd), lambda j, i: (i, 0)),
                  pl.BlockSpec((1, kd, bn), lambda j, i: (layer, 0, j)),
                  pl.BlockSpec((1, kd, bn), lambda j, i: (layer, 0, j + nb))],
        out_specs=pl.BlockSpec((bm, bn), lambda j, i: (i, j)),
        out_shape=jax.ShapeDtypeStruct((m, f), jnp.bfloat16),
        scratch_shapes=[pltpu.VMEM((kd, bn), jnp.bfloat16), pltpu.VMEM((kd, bn), jnp.bfloat16)],
        compiler_params=_cparams(2),
        name=name,
    )(a, w_gu, w_gu)


def _res32_kernel(a_ref, w_ref, x_ref, gt_ref, o_ref, acc_ref, w_s, *, bk):
    i, k = pl.program_id(1), pl.program_id(2)
    koff = pl.multiple_of(k * bk, bk)

    @pl.when(i == 0)
    def _():
        w_s[pl.ds(koff, bk), :] = w_ref[0].astype(jnp.bfloat16)

    @pl.when(k == 0)
    def _():
        acc_ref[...] = jnp.zeros_like(acc_ref)

    acc_ref[...] += jnp.dot(a_ref[...], w_s[pl.ds(koff, bk), :], preferred_element_type=jnp.float32)

    @pl.when(k == pl.num_programs(2) - 1)
    def _():
        o_ref[...] = x_ref[...] + gt_ref[0] * acc_ref[...]


def _res32(a, w, layer, x, gt3, *, rows_per_gate, n_k, name):
    m, kd = a.shape
    n = w.shape[2]
    bm = min(m, 512)
    bn = 512
    bk = kd // n_k
    assert kd % n_k == 0 and bk % 16 == 0 and m % bm == 0 and n % bn == 0
    if rows_per_gate is None:
        gt_spec = pl.BlockSpec((1, bm, bn), lambda j, i, k: (0, i, j))
    else:
        tiles_per_gate = rows_per_gate // bm
        gt_spec = pl.BlockSpec((1, 1, bn), lambda j, i, k: (i // tiles_per_gate, 0, j))
    return pl.pallas_call(
        functools.partial(_res32_kernel, bk=bk),
        grid=(n // bn, m // bm, n_k),
        in_specs=[pl.BlockSpec((bm, bk), lambda j, i, k: (i, k)),
                  pl.BlockSpec((1, bk, bn), lambda j, i, k: (layer, k, j)),
                  pl.BlockSpec((bm, bn), lambda j, i, k: (i, j)),
                  gt_spec],
        out_specs=pl.BlockSpec((bm, bn), lambda j, i, k: (i, j)),
        out_shape=jax.ShapeDtypeStruct((m, n), jnp.float32),
        scratch_shapes=[pltpu.VMEM((bm, bn), jnp.float32), pltpu.VMEM((kd, bn), jnp.bfloat16)],
        compiler_params=_cparams(3),
        name=name,
    )(a, w, x, gt3)


def _merge_kernel(*refs):
    obs, wb_ref, gates, o_ref = refs[:N_BRANCH], refs[N_BRANCH], refs[N_BRANCH + 1:2 * N_BRANCH + 1], refs[-1]
    acc = None
    for br in range(N_BRANCH):
        y = gates[br][...].astype(jnp.float32) * jnp.dot(obs[br][...], wb_ref[br],
                                                         preferred_element_type=jnp.float32)
        acc = y if acc is None else acc + y
    o_ref[...] = acc.astype(o_ref.dtype)


def _merge(obs, wb, gates, name="merge"):
    m, wd = obs[0].shape
    n = wb.shape[2]
    bm, bn = _tiles(m, n)
    nb = n // bn
    gate_specs = [pl.BlockSpec((bm, bn), functools.partial(lambda j, i, br: (i, br * nb + j), br=br))
                  for br in range(N_BRANCH)]
    return pl.pallas_call(
        _merge_kernel,
        grid=(nb, m // bm),
        in_specs=([pl.BlockSpec((bm, wd), lambda j, i: (i, 0))] * N_BRANCH
                  + [pl.BlockSpec((N_BRANCH, wd, bn), lambda j, i: (0, 0, j))] + gate_specs),
        out_specs=pl.BlockSpec((bm, bn), lambda j, i: (i, j)),
        out_shape=jax.ShapeDtypeStruct((m, n), jnp.bfloat16),
        compiler_params=_cparams(2),
        name=name,
    )(*obs, wb, *([gates] * N_BRANCH))


def _normmod_kernel(x_ref, g_ref, sc_ref, sh_ref, o_ref):
    x = x_ref[0]
    y = x * lax.rsqrt(jnp.mean(x * x, axis=-1, keepdims=True) + EPS) * g_ref[...]
    o_ref[0] = (y * (1.0 + sc_ref[0]) + sh_ref[0]).astype(o_ref.dtype)


def _normmod(x, g, sc, sh):
    B, T, D = x.shape
    tt = min(T, 256)
    row = pl.BlockSpec((1, 1, D), lambda b, t: (b, 0, 0))
    return pl.pallas_call(
        _normmod_kernel,
        grid=(B, T // tt),
        in_specs=[pl.BlockSpec((1, tt, D), lambda b, t: (b, t, 0)),
                  pl.BlockSpec((1, D), lambda b, t: (0, 0)), row, row],
        out_specs=pl.BlockSpec((1, tt, D), lambda b, t: (b, t, 0)),
        out_shape=jax.ShapeDtypeStruct((B, T, D), jnp.bfloat16),
        compiler_params=_cparams(2),
        name="normmod",
    )(x, g[None], sc[:, None], sh[:, None])


def _prep_w(w, n_mult=1024, k_mult=None):
    kd, n = w.shape
    np_ = _round_up(n, n_mult)
    kp = kd if k_mult is None else _round_up(kd, k_mult)
    w = w.astype(jnp.bfloat16)
    if np_ != n or kp != kd:
        w = jnp.pad(w, ((0, kp - kd), (0, np_ - n)))
    return w


NSA_TQ = 128
NSA_KB = 128
SEL_PER_KB = NSA_KB // SEL_BLOCK
WIN_TILES = WINDOW // NSA_KB + 1


def _split3_bf16(x):
    hi = x.astype(jnp.bfloat16)
    r1 = x - hi.astype(jnp.float32)
    mid = r1.astype(jnp.bfloat16)
    lo = (r1 - mid.astype(jnp.float32)).astype(jnp.bfloat16)
    return hi, mid, lo


def _nsa_prompt_kernel(qT_ref, ksel_ref, vselT_ref, kwin_ref, vwinT_ref, kc_ref, vcT_ref, covT_ref,
                       bcmp_ref, bsel_ref, bwin_ref, gate_ref, o_ref,
                       mt_ref, m_ref, l_ref, acc_ref, *, n_top):
    hpg = qT_ref.shape[2]
    tq = qT_ref.shape[4]
    ns = covT_ref.shape[0]
    qb = pl.program_id(2)
    t0 = qb * tq
    gates = jax.nn.sigmoid(gate_ref[0, 0])

    kc = kc_ref[0, 0]
    vcT = vcT_ref[0, 0]
    p_sum = jnp.zeros((kc.shape[0], tq), jnp.float32)
    for h in range(hpg):
        bias = bcmp_ref[0, h]
        s = jnp.dot(kc, qT_ref[0, 0, h], preferred_element_type=jnp.float32) + bias
        e = jnp.exp(s - jnp.max(s, axis=0, keepdims=True))
        p = jnp.where(bias > 0.5 * NEG, e / jnp.sum(e, axis=0, keepdims=True), 0.0)
        p_sum = p_sum + p
        o_c = jnp.dot(vcT, p.astype(jnp.bfloat16), preferred_element_type=jnp.float32)
        o_ref[0, 0, h] = gates[0, h:h + 1, :] * o_c
    cov = covT_ref[...]
    imp = sum(jnp.dot(cov, part, preferred_element_type=jnp.float32) for part in _split3_bf16(p_sum))

    blk = lax.broadcasted_iota(jnp.int32, (ns, tq), 0)
    cur = (t0 + lax.broadcasted_iota(jnp.int32, (ns, tq), 1)) // SEL_BLOCK
    forced = (blk == 0) | (blk == cur) | (blk == cur - 1)
    score = jnp.where(forced, BIG, jnp.where(blk <= cur, imp, -BIG))
    rank = jnp.zeros((ns, tq), jnp.int32)
    for jp in range(ns):
        row = score[jp:jp + 1, :]
        beats = (row > score) | ((row == score) & (blk > jp))
        rank = rank + beats.astype(jnp.int32)
    mt_ref[...] = (rank < n_top).astype(jnp.float32)

    sub = lax.broadcasted_iota(jnp.int32, (NSA_KB, tq), 0)

    def attend(k_ref, vT_ref, bias_ref, n_tiles, lo, use_sel):
        m_ref[...] = jnp.full(m_ref.shape, NEG, jnp.float32)
        l_ref[...] = jnp.zeros(l_ref.shape, jnp.float32)
        acc_ref[...] = jnp.zeros(acc_ref.shape, jnp.float32)

        def body(kb, carry):
            koff = pl.multiple_of(kb * NSA_KB, NSA_KB)
            k_blk = k_ref[0, 0, pl.ds(koff, NSA_KB), :]
            vT_blk = vT_ref[0, 0, :, pl.ds(koff, NSA_KB)]
            tile = jnp.minimum(qb - kb, n_tiles - 1)
            if use_sel:
                r0 = mt_ref[pl.ds(kb * SEL_PER_KB, 1), :]
                r1 = mt_ref[pl.ds(kb * SEL_PER_KB + 1, 1), :]
                selm = jnp.where(sub < SEL_BLOCK, r0, r1) > 0.5
            for h in range(hpg):
                s = jnp.dot(k_blk, qT_ref[0, 0, h], preferred_element_type=jnp.float32) + bias_ref[0, h, tile]
                if use_sel:
                    s = jnp.where(selm, s, NEG)
                m_old = m_ref[h]
                m_new = jnp.maximum(m_old, jnp.max(s, axis=0, keepdims=True))
                alpha = jnp.exp(m_old - m_new)
                p = jnp.exp(s - m_new)
                l_ref[h] = alpha * l_ref[h] + jnp.sum(p, axis=0, keepdims=True)
                acc_ref[h] = alpha * acc_ref[h] + jnp.dot(vT_blk, p.astype(jnp.bfloat16),
                                                          preferred_element_type=jnp.float32)
                m_ref[h] = m_new
            return carry

        lax.fori_loop(lo, qb + 1, body, 0)

    attend(ksel_ref, vselT_ref, bsel_ref, bsel_ref.shape[2], 0, True)
    for h in range(hpg):
        o_ref[0, 0, h] += gates[1, h:h + 1, :] * (acc_ref[h] / l_ref[h])

    attend(kwin_ref, vwinT_ref, bwin_ref, bwin_ref.shape[2], jnp.maximum(qb - (WIN_TILES - 1), 0), False)
    for h in range(hpg):
        o_ref[0, 0, h] += gates[2, h:h + 1, :] * (acc_ref[h] / l_ref[h])


def _bias_lookup(rel_bias, dist):
    onehot = jax.nn.one_hot(_t5_bucket(dist), REL_BUCKETS, dtype=jnp.float32)
    return jnp.einsum('...k,kh->...h', onehot, rel_bias, precision=lax.Precision.HIGHEST)


def _bias_tables(rel_bias, T):
    G, HPG = NSA_KV_HEADS, NSA_HPG
    nc = (T - CMP_BLOCK) // CMP_STRIDE + 1
    ncp = T // CMP_STRIDE

    def lookup(dist, valid):
        b = _bias_lookup(rel_bias, dist)
        b = jnp.where(valid[..., None], b, NEG)
        return jnp.moveaxis(b, -1, 0).reshape((G, HPG) + dist.shape)

    c = jnp.arange(NSA_KB)[:, None]
    i = jnp.arange(NSA_TQ)[None, :]
    sel_d = jnp.stack([dlt + i - c for dlt in (0, NSA_KB, 2 * NSA_KB)])
    bsel = lookup(sel_d, sel_d >= 0)
    win_d = jnp.stack([dlt * NSA_KB + i - c for dlt in range(WIN_TILES)])
    bwin = lookup(win_d, (win_d >= 0) & (win_d <= WINDOW))
    n = jnp.arange(ncp)[:, None]
    t = jnp.arange(T)[None, :]
    cmp_d = t - (n * CMP_STRIDE + CMP_BLOCK - 1)
    bcmp = lookup(cmp_d, (cmp_d >= 0) & (n < nc))
    ns = -(-T // SEL_BLOCK)
    covT = _block_cover(nc, ns).T
    covT = jnp.pad(covT, ((0, 0), (0, ncp - nc))).astype(jnp.bfloat16)
    return bsel, bwin, bcmp, covT


def _nsa_prompt(q, k_sel, v_sel, k_win, v_win, kc, vc, ngt, tables):
    B, T, G, HPG, HD = q.shape
    bsel, bwin, bcmp, covT = tables
    ns, ncp = covT.shape
    assert kc.shape == (B, G, ncp, HD), kc.shape
    bf = jnp.bfloat16
    qT = q.astype(bf).transpose(0, 2, 3, 4, 1)
    ksel = k_sel.astype(bf).transpose(0, 2, 1, 3)
    vselT = v_sel.astype(bf).transpose(0, 2, 3, 1)
    kwin = k_win.astype(bf).transpose(0, 2, 1, 3)
    vwinT = v_win.astype(bf).transpose(0, 2, 3, 1)
    kcp = kc.astype(bf)
    vcT = vc.astype(bf).transpose(0, 1, 3, 2)
    gT = ngt.reshape(B, T, 3, G, HPG).transpose(0, 3, 2, 4, 1)
    tq = NSA_TQ
    full = lambda b, g, i: (b, g, 0, 0)
    oT = pl.pallas_call(
        functools.partial(_nsa_prompt_kernel, n_top=min(N_SELECT, ns)),
        grid=(B, G, T // tq),
        in_specs=[
            pl.BlockSpec((1, 1, HPG, HD, tq), lambda b, g, i: (b, g, 0, 0, i)),
            pl.BlockSpec((1, 1, T, HD), full),
            pl.BlockSpec((1, 1, HD, T), full),
            pl.BlockSpec((1, 1, T, HD), full),
            pl.BlockSpec((1, 1, HD, T), full),
            pl.BlockSpec((1, 1, ncp, HD), full),
            pl.BlockSpec((1, 1, HD, ncp), full),
            pl.BlockSpec((ns, ncp), lambda b, g, i: (0, 0)),
            pl.BlockSpec((1, HPG, ncp, tq), lambda b, g, i: (g, 0, 0, i)),
            pl.BlockSpec((1, HPG) + bsel.shape[2:], lambda b, g, i: (g, 0, 0, 0, 0)),
            pl.BlockSpec((1, HPG) + bwin.shape[2:], lambda b, g, i: (g, 0, 0, 0, 0)),
            pl.BlockSpec((1, 1, 3, HPG, tq), lambda b, g, i: (b, g, 0, 0, i)),
        ],
        out_specs=pl.BlockSpec((1, 1, HPG, HD, tq), lambda b, g, i: (b, g, 0, 0, i)),
        out_shape=jax.ShapeDtypeStruct((B, G, HPG, HD, T), jnp.float32),
        scratch_shapes=[pltpu.VMEM((ns, tq), jnp.float32),
                        pltpu.VMEM((HPG, 1, tq), jnp.float32),
                        pltpu.VMEM((HPG, 1, tq), jnp.float32),
                        pltpu.VMEM((HPG, HD, tq), jnp.float32)],
        compiler_params=pltpu.CompilerParams(
            dimension_semantics=("arbitrary", "arbitrary", "arbitrary"),
            vmem_limit_bytes=V7X_VMEM_LIMIT_BYTES),
        name="nsa_prompt",
    )(qT, ksel, vselT, kwin, vwinT, kcp, vcT, covT, bcmp, bsel, bwin, gT)
    return oT.transpose(0, 4, 1, 2, 3).reshape(B, T, G * HPG * HD)


LIN_TB = 256


def _logsigmoid(x):
    return jnp.minimum(x, 0.0) - jnp.log(1.0 + jnp.exp(-jnp.abs(x)))


def _lin_attn_kernel(*refs, mode, chunk):
    if mode == "gla":
        (q_ref, k_ref, v_ref, r_ref, a_ref, wa_ref, ba_ref, gn_ref, o_ref, st_ref,
         q_s, k_s, g_s, sT_ref) = refs
    else:
        (q_ref, k_ref, v_ref, r_ref, llb_ref, l1m_ref, gn_ref, o_ref, st_ref,
         q_s, k_s, g_s, sT_ref) = refs
    tb, dk = q_s.shape
    C = chunk
    t = pl.program_id(2)

    if mode == "gla":
        q_s[...] = q_ref[0] * (dk ** -0.5)
        k_s[...] = k_ref[0]
        pre = jnp.dot(a_ref[0].astype(jnp.bfloat16), wa_ref[...], preferred_element_type=jnp.float32) + ba_ref[...]
        g_s[...] = _logsigmoid(pre) / GLA_TAU
    else:
        x = q_ref[0]
        q_s[...] = x * jax.nn.sigmoid(x)
        u = llb_ref[...]
        w = l1m_ref[...] + _logsigmoid(k_ref[0])
        lf = jnp.maximum(u, w) + jnp.log(1.0 + jnp.exp(-jnp.abs(u - w)))
        g_s[...] = lf
        k_s[...] = 1.0 - jnp.exp(lf)

    @pl.when(t == 0)
    def _():
        sT_ref[...] = jnp.zeros(sT_ref.shape, jnp.float32)

    rr = lax.broadcasted_iota(jnp.int32, (C, C), 0)
    cc = lax.broadcasted_iota(jnp.int32, (C, C), 1)
    tril = (rr >= cc).astype(jnp.bfloat16)
    row8 = lax.broadcasted_iota(jnp.int32, (SUB, dk), 0)
    cc8 = lax.broadcasted_iota(jnp.int32, (SUB, C), 1)
    gn = gn_ref[...]
    bf = jnp.bfloat16

    def chunk_body(c, carry):
        r0 = pl.multiple_of(c * C, C)
        qc = q_s[pl.ds(r0, C), :]
        kc = k_s[pl.ds(r0, C), :]
        vc = v_ref[0, pl.ds(r0, C), :].astype(bf)
        b = sum(jnp.dot(tril, part, preferred_element_type=jnp.float32) for part in _split3_bf16(g_s[pl.ds(r0, C), :]))
        bl = b[C - 1:C, :]

        nt = C // SUB
        b_t = [b[i * SUB:(i + 1) * SUB] for i in range(nt)]
        q_t = [qc[i * SUB:(i + 1) * SUB] for i in range(nt)]
        att_t = [jnp.zeros((SUB, C), jnp.float32) for _ in range(nt)]
        for s in range(C):
            bs = b[s:s + 1, :]
            ks = kc[s:s + 1, :]
            for i in range(s // SUB, nt):
                d = b_t[i] - bs
                if i == s // SUB and s % SUB:
                    d = jnp.where(row8 >= s % SUB, d, NEG)
                col = jnp.sum(q_t[i] * ks * jnp.exp(d), axis=-1, keepdims=True)
                att_t[i] = jnp.where(cc8 == s, col, att_t[i])
        att = jnp.concatenate(att_t, axis=0)
        sT = sT_ref[...]
        o = (lax.dot_general((qc * jnp.exp(b)).astype(bf), sT.astype(bf), (((1,), (1,)), ((), ())),
                             preferred_element_type=jnp.float32)
             + jnp.dot(att.astype(bf), vc, preferred_element_type=jnp.float32))
        sT_ref[...] = sT * jnp.exp(bl) + lax.dot_general(
            vc, (kc * jnp.exp(bl - b)).astype(bf), (((0,), (0,)), ((), ())), preferred_element_type=jnp.float32)
        y = o * lax.rsqrt(jnp.mean(o * o, axis=-1, keepdims=True) + EPS) * gn
        r = r_ref[0, pl.ds(r0, C), :]
        gate = jax.nn.sigmoid(r)
        if mode == "gla":
            gate = r * gate
        o_ref[0, pl.ds(r0, C), :] = (y * gate).astype(o_ref.dtype)
        return carry

    lax.fori_loop(0, tb // C, chunk_body, 0)

    @pl.when(t == pl.num_programs(2) - 1)
    def _():
        st_ref[0, 0] = sT_ref[...]


def _lin_attn_prompt(z, B, T, mode, params):
    tb = LIN_TB
    z3 = z.reshape(B, T, z.shape[-1])
    if mode == "gla":
        H, dk, dv = GLA_HEADS, GLA_DK, GLA_DV
        wa, ba, gn = params
        in_specs = [
            pl.BlockSpec((1, tb, dk), lambda b, h, t: (b, t, h)),
            pl.BlockSpec((1, tb, dk), lambda b, h, t: (b, t, H + h)),
            pl.BlockSpec((1, tb, dv), lambda b, h, t: (b, t, (2 * H * dk) // dv + h)),
            pl.BlockSpec((1, tb, dv), lambda b, h, t: (b, t, (2 * H * dk + H * dv) // dv + h)),
            pl.BlockSpec((1, tb, LANE), lambda b, h, t: (b, t, (2 * H * dk + 2 * H * dv) // LANE)),
            pl.BlockSpec((LANE, dk), lambda b, h, t: (0, h)),
            pl.BlockSpec((1, dk), lambda b, h, t: (0, h)),
            pl.BlockSpec((1, dv), lambda b, h, t: (0, 0)),
        ]
        args = [z3, z3, z3, z3, z3, wa, ba, gn]
    else:
        H, dk, dv = HGRN_HEADS, HGRN_EXPAND, HGRN_EXPAND
        llb, l1m, gn = params
        in_specs = [
            pl.BlockSpec((1, tb, dk), lambda b, h, t: (b, t, h)),
            pl.BlockSpec((1, tb, dk), lambda b, h, t: (b, t, H + h)),
            pl.BlockSpec((1, tb, dv), lambda b, h, t: (b, t, 2 * H + h)),
            pl.BlockSpec((1, tb, dv), lambda b, h, t: (b, t, 3 * H + h)),
            pl.BlockSpec((1, dk), lambda b, h, t: (0, h)),
            pl.BlockSpec((1, dk), lambda b, h, t: (0, h)),
            pl.BlockSpec((1, dv), lambda b, h, t: (0, 0)),
        ]
        args = [z3, z3, z3, z3, llb, l1m, gn]
    o, sT = pl.pallas_call(
        functools.partial(_lin_attn_kernel, mode=mode, chunk=CHUNK),
        grid=(B, H, T // tb),
        in_specs=in_specs,
        out_specs=[pl.BlockSpec((1, tb, dv), lambda b, h, t: (b, t, h)),
                   pl.BlockSpec((1, 1, dv, dk), lambda b, h, t: (b, h, 0, 0))],
        out_shape=[jax.ShapeDtypeStruct((B, T, H * dv), jnp.bfloat16),
                   jax.ShapeDtypeStruct((B, H, dv, dk), jnp.float32)],
        scratch_shapes=[pltpu.VMEM((tb, dk), jnp.float32), pltpu.VMEM((tb, dk), jnp.float32),
                        pltpu.VMEM((tb, dk), jnp.float32), pltpu.VMEM((dv, dk), jnp.float32)],
        compiler_params=pltpu.CompilerParams(
            dimension_semantics=("arbitrary", "arbitrary", "arbitrary"),
            vmem_limit_bytes=V7X_VMEM_LIMIT_BYTES),
        name="lin_attn_" + mode,
    )(*args)
    return o.reshape(B * T, H * dv), sT.transpose(0, 1, 3, 2)


def _mlstm_kernel(xq_ref, xk_ref, v_ref, og_ref, gi_ref, wq_ref, wk_ref, bq_ref, bk_ref, bibf_ref, gn_ref,
                  o_ref, c_out, n_out, m_out, xe_q, xe_k, q_s, k_s, c_s, n_s, m_s, *, chunk):
    tb, dk = q_s.shape
    C = chunk
    t = pl.program_id(2)
    h = pl.program_id(1)
    W = MLSTM_CONV

    @pl.when(t == 0)
    def _():
        xe_q[0:SUB, :] = jnp.zeros((SUB, dk), jnp.float32)
        xe_k[0:SUB, :] = jnp.zeros((SUB, dk), jnp.float32)
        c_s[...] = jnp.zeros(c_s.shape, jnp.float32)
        n_s[...] = jnp.zeros(n_s.shape, jnp.float32)
        m_s[...] = jnp.zeros(m_s.shape, jnp.float32)

    for xe, x_ref, w_ref, b_ref, dst, scale in ((xe_q, xq_ref, wq_ref, bq_ref, q_s, 1.0),
                                                (xe_k, xk_ref, wk_ref, bk_ref, k_s, dk ** -0.5)):
        xe[SUB:SUB + tb, :] = x_ref[0]
        y = b_ref[...]
        for j in range(W):
            y = y + xe[SUB - (W - 1) + j:SUB - (W - 1) + j + tb, :] * w_ref[j:j + 1, :]
        dst[...] = (y * jax.nn.sigmoid(y)) * scale
        xe[SUB - (W - 1):SUB, :] = xe[SUB + tb - (W - 1):SUB + tb, :]

    lane = lax.broadcasted_iota(jnp.int32, (tb, LANE), 1)
    g = gi_ref[0] + bibf_ref[...]
    i_all = jnp.sum(jnp.where(lane == h, g, 0.0), axis=1, keepdims=True)
    f_all = _logsigmoid(jnp.sum(jnp.where(lane == MLSTM_HEADS + h, g, 0.0), axis=1, keepdims=True))
    rr = lax.broadcasted_iota(jnp.int32, (C, C), 0)
    cc = lax.broadcasted_iota(jnp.int32, (C, C), 1)
    causal = rr >= cc
    eye = rr == cc
    gn = gn_ref[...]
    bf = jnp.bfloat16

    def to_row(col):
        return jnp.sum(jnp.where(eye, col, 0.0), axis=0, keepdims=True)

    m = m_s[0:1, 0:1]
    for c in range(tb // C):
        r0 = c * C
        qc = q_s[r0:r0 + C, :]
        kc = k_s[r0:r0 + C, :]
        vc = v_ref[0, r0:r0 + C, :]
        i_col = i_all[r0:r0 + C]
        f_col = f_all[r0:r0 + C]
        f_row = to_row(f_col)
        F_col = jnp.sum(jnp.where(causal, f_row, 0.0), axis=1, keepdims=True)
        F_row = to_row(F_col)
        i_row = to_row(i_col)
        logw = jnp.where(causal, F_col - F_row + i_row, NEG)
        from_state = F_col + m
        m_hat = jnp.maximum(from_state, jnp.max(logw, axis=1, keepdims=True))
        qk = lax.dot_general(qc.astype(bf), kc.astype(bf), (((1,), (1,)), ((), ())),
                             preferred_element_type=jnp.float32)
        w = jnp.exp(logw - m_hat) * qk
        ws = jnp.exp(from_state - m_hat)
        cs = c_s[...]
        ns = n_s[0:1, :]
        num = (ws * lax.dot_general(qc.astype(bf), cs.astype(bf), (((1,), (1,)), ((), ())),
                                    preferred_element_type=jnp.float32)
               + jnp.dot(w.astype(bf), vc.astype(bf), preferred_element_type=jnp.float32))
        den = ws * jnp.sum(qc * ns, axis=1, keepdims=True) + jnp.sum(w, axis=1, keepdims=True)
        hh = num / jnp.maximum(jnp.abs(den), jnp.exp(-m_hat))
        m_new = m_hat[C - 1:C, :]
        F_last = F_col[C - 1:C, :]
        ds = jnp.exp(F_last - F_col + i_col - m_new)
        dst = jnp.exp(F_last + m - m_new)
        c_s[...] = dst * cs + lax.dot_general((vc * ds).astype(bf), kc.astype(bf), (((0,), (0,)), ((), ())),
                                              preferred_element_type=jnp.float32)
        n_s[...] = jnp.broadcast_to(dst * ns + jnp.sum(ds * kc, axis=0, keepdims=True), n_s.shape)
        m = m_new
        y = hh * lax.rsqrt(jnp.mean(hh * hh, axis=-1, keepdims=True) + EPS) * gn
        o_ref[0, r0:r0 + C, :] = (jax.nn.sigmoid(og_ref[0, r0:r0 + C, :]) * y).astype(o_ref.dtype)
    m_s[...] = jnp.broadcast_to(m, m_s.shape)

    @pl.when(t == pl.num_programs(2) - 1)
    def _():
        c_out[0, 0] = c_s[...]
        n_out[0, 0] = n_s[...]
        m_out[0, 0] = m_s[...]


def _mlstm_prompt(z, B, T, p):
    H, dk, dv = MLSTM_HEADS, MLSTM_DQK, MLSTM_DV
    tb = LIN_TB
    z3 = z.reshape(B, T, z.shape[-1])
    nqk = H * dk
    bibf = jnp.pad(jnp.concatenate([p['m_bi'], p['m_bf']]), (0, LANE - 2 * H))[None]
    wconv, bconv = p['m_wconv'], p['m_bconv'][None]
    o, c_f, n_f, m_f = pl.pallas_call(
        functools.partial(_mlstm_kernel, chunk=CHUNK),
        grid=(B, H, T // tb),
        in_specs=[
            pl.BlockSpec((1, tb, dk), lambda b, h, t: (b, t, h)),
            pl.BlockSpec((1, tb, dk), lambda b, h, t: (b, t, H + h)),
            pl.BlockSpec((1, tb, dv), lambda b, h, t: (b, t, (2 * nqk) // dv + h)),
            pl.BlockSpec((1, tb, dv), lambda b, h, t: (b, t, (2 * nqk + H * dv) // dv + h)),
            pl.BlockSpec((1, tb, LANE), lambda b, h, t: (b, t, (2 * nqk + 2 * H * dv) // LANE)),
            pl.BlockSpec((MLSTM_CONV, dk), lambda b, h, t: (0, h)),
            pl.BlockSpec((MLSTM_CONV, dk), lambda b, h, t: (0, H + h)),
            pl.BlockSpec((1, dk), lambda b, h, t: (0, h)),
            pl.BlockSpec((1, dk), lambda b, h, t: (0, H + h)),
            pl.BlockSpec((1, LANE), lambda b, h, t: (0, 0)),
            pl.BlockSpec((1, dv), lambda b, h, t: (0, 0)),
        ],
        out_specs=[pl.BlockSpec((1, tb, dv), lambda b, h, t: (b, t, h)),
                   pl.BlockSpec((1, 1, dv, dk), lambda b, h, t: (b, h, 0, 0)),
                   pl.BlockSpec((1, 1, SUB, dk), lambda b, h, t: (b, h, 0, 0)),
                   pl.BlockSpec((1, 1, SUB, LANE), lambda b, h, t: (b, h, 0, 0))],
        out_shape=[jax.ShapeDtypeStruct((B, T, H * dv), jnp.bfloat16),
                   jax.ShapeDtypeStruct((B, H, dv, dk), jnp.float32),
                   jax.ShapeDtypeStruct((B, H, SUB, dk), jnp.float32),
                   jax.ShapeDtypeStruct((B, H, SUB, LANE), jnp.float32)],
        scratch_shapes=[pltpu.VMEM((SUB + tb, dk), jnp.float32), pltpu.VMEM((SUB + tb, dk), jnp.float32),
                        pltpu.VMEM((tb, dk), jnp.float32), pltpu.VMEM((tb, dk), jnp.float32),
                        pltpu.VMEM((dv, dk), jnp.float32), pltpu.VMEM((SUB, dk), jnp.float32),
                        pltpu.VMEM((SUB, LANE), jnp.float32)],
        compiler_params=_cparams(3),
        name="mlstm_prompt",
    )(z3, z3, z3, z3, z3, wconv, wconv, bconv, bconv, bibf, p['m_gn'][None])
    return o.reshape(B * T, H * dv), c_f, n_f[:, :, 0, :], m_f[:, :, 0, 0]


CMP_PAGES = 16
CMP_PER_PAGE = PAGE_SIZE // CMP_STRIDE
HEAD_ROWS = SUB


KV_ROWS = 4 * NSA_KV_HEADS


def _compress_blocks(load, kind, w1_ref, w2_ref, pe_ref, gk_ref, nblk):
    acc = jnp.zeros((nblk, HEAD_DIM), jnp.float32)
    for j in range(0, CMP_BLOCK, 2):
        xa = load(j) + pe_ref[kind, j:j + 1, :]
        xb = load(j + 1) + pe_ref[kind, j + 1:j + 2, :]
        x2 = jnp.concatenate([xa, xb], axis=1).astype(jnp.bfloat16)
        acc = acc + jnp.dot(x2, w1_ref[kind, j * HEAD_DIM:(j + 2) * HEAD_DIM, :], preferred_element_type=jnp.float32)
    y = jnp.dot((acc * jax.nn.sigmoid(acc)).astype(jnp.bfloat16), w2_ref[kind], preferred_element_type=jnp.float32)
    if kind == 0:
        y = y * lax.rsqrt(jnp.mean(y * y, axis=-1, keepdims=True) + EPS) * gk_ref[...]
    return y


def _cmp_sample_kernel(pt_ref, *refs):
    del pt_ref
    pages = refs[:CMP_PAGES + 1]
    w1_ref, w2_ref, pe_ref, gk_ref, o_ref, xs_ref = refs[CMP_PAGES + 1:]
    rows = PAGE_SIZE * KV_ROWS
    for u in range(CMP_PAGES + 1):
        xs_ref[u * rows:(u + 1) * rows, :] = pages[u][...]
    nblk = CMP_PAGES * CMP_PER_PAGE
    for c in range(2 * NSA_KV_HEADS):
        load = functools.partial(lambda j, c: xs_ref[pl.ds(KV_ROWS * j + c, nblk, stride=KV_ROWS * CMP_STRIDE), :], c=c)
        o_ref[0, c] = _compress_blocks(load, c // NSA_KV_HEADS, w1_ref, w2_ref, pe_ref, gk_ref, nblk)


def _cmp_prompt_kernel(*refs):
    srcs = refs[:2 * NSA_KV_HEADS]
    w1_ref, w2_ref, pe_ref, gk_ref, o_ref, xs_ref = refs[2 * NSA_KV_HEADS:]
    T = srcs[0].shape[1]
    nblk = T // CMP_STRIDE
    xs_ref[T:, :] = jnp.zeros((xs_ref.shape[0] - T, HEAD_DIM), jnp.float32)
    load = lambda j: xs_ref[pl.ds(j, nblk, stride=CMP_STRIDE), :]
    for c in range(2 * NSA_KV_HEADS):
        xs_ref[0:T, :] = srcs[c][0]
        o_ref[0, c] = _compress_blocks(load, c // NSA_KV_HEADS, w1_ref, w2_ref, pe_ref, gk_ref, nblk)


def _cmp_prompt(z3, w1, w2, pe, gk):
    B, T, _ = z3.shape
    G = NSA_KV_HEADS
    first = NSA_HEADS
    full = lambda *shape: pl.BlockSpec(shape, lambda b: (0,) * len(shape))
    return pl.pallas_call(
        _cmp_prompt_kernel,
        grid=(B,),
        in_specs=[pl.BlockSpec((1, T, HEAD_DIM), functools.partial(lambda b, c: (b, 0, first + c), c=c))
                  for c in range(2 * G)] + [
            full(2, CMP_BLOCK * HEAD_DIM, HEAD_DIM), full(2, HEAD_DIM, HEAD_DIM), full(2, CMP_BLOCK, HEAD_DIM),
            full(1, HEAD_DIM)],
        out_specs=pl.BlockSpec((1, 2 * G, T // CMP_STRIDE, HEAD_DIM), lambda b: (b, 0, 0, 0)),
        out_shape=jax.ShapeDtypeStruct((B, 2 * G, T // CMP_STRIDE, HEAD_DIM), jnp.float32),
        scratch_shapes=[pltpu.VMEM((T + PAGE_SIZE, HEAD_DIM), jnp.float32)],
        compiler_params=_cparams(1),
        name="nsa_cmp_prompt",
    )(*([z3] * (2 * G)), w1.astype(jnp.bfloat16), w2.astype(jnp.bfloat16), pe, gk[None])


def _cmp_sample(cache2d, page_table, layer_off, w1, w2, pe, gk):
    B, n_pages = page_table.shape
    G = NSA_KV_HEADS
    assert n_pages % CMP_PAGES == 0
    nblk = CMP_PAGES * CMP_PER_PAGE
    rows = PAGE_SIZE * KV_ROWS

    def page_spec(u):
        def imap(b, grp, pt):
            page = jnp.minimum(grp * CMP_PAGES + u, n_pages - 1)
            return (layer_off + pt[b * n_pages + page], 0)
        return pl.BlockSpec((rows, HEAD_DIM), imap)

    full = lambda *shape: pl.BlockSpec(shape, lambda b, grp, pt: (0,) * len(shape))
    return pl.pallas_call(
        _cmp_sample_kernel,
        grid_spec=pltpu.PrefetchScalarGridSpec(
            num_scalar_prefetch=1,
            grid=(B, n_pages // CMP_PAGES),
            in_specs=[page_spec(u) for u in range(CMP_PAGES + 1)] + [
                full(2, CMP_BLOCK * HEAD_DIM, HEAD_DIM), full(2, HEAD_DIM, HEAD_DIM),
                full(2, CMP_BLOCK, HEAD_DIM), full(1, HEAD_DIM)],
            out_specs=pl.BlockSpec((1, 2 * G, nblk, HEAD_DIM), lambda b, grp, pt: (b, 0, grp, 0)),
            scratch_shapes=[pltpu.VMEM(((CMP_PAGES + 1) * rows, HEAD_DIM), jnp.float32)],
        ),
        out_shape=jax.ShapeDtypeStruct((B, 2 * G, n_pages * CMP_PER_PAGE, HEAD_DIM), jnp.float32),
        compiler_params=_cparams(2),
        name="nsa_cmp_sample",
    )(page_table.reshape(-1), *([cache2d] * (CMP_PAGES + 1)),
      w1.astype(jnp.bfloat16), w2.astype(jnp.bfloat16), pe, gk[None])


def _topk_sample_kernel(q_ref, kc_ref, vc_ref, bias_ref, cov_ref, ocmp_ref, top_ref, *, cur, n_top):
    nsp = cov_ref.shape[1]
    q = q_ref[0, 0]
    bias = bias_ref[0]
    s = lax.dot_general(q, kc_ref[0, 0].astype(jnp.bfloat16), (((1,), (1,)), ((), ())),
                        preferred_element_type=jnp.float32) + bias
    e = jnp.exp(s - jnp.max(s, axis=-1, keepdims=True))
    p = jnp.where(bias > 0.5 * NEG, e / jnp.sum(e, axis=-1, keepdims=True), 0.0)
    ocmp_ref[0, 0] = jnp.dot(p.astype(jnp.bfloat16), vc_ref[0, 0].astype(jnp.bfloat16),
                             preferred_element_type=jnp.float32)
    head = lax.broadcasted_iota(jnp.int32, p.shape, 0)
    p_sum = jnp.sum(jnp.where(head < NSA_HPG, p, 0.0), axis=0, keepdims=True)
    p_sum = jnp.broadcast_to(p_sum, (SUB, p_sum.shape[1]))
    cov = cov_ref[...]
    imp = sum(jnp.dot(part, cov, preferred_element_type=jnp.float32) for part in _split3_bf16(p_sum))[0:1]
    blk = lax.broadcasted_iota(jnp.int32, (1, nsp), 1)
    forced = (blk == 0) | (blk == cur) | (blk == cur - 1)
    score = jnp.where(forced, BIG, jnp.where(blk <= cur, imp, -BIG))
    score = jnp.where(blk <= cur, score, -2.0 * BIG)
    ii = lax.broadcasted_iota(jnp.int32, (nsp, nsp), 0)
    jj = lax.broadcasted_iota(jnp.int32, (nsp, nsp), 1)
    eye = ii == jj

    def to_col(row):
        return jnp.sum(jnp.where(eye, row, 0.0), axis=1, keepdims=True)

    col = to_col(score)
    beats = (col > score) | ((col == score) & (ii < jj))
    rank = jnp.sum(beats.astype(jnp.float32), axis=0, keepdims=True)
    sel = jnp.where((rank < n_top) & (blk < cur), 1.0, 0.0)
    pos = jnp.sum(jnp.where(ii < jj, to_col(sel), 0.0), axis=0, keepdims=True)
    lane = lax.broadcasted_iota(jnp.int32, (1, LANE), 1)
    out = jnp.zeros((1, LANE), jnp.float32)
    blk_f = blk.astype(jnp.float32)
    for k in range(n_top - 1):
        idx = jnp.sum(jnp.where((sel > 0.5) & (pos == k), blk_f, 0.0), axis=1, keepdims=True)
        out = jnp.where(lane == k, idx, out)
    top_ref[0, 0] = jnp.broadcast_to(out, (SUB, LANE)).astype(jnp.int32)


def _topk_sample(q8, kcvc, bias_c, cov, cur):
    B, G = q8.shape[:2]
    nc = kcvc.shape[2]
    nsp = cov.shape[1]
    return pl.pallas_call(
        functools.partial(_topk_sample_kernel, cur=cur, n_top=N_SELECT),
        grid=(B, G),
        in_specs=[pl.BlockSpec((1, 1, HEAD_ROWS, HEAD_DIM), lambda b, g: (b, g, 0, 0)),
                  pl.BlockSpec((1, 1, nc, HEAD_DIM), lambda b, g: (b, g, 0, 0)),
                  pl.BlockSpec((1, 1, nc, HEAD_DIM), lambda b, g: (b, G + g, 0, 0)),
                  pl.BlockSpec((1, HEAD_ROWS, nc), lambda b, g: (g, 0, 0)),
                  pl.BlockSpec((nc, nsp), lambda b, g: (0, 0))],
        out_specs=[pl.BlockSpec((1, 1, HEAD_ROWS, HEAD_DIM), lambda b, g: (b, g, 0, 0)),
                   pl.BlockSpec((1, 1, SUB, LANE), lambda b, g: (b, g, 0, 0))],
        out_shape=[jax.ShapeDtypeStruct((B, G, HEAD_ROWS, HEAD_DIM), jnp.float32),
                   jax.ShapeDtypeStruct((B, G, SUB, LANE), jnp.int32)],
        compiler_params=_cparams(2),
        name="nsa_topk_sample",
    )(q8, kcvc, kcvc, bias_c, cov)


def _sel_sample_kernel(pt_ref, top_ref, q_ref, blk_ref, new_ref, kw_ref, vw_ref, ocmp_ref,
                       bsel_ref, bwin_ref, b0_ref, gate_ref, o_ref, m_ref, l_ref, acc_ref, *, cur):
    del pt_ref
    b, g, k = pl.program_id(0), pl.program_id(1), pl.program_id(2)
    nk = pl.num_programs(2)
    q = q_ref[0, 0]
    qf = q.astype(jnp.float32)
    b0 = b0_ref[0]
    bf = jnp.bfloat16

    @pl.when(k == 0)
    def _():
        m_ref[...] = jnp.sum(qf * new_ref[0, 0, 0:1, :], axis=-1, keepdims=True) + b0
        l_ref[...] = jnp.ones(l_ref.shape, jnp.float32)
        acc_ref[...] = jnp.broadcast_to(new_ref[0, 0, 1:2, :], acc_ref.shape)

    j = top_ref[(b * NSA_KV_HEADS + g) * LANE + k]
    tile = jnp.minimum(cur - j, bsel_ref.shape[1]) - 1
    def group_rows(kind):
        out = None
        for gi in range(NSA_KV_HEADS):
            r = blk_ref[pl.ds(kind * NSA_KV_HEADS + gi, SEL_BLOCK, stride=KV_ROWS), :]
            out = r if out is None else jnp.where(g == gi, r, out)
        return out.astype(bf)

    s = lax.dot_general(q, group_rows(2), (((1,), (1,)), ((), ())),
                        preferred_element_type=jnp.float32) + bsel_ref[0, tile]
    m_old = m_ref[...]
    m_new = jnp.maximum(m_old, jnp.max(s, axis=-1, keepdims=True))
    alpha = jnp.exp(m_old - m_new)
    p = jnp.exp(s - m_new)
    l_ref[...] = alpha * l_ref[...] + jnp.sum(p, axis=-1, keepdims=True)
    acc_ref[...] = alpha * acc_ref[...] + jnp.dot(p.astype(bf), group_rows(3),
                                                  preferred_element_type=jnp.float32)
    m_ref[...] = m_new

    @pl.when(k == nk - 1)
    def _():
        gates = jax.nn.sigmoid(gate_ref[0, 0])
        o_sel = acc_ref[...] / l_ref[...]
        sw = lax.dot_general(q, kw_ref[0].astype(bf), (((1,), (1,)), ((), ())),
                             preferred_element_type=jnp.float32) + bwin_ref[0]
        sn = jnp.sum(qf * new_ref[0, 0, 2:3, :], axis=-1, keepdims=True) + b0
        mw = jnp.maximum(jnp.max(sw, axis=-1, keepdims=True), sn)
        pw = jnp.exp(sw - mw)
        pn = jnp.exp(sn - mw)
        o_w = (jnp.dot(pw.astype(bf), vw_ref[0].astype(bf), preferred_element_type=jnp.float32)
               + pn * new_ref[0, 0, 3:4, :]) / (jnp.sum(pw, axis=-1, keepdims=True) + pn)
        o_ref[0, 0] = gates[0] * ocmp_ref[0, 0] + gates[1] * o_sel + gates[2] * o_w


def _sel_sample(cache2d, page_table, layer_off, top, q8, new_rows, win_rows, ocmp, bsel, bwin, b0, gate8, cur):
    B, n_pages = page_table.shape
    G = NSA_KV_HEADS
    per_page = PAGE_SIZE // SEL_BLOCK
    wb = win_rows.shape[1]

    def sel_map(b, g, k, pt, tp):
        j = tp[(b * G + g) * LANE + k]
        page = layer_off + pt[b * n_pages + j // per_page]
        return (page * per_page + j % per_page, 0)

    sel_spec = pl.BlockSpec((SEL_BLOCK * KV_ROWS, HEAD_DIM), sel_map)

    bg = lambda b, g, k, pt, tp: (b, g, 0, 0)
    return pl.pallas_call(
        functools.partial(_sel_sample_kernel, cur=cur),
        grid_spec=pltpu.PrefetchScalarGridSpec(
            num_scalar_prefetch=2,
            grid=(B, G, N_SELECT - 1),
            in_specs=[
                pl.BlockSpec((1, 1, HEAD_ROWS, HEAD_DIM), bg),
                sel_spec,
                pl.BlockSpec((1, 1, SUB, HEAD_DIM), bg),
                pl.BlockSpec((1, wb, HEAD_DIM), lambda b, g, k, pt, tp: (b, 0, g)),
                pl.BlockSpec((1, wb, HEAD_DIM), lambda b, g, k, pt, tp: (b, 0, G + g)),
                pl.BlockSpec((1, 1, HEAD_ROWS, HEAD_DIM), bg),
                pl.BlockSpec((1,) + bsel.shape[1:], lambda b, g, k, pt, tp: (g, 0, 0, 0)),
                pl.BlockSpec((1,) + bwin.shape[1:], lambda b, g, k, pt, tp: (g, 0, 0)),
                pl.BlockSpec((1,) + b0.shape[1:], lambda b, g, k, pt, tp: (g, 0, 0)),
                pl.BlockSpec((1, 1, 3, HEAD_ROWS, HEAD_DIM), lambda b, g, k, pt, tp: (b, g, 0, 0, 0)),
            ],
            out_specs=pl.BlockSpec((1, 1, HEAD_ROWS, HEAD_DIM), bg),
            scratch_shapes=[pltpu.VMEM((HEAD_ROWS, 1), jnp.float32), pltpu.VMEM((HEAD_ROWS, 1), jnp.float32),
                            pltpu.VMEM((HEAD_ROWS, HEAD_DIM), jnp.float32)],
        ),
        out_shape=jax.ShapeDtypeStruct((B, G, HEAD_ROWS, HEAD_DIM), jnp.float32),
        compiler_params=_cparams(3),
        name="nsa_sel_sample",
    )(page_table.reshape(-1), top.reshape(-1), q8, cache2d, new_rows, win_rows, win_rows,
      ocmp, bsel, bwin, b0, gate8)


def _nsa_sample(z_nsa, cache_l, layer_off, page_table, win_l, p, rel_bias):
    B, n_pages = page_table.shape
    G, HPG, HD = NSA_KV_HEADS, NSA_HPG, HEAD_DIM
    past_len = n_pages * PAGE_SIZE
    cur = past_len // SEL_BLOCK
    nq = z_nsa[:, :G * HPG * HD].reshape(B, G, HPG, HD)
    kv = z_nsa[:, G * HPG * HD:G * HPG * HD + 6 * G * HD].reshape(B, 6, G, HD)
    ngt = z_nsa[:, G * HPG * HD + 6 * G * HD:][:, :3 * G * HPG].reshape(B, 3, G, HPG)
    q = _rmsnorm(nq, p['nsa_gq']) * (HD ** -0.5)
    rows_new = jnp.stack([kv[:, 0], kv[:, 1], _rmsnorm(kv[:, 2], p['nsa_gk'][1]), kv[:, 3]], axis=1)
    win_new = jnp.stack([_rmsnorm(kv[:, 4], p['nsa_gk'][2]), kv[:, 5]], axis=1)
    pad_heads = lambda a: jnp.pad(a, ((0, 0), (0, 0), (0, HEAD_ROWS - HPG), (0, 0)))
    q8 = pad_heads(q).astype(jnp.bfloat16)
    new_rows = jnp.stack([rows_new[:, 2], rows_new[:, 3], win_new[:, 0], win_new[:, 1]], axis=2)
    new_rows = jnp.pad(new_rows, ((0, 0), (0, 0), (0, SUB - 4), (0, 0)))
    gate8 = jnp.pad(ngt.transpose(0, 2, 1, 3)[..., None], ((0, 0), (0, 0), (0, 0), (0, HEAD_ROWS - HPG), (0, 0)))
    gate8 = jnp.broadcast_to(gate8, (B, G, 3, HEAD_ROWS, HD))

    def lookup(dist, valid):
        t = jnp.where(valid[..., None], _bias_lookup(rel_bias, dist), NEG)
        t = jnp.moveaxis(t, -1, 0).reshape((G, HPG) + dist.shape)
        return jnp.pad(t, ((0, 0), (0, HEAD_ROWS - HPG)) + ((0, 0),) * dist.ndim)

    nc = n_pages * CMP_PER_PAGE
    n = jnp.arange(nc)
    bias_c = lookup(past_len - (n * CMP_STRIDE + CMP_BLOCK - 1), n < nc - 1)
    ns = cur + 1
    nsp = _round_up(ns, LANE)
    cov = jnp.pad(_block_cover(nc - 1, ns), ((0, 1), (0, nsp - ns))).astype(jnp.bfloat16)
    jj = jnp.arange(SEL_BLOCK)
    sel_d = jnp.stack([SEL_BLOCK * d - jj for d in (1, 2, 3)])
    bsel = lookup(sel_d, sel_d >= 0).transpose(0, 2, 1, 3)
    wb = win_l.shape[1]
    bwin = lookup(wb - jnp.arange(wb), jnp.ones((wb,), bool))
    b0 = lookup(jnp.zeros((1,), jnp.int32), jnp.ones((1,), bool))

    cache2d = cache_l.reshape(-1, HD)
    kcvc = _cmp_sample(cache2d, page_table, layer_off, p['cmp_w1'], p['cmp_w2'], p['cmp_pe'], p['nsa_gk'][0])
    ocmp, top = _topk_sample(q8, kcvc, bias_c, cov, cur)
    win_rows = win_l.reshape(B, wb, 2 * G * HD)
    o = _sel_sample(cache2d, page_table, layer_off, top[:, :, 0, :], q8, new_rows, win_rows, ocmp,
                    bsel, bwin, b0, gate8, cur)
    o_nsa = o[:, :, :HPG, :].reshape(B, G * HPG * HD)
    win_state = jnp.concatenate([win_l[:, 1:], win_new[:, None]], axis=1)
    return o_nsa, rows_new[:, None], win_state


def _split(z, sizes):
    cuts = [int(c) for c in np.cumsum(sizes)[:-1]]
    return jnp.split(z, cuts, axis=-1)


def _rmsnorm(x, g):
    xf = x.astype(jnp.float32)
    y = xf * lax.rsqrt(jnp.mean(xf * xf, axis=-1, keepdims=True) + EPS)
    return (y * g.astype(jnp.float32)).astype(x.dtype)


def _masked_softmax(s, mask):
    s = jnp.where(mask, s.astype(jnp.float32), NEG)
    return jnp.where(mask, jax.nn.softmax(s, axis=-1), 0.0)


def _t5_bucket(dist):
    d = jnp.maximum(dist, 0)
    exact = REL_BUCKETS // 2
    far = exact + (jnp.log(jnp.maximum(d, 1).astype(jnp.float32) / exact)
                   / math.log(REL_MAX_DIST / exact) * (REL_BUCKETS - exact)).astype(jnp.int32)
    return jnp.where(d < exact, d, jnp.minimum(far, REL_BUCKETS - 1))


def _nsa_compress(k_rows, v_rows, pe, w1, w2, g_kc):
    B, L, G, _ = k_rows.shape
    nc = (L - CMP_BLOCK) // CMP_STRIDE + 1
    idx = jnp.arange(nc)[:, None] * CMP_STRIDE + jnp.arange(CMP_BLOCK)[None, :]

    def phi(rows, pe_, w1_, w2_):
        blk = rows[:, idx] + pe_[None, None, :, None, :]
        flat = blk.transpose(0, 1, 3, 2, 4).reshape(B, nc, G, CMP_BLOCK * HEAD_DIM)
        return jax.nn.silu(flat @ w1_) @ w2_

    kc = _rmsnorm(phi(k_rows, pe[0], w1[0], w2[0]), g_kc)
    vc = phi(v_rows, pe[1], w1[1], w2[1])
    c_end = jnp.arange(nc) * CMP_STRIDE + CMP_BLOCK - 1
    return kc, vc, c_end


def _block_cover(nc, ns):
    c0 = jnp.arange(nc)[:, None] * CMP_STRIDE
    s0 = jnp.arange(ns)[None, :] * SEL_BLOCK
    return ((c0 <= s0 + SEL_BLOCK - 1) & (c0 + CMP_BLOCK - 1 >= s0)).astype(jnp.float32)


def _nsa_cmp_sel(q, q_pos, kc, vc, c_end, ks_t, vs_t, rel_bias, cover):
    B, Tq, G, HPG, _ = q.shape
    L = ks_t.shape[2]
    ns = cover.shape[1]
    s_c = jnp.einsum('btghd,bngd->btghn', q, kc).astype(jnp.float32)
    dist_c = q_pos[:, None] - c_end[None, :]
    bias_c = rel_bias[_t5_bucket(dist_c)].reshape(Tq, -1, G, HPG).transpose(0, 2, 3, 1)
    p_c = _masked_softmax(s_c + bias_c, (dist_c >= 0)[:, None, None, :])
    o_cmp = jnp.einsum('btghn,bngd->btghd', p_c.astype(vc.dtype), vc)
    imp = jnp.einsum('btgn,ns->btgs', p_c.sum(axis=3), cover)
    blk = jnp.arange(ns)[None, :]
    cur = (q_pos // SEL_BLOCK)[:, None]
    forced = ((blk == 0) | (blk == cur) | (blk == cur - 1))[:, None, :]
    valid = (blk <= cur)[:, None, :]
    score = jnp.where(forced, BIG, jnp.where(valid, imp, -BIG))
    n_top = min(N_SELECT, ns)
    _, top = lax.top_k(score, n_top)
    tok = (top[..., None] * SEL_BLOCK + jnp.arange(SEL_BLOCK)).reshape(B, Tq, G, n_top * SEL_BLOCK)
    tok_c = jnp.minimum(tok, L - 1)
    bi = jnp.arange(B)[:, None, None, None]
    gi = jnp.arange(G)[None, None, :, None]
    kg = ks_t[bi, gi, tok_c]
    vg = vs_t[bi, gi, tok_c]
    s_s = jnp.einsum('btghd,btgsd->btghs', q, kg).astype(jnp.float32)
    dist_s = q_pos[None, :, None, None] - tok
    bias_s = rel_bias.reshape(REL_BUCKETS, G, HPG)[_t5_bucket(dist_s), gi]
    p_s = _masked_softmax(s_s + bias_s.transpose(0, 1, 2, 4, 3), (dist_s >= 0)[:, :, :, None, :])
    o_sel = jnp.einsum('btghs,btgsd->btghd', p_s.astype(vg.dtype), vg)
    return o_cmp, o_sel


def _window_attend(q, q_pos, k, v, k_pos, rel_bias):
    N, Qb = q_pos.shape
    Kb = k_pos.shape[1]
    G, HPG = q.shape[3], q.shape[4]
    s = jnp.einsum('bnqghd,bnkgd->bnqghk', q, k).astype(jnp.float32)
    dist = q_pos[:, :, None] - k_pos[:, None, :]
    bias = rel_bias[_t5_bucket(dist)].reshape(N, Qb, Kb, G, HPG).transpose(0, 1, 3, 4, 2)
    mask = ((dist >= 0) & (dist <= WINDOW) & (k_pos[:, None, :] >= 0))[:, :, None, None, :]
    p = _masked_softmax(s + bias, mask)
    return jnp.einsum('bnqghk,bnkgd->bnqghd', p.astype(v.dtype), v)


def _gated_linear_attn(q, k, v, log_a, s0):
    B, T, H, _ = q.shape
    dv = v.shape[-1]
    C = CHUNK if T % CHUNK == 0 else T
    n = T // C

    def chunks(a):
        return a.astype(jnp.float32).reshape(B, n, C, H, a.shape[-1]).transpose(1, 0, 3, 2, 4)

    causal = jnp.tril(jnp.ones((C, C), bool))

    def step(S, inp):
        qc, kc, vc, gc = inp
        b = jnp.cumsum(gc, axis=2)
        diff = jnp.where(causal[:, :, None], b[:, :, :, None, :] - b[:, :, None, :, :], -jnp.inf)
        att = jnp.einsum('bhtk,bhsk,bhtsk->bhts', qc, kc, jnp.exp(diff))
        o = (jnp.einsum('bhtk,bhkv->bhtv', qc * jnp.exp(b), S)
             + jnp.einsum('bhts,bhsv->bhtv', att, vc))
        b_last = b[:, :, -1:, :]
        S = (jnp.exp(b_last[:, :, 0, :])[..., None] * S
             + jnp.einsum('bhsk,bhsv->bhkv', kc * jnp.exp(b_last - b), vc))
        return S, o

    S, o = lax.scan(step, s0.astype(jnp.float32), (chunks(q), chunks(k), chunks(v), chunks(log_a)))
    o = o.transpose(1, 0, 3, 2, 4).reshape(B, T, H, dv)
    return o.astype(v.dtype), S.astype(s0.dtype)


def _mlstm(q, k, v, ig, lf, C0, n0, m0):
    B, T, H, _ = q.shape
    dv = v.shape[-1]
    C = CHUNK if T % CHUNK == 0 else T
    n = T // C

    def chunks(a):
        return a.astype(jnp.float32).reshape(B, n, C, H, a.shape[-1]).transpose(1, 0, 3, 2, 4)

    def gchunks(a):
        return a.astype(jnp.float32).reshape(B, n, C, H).transpose(1, 0, 3, 2)

    causal = jnp.tril(jnp.ones((C, C), bool))

    def step(carry, inp):
        Cs, ns_, m = carry
        qc, kc, vc, ic, fc = inp
        F = jnp.cumsum(fc, axis=-1)
        logw = jnp.where(causal, F[..., :, None] - F[..., None, :] + ic[..., None, :], -jnp.inf)
        from_state = F + m[..., None]
        m_hat = jnp.maximum(from_state, logw.max(-1))
        w = jnp.exp(logw - m_hat[..., None]) * jnp.einsum('bhtk,bhsk->bhts', qc, kc)
        ws = jnp.exp(from_state - m_hat)
        num = ws[..., None] * jnp.einsum('bhtk,bhvk->bhtv', qc, Cs) + jnp.einsum('bhts,bhsv->bhtv', w, vc)
        den = ws * jnp.einsum('bhtk,bhk->bht', qc, ns_) + w.sum(-1)
        h = num / jnp.maximum(jnp.abs(den), jnp.exp(-m_hat))[..., None]
        m_new = m_hat[..., -1]
        ds = jnp.exp(F[..., -1:] - F + ic - m_new[..., None])
        dst = jnp.exp(F[..., -1] + m - m_new)
        Cs = dst[..., None, None] * Cs + jnp.einsum('bhs,bhsv,bhsk->bhvk', ds, vc, kc)
        ns_ = dst[..., None] * ns_ + jnp.einsum('bhs,bhsk->bhk', ds, kc)
        return (Cs, ns_, m_new), h

    carry0 = (C0.astype(jnp.float32), n0.astype(jnp.float32), m0.astype(jnp.float32))
    (Cf, nf, mf), h = lax.scan(step, carry0, (chunks(q), chunks(k), chunks(v), gchunks(ig), gchunks(lf)))
    h = h.transpose(1, 0, 3, 2, 4).reshape(B, T, H, dv)
    return h.astype(v.dtype), Cf.astype(C0.dtype), nf.astype(n0.dtype), mf.astype(m0.dtype)


def _causal_conv(u, buf, w, b):
    T = u.shape[1]
    up = jnp.concatenate([buf, u], axis=1)
    y = b
    for j in range(MLSTM_CONV):
        y = y + up[:, j:j + T] * w[j]
    return jax.nn.silu(y), up[:, -(MLSTM_CONV - 1):]


GROUPS = {
    'nsa': (('nq', 'nkv', 'ngt'), 3072),
    'gla': (('gq', 'gk', 'gv', 'gr', 'ga'), 3584),
    'hgrn': (('hq', 'hf', 'hi', 'hg'), 4096),
    'mlstm': (('mqk', 'mv', 'mo', 'mi', 'mf'), 3584),
    'gate': (('mg',), N_BRANCH * D_MODEL),
}
IN_NAMES = ('nq', 'nkv', 'ngt', 'gq', 'gk', 'gv', 'ga', 'gr', 'hq', 'hf', 'hi', 'hg',
            'mqk', 'mv', 'mi', 'mf', 'mo', 'mg')
IN_WIDTH = dict(zip(IN_NAMES, IN_SIZES))


def _group_weights(w_in_l):
    cuts = dict(zip(IN_NAMES, np.cumsum((0,) + IN_SIZES[:-1])))
    out = {}
    for gname, (members, width) in GROUPS.items():
        w = jnp.concatenate([w_in_l[:, int(cuts[m]):int(cuts[m]) + IN_WIDTH[m]] for m in members], axis=1)
        w = w.astype(jnp.bfloat16)
        out[gname] = jnp.pad(w, ((0, 0), (0, width - w.shape[1])))
    return out


def _group_fields(zs, gname, B, T):
    out, off = {}, 0
    for m in GROUPS[gname][0]:
        out[m] = zs[gname][:, off:off + IN_WIDTH[m]].reshape(B, T, IN_WIDTH[m])
        off += IN_WIDTH[m]
    return out


def _token_mixers(zs, B, T, pos0, past, p, rel_bias, lb, tables):
    G, HPG, HD = NSA_KV_HEADS, NSA_HPG, HEAD_DIM
    f = _group_fields(zs, 'nsa', B, T)
    nq, nkv, ngt = f['nq'], f['nkv'], f['ngt']
    f = _group_fields(zs, 'mlstm', B, T)
    mqk, mv, mo, mi, mf = f['mqk'], f['mv'], f['mo'], f['mi'], f['mf']
    q_pos = pos0 + jnp.arange(T)

    if past is None:
        q = _rmsnorm(nq.reshape(B, T, G, HPG, HD), p['nsa_gq']) * (HD ** -0.5)
        kv = nkv.reshape(B, T, 6, G, HD)
        rows_new = jnp.stack([kv[:, :, 0], kv[:, :, 1], _rmsnorm(kv[:, :, 2], p['nsa_gk'][1]), kv[:, :, 3]], axis=2)
        win_new = jnp.stack([_rmsnorm(kv[:, :, 4], p['nsa_gk'][2]), kv[:, :, 5]], axis=2)
        kcvc = _cmp_prompt(zs['nsa'].reshape(B, T, -1), p['cmp_w1'], p['cmp_w2'], p['cmp_pe'], p['nsa_gk'][0])
        o_nsa = _nsa_prompt(q, rows_new[:, :, 2], rows_new[:, :, 3], win_new[:, :, 0], win_new[:, :, 1],
                            kcvc[:, :G], kcvc[:, G:], ngt, tables)
        win_state = win_new[:, -min(WINDOW, T):]
    else:
        o_nsa, rows_new, win_state = _nsa_sample(zs['nsa'], past['cache'], past['layer_off'], past['page_table'],
                                                 past['win'], p, rel_bias)

    if past is None:
        wa_p = jnp.pad(p['gla_wa'], ((0, LANE - GLA_GATE_RANK), (0, 0))).astype(jnp.bfloat16)
        o_gla, gla_state = _lin_attn_prompt(zs['gla'], B, T, "gla", (wa_p, p['gla_ba'][None], p['gla_gn'][None]))
        o_hgrn, hgrn_state = _lin_attn_prompt(zs['hgrn'], B, T, "hgrn",
                                              (jnp.log(lb)[None], jnp.log1p(-lb)[None], p['hgrn_gn'][None]))
    else:
        f = _group_fields(zs, 'gla', B, T)
        gq, gk, gv, gr, ga = f['gq'], f['gk'], f['gv'], f['gr'], f['ga']
        g_q = gq.reshape(B, T, GLA_HEADS, GLA_DK) * (GLA_DK ** -0.5)
        g_k = gk.reshape(B, T, GLA_HEADS, GLA_DK)
        g_v = gv.reshape(B, T, GLA_HEADS, GLA_DV)
        log_a = (jax.nn.log_sigmoid((ga @ p['gla_wa'] + p['gla_ba']).astype(jnp.float32)) / GLA_TAU).reshape(B, T, GLA_HEADS, GLA_DK)
        o, gla_state = _gated_linear_attn(g_q, g_k, g_v, log_a, past['gla'])
        o_gla = (_rmsnorm(o, p['gla_gn']) * jax.nn.silu(gr.reshape(B, T, GLA_HEADS, GLA_DV))).reshape(B * T, -1)

        f = _group_fields(zs, 'hgrn', B, T)
        hq, hf, hi, hg = f['hq'], f['hf'], f['hi'], f['hg']
        log_f = jnp.logaddexp(jnp.log(lb), jnp.log1p(-lb) + jax.nn.log_sigmoid(hf.astype(jnp.float32)))
        log_f = log_f.reshape(B, T, HGRN_HEADS, HGRN_EXPAND)
        h_q = jax.nn.silu(hq).reshape(B, T, HGRN_HEADS, HGRN_EXPAND)
        h_v = hi.reshape(B, T, HGRN_HEADS, HGRN_EXPAND)
        o, hgrn_state = _gated_linear_attn(h_q, -jnp.expm1(log_f), h_v, log_f, past['hgrn'])
        o_hgrn = (_rmsnorm(o, p['hgrn_gn']) * jax.nn.sigmoid(hg.reshape(B, T, HGRN_HEADS, HGRN_EXPAND))).reshape(B * T, -1)

    nqk = MLSTM_HEADS * MLSTM_DQK
    if past is None:
        o_mlstm, mC, mn, mm = _mlstm_prompt(zs['mlstm'], B, T, p)
        conv_state = mqk[:, -(MLSTM_CONV - 1):]
    else:
        u, conv_state = _causal_conv(mqk, past['conv'], p['m_wconv'], p['m_bconv'])
        m_q = u[..., :nqk].reshape(B, T, MLSTM_HEADS, MLSTM_DQK)
        m_k = u[..., nqk:].reshape(B, T, MLSTM_HEADS, MLSTM_DQK) * (MLSTM_DQK ** -0.5)
        m_v = mv.reshape(B, T, MLSTM_HEADS, MLSTM_DV)
        ig = (mi + p['m_bi']).astype(jnp.float32)
        lf = jax.nn.log_sigmoid((mf + p['m_bf']).astype(jnp.float32))
        hm, mC, mn, mm = _mlstm(m_q, m_k, m_v, ig, lf, past['mC'], past['mn'], past['mm'])
        o_mlstm = (jax.nn.sigmoid(mo).reshape(B, T, MLSTM_HEADS, MLSTM_DV) * _rmsnorm(hm, p['m_gn'])).reshape(B, T, -1)

    obs = [o.reshape(B * T, BRANCH_WIDTH).astype(jnp.bfloat16) for o in (o_nsa, o_gla, o_hgrn, o_mlstm)]
    return obs, (rows_new, win_state, gla_state, hgrn_state, mC, mn, mm, conv_state)


def _layer(x, mod, pos0, past, p, rel_bias, lb, tables=None):
    B, T, _ = x.shape
    m = B * T
    mp = _round_up(m, 16)

    def rows(a):
        a = a.reshape(m, a.shape[-1])
        return a if mp == m else jnp.pad(a, ((0, mp - m), (0, 0)))

    sh1, sc1, gt1, sh2, sc2, gt2 = jnp.split(mod, 6, axis=-1)
    if T > 1:
        gates3 = lambda g: (g[:, None, :], T)
    else:
        gates3 = lambda g: (rows(g[:, None, :])[None], None)
    hb = rows(_normmod(x, p['g_mix'], sc1, sh1))
    zs = {g: _proj(hb, p['w_' + g], name="proj_" + g) for g in ('nsa', 'gla', 'hgrn', 'mlstm')}
    gates = _proj(hb, p['w_gate'], out_dtype=jnp.bfloat16, act="sigmoid", name="proj_gate")
    obs, st = _token_mixers({g: z[:m] for g, z in zs.items()}, B, T, pos0, past, p, rel_bias, lb, tables)
    merged = _merge([rows(o) for o in obs], p['w_branch'], gates)
    g3, rpg = gates3(gt1)
    x1 = _res32(merged, p['w_out'], p['layer'], rows(x), g3, rows_per_gate=rpg, n_k=1, name="mm_out")
    hb2 = rows(_normmod(x1[:m].reshape(B, T, D_MODEL), p['g_ffn'], sc2, sh2))
    act = _swiglu32(hb2, p['w_ffn_in'], p['layer'])
    g3, rpg = gates3(gt2)
    x2 = _mm_res(act, p['w_ffn_out_bf'], x1, g3, rows_per_gate=rpg, bk=act.shape[1] // 2, name="mm_ffn_out")
    return x2[:m].reshape(B, T, D_MODEL), st


def kernel(x_prompt, x_sample, cache_nsa_kv, state_nsa_win, state_gla, state_hgrn, state_mlstm_C,
           state_mlstm_n, state_mlstm_m, state_mlstm_conv, page_table, c_prompt, c_sample, rel_bias,
           w_ada, b_ada, g_mix, g_ffn, w_in, nsa_gq, nsa_gk, cmp_pe, cmp_w1, cmp_w2, gla_wa, gla_ba,
           gla_gn, hgrn_lb, hgrn_gn, m_wconv, m_bconv, m_bi, m_bf, m_gn, w_branch, w_out, w_ffn_in, w_ffn_out):
    lb_cum = jnp.cumsum(jax.nn.softmax(hgrn_lb.astype(jnp.float32), axis=0), axis=0)
    lb_all = lb_cum - lb_cum[:1]
    dec_b, n_pages = page_table.shape
    past_len = n_pages * PAGE_SIZE
    n_prompt = c_prompt.shape[0]
    x_p, x_s = x_prompt, x_sample
    st_prompt, st_sample = [], []
    c_all = jax.nn.silu(jnp.concatenate([c_prompt, c_sample], axis=0))
    n_c = c_all.shape[0]
    c_act = jnp.pad(c_all, ((0, _round_up(n_c, 16) - n_c), (0, 0))).astype(jnp.bfloat16)
    tables = _bias_tables(rel_bias, x_prompt.shape[1])
    w_ffn_out_bf = w_ffn_out.astype(jnp.bfloat16)
    n_pool = cache_nsa_kv.shape[1]
    cache_all =cache_nsa_kv.reshape((DEPTH * n_pool,) + cache_nsa_kv.shape[2:])
    for l in range(DEPTH):
        p = {'g_mix': g_mix[l], 'g_ffn': g_ffn[l], 'layer': l,
             'nsa_gq': nsa_gq[l], 'nsa_gk': nsa_gk[l], 'cmp_pe': cmp_pe[l], 'cmp_w1': cmp_w1[l],
             'cmp_w2': cmp_w2[l], 'gla_wa': gla_wa[l], 'gla_ba': gla_ba[l], 'gla_gn': gla_gn[l],
             'hgrn_gn': hgrn_gn[l], 'm_wconv': m_wconv[l], 'm_bconv': m_bconv[l], 'm_bi': m_bi[l],
             'm_bf': m_bf[l], 'm_gn': m_gn[l],
             'w_branch': w_branch[l].astype(jnp.bfloat16),
             'w_out': w_out, 'w_ffn_in': w_ffn_in, 'w_ffn_out_bf': w_ffn_out_bf[l]}
        p.update({'w_' + g: w for g, w in _group_weights(w_in[l]).items()})
        mod = _ada_mod(c_act, w_ada, b_ada, l)[:n_c]
        x_p, sp = _layer(x_p, mod[:n_prompt], 0, None, p, rel_bias, lb_all[l], tables)
        past = {'cache': cache_all, 'layer_off': l * n_pool, 'page_table': page_table,
                'win': state_nsa_win[l], 'gla': state_gla[l], 'hgrn': state_hgrn[l],
                'mC': state_mlstm_C[l], 'mn': state_mlstm_n[l], 'mm': state_mlstm_m[l],
                'conv': state_mlstm_conv[l]}
        x_s, ss = _layer(x_s, mod[n_prompt:], past_len, past, p, rel_bias, lb_all[l])
        st_prompt.append(sp)
        st_sample.append(ss)

    def stk(sts, i):
        return jnp.stack([s[i] for s in sts], axis=0)

    outs = [x_p, x_s]
    for i in range(8):
        outs.append(stk(st_prompt, i))
        outs.append(stk(st_sample, i))
    return tuple(outs)
```
